```python
import jax
import jax.numpy as jnp
from jax import lax
import numpy as np

D_MODEL = 2048
BATCH = 4
SEQ = 4096
DEPTH = 1

N_Q_HEADS = 32
N_KV_HEADS = 4
HEAD_DIM = 64
Q_WIDTH = N_Q_HEADS * HEAD_DIM
KV_WIDTH = N_KV_HEADS * HEAD_DIM
WINDOW = 128
LRU_WIDTH = D_MODEL
LRU_BLOCKS = 8
LRU_BLOCK_WIDTH = LRU_WIDTH // LRU_BLOCKS
CONV_WIDTH = 4
LRU_C = 8.0
IN_SPLITS = (Q_WIDTH, KV_WIDTH, KV_WIDTH, LRU_WIDTH, LRU_WIDTH, D_MODEL, D_MODEL)
IN_WIDTH = sum(IN_SPLITS)
IN_CUTS = [sum(IN_SPLITS[:i + 1]) for i in range(len(IN_SPLITS) - 1)]
N_EXPERTS = 32
TOP_K = 4
D_EXPERT = D_MODEL
SWIGLU_LIMIT = 7.0
SWIGLU_ALPHA = 1.702
EXPERT_BLOCK = 256
PLE_DIM = 256
RMS_EPS = 1e-6

kernel_name = 'hybrid_swa_sink_rglru_moe_block'


def rmsnorm(x, g):
    xf = x.astype(jnp.float32)
    var = jnp.mean(xf * xf, axis=-1, keepdims=True)
    return (xf * lax.rsqrt(var + RMS_EPS)).astype(x.dtype) * g


def sliding_window_attention(q, k, v, sinks):
    b, s, _ = q.shape
    nb = s // WINDOW
    g = N_Q_HEADS // N_KV_HEADS
    qb = q.reshape(b, nb, WINDOW, N_KV_HEADS, g, HEAD_DIM)

    def band(t):
        tb = t.reshape(b, nb, WINDOW, N_KV_HEADS, HEAD_DIM)
        prev = jnp.pad(tb[:, :-1], ((0, 0), (1, 0), (0, 0), (0, 0), (0, 0)))
        return jnp.concatenate([prev, tb], axis=2)

    kk, vv = band(k), band(v)
    scores = jnp.einsum('bnqkgd,bnjkd->bnkgqj', qb, kk).astype(jnp.float32) * (HEAD_DIM ** -0.5)
    i = jnp.arange(WINDOW)[:, None]
    j = jnp.arange(2 * WINDOW)[None, :]
    in_band = (j > i) & (j <= i + WINDOW)
    has_prev = (jnp.arange(nb) > 0)[:, None, None] | (j >= WINDOW)[None]
    mask = in_band[None] & has_prev
    scores = jnp.where(mask[None, :, None, None], scores, jnp.finfo(jnp.float32).min)
    sink = jnp.broadcast_to(sinks.astype(jnp.float32).reshape(1, 1, N_KV_HEADS, g, 1, 1),
                            scores.shape[:-1] + (1,))
    probs = jax.nn.softmax(jnp.concatenate([scores, sink], axis=-1), axis=-1)[..., :-1]
    out = jnp.einsum('bnkgqj,bnjkd->bnqkgd', probs.astype(v.dtype), vv)
    return out.reshape(b, s, Q_WIDTH)


def rglru_branch(u, gate, conv_w, conv_b, w_rg_a, b_rg_a, w_rg_x, b_rg_x, lam):
    b, s, w = u.shape
    upad = jnp.pad(u, ((0, 0), (CONV_WIDTH - 1, 0), (0, 0)))
    uc = conv_b + upad[:, 0:s] * conv_w[0]
    for tap in range(1, CONV_WIDTH):
        uc = uc + upad[:, tap:tap + s] * conv_w[tap]
    ub = uc.reshape(b, s, LRU_BLOCKS, LRU_BLOCK_WIDTH)
    r = jax.nn.sigmoid(jnp.einsum('bsnc,ncd->bsnd', ub, w_rg_a) + b_rg_a).reshape(b, s, w)
    ig = jax.nn.sigmoid(jnp.einsum('bsnc,ncd->bsnd', ub, w_rg_x) + b_rg_x).reshape(b, s, w)
    log_a = (-LRU_C * r.astype(jnp.float32)) * jax.nn.softplus(-lam.astype(jnp.float32))
    a = jnp.exp(log_a)
    mult = jnp.sqrt(-jnp.expm1(2.0 * log_a))
    mult = jnp.where((jnp.arange(s) == 0)[None, :, None], 1.0, mult)
    xin = mult * (ig * uc).astype(jnp.float32)

    def combine(left, right):
        a_l, b_l = left
        a_r, b_r = right
        return a_l * a_r, a_r * b_l + b_r

    _, h = lax.associative_scan(combine, (a, xin), axis=1)
    return h.astype(u.dtype) * jax.nn.gelu(gate)


def moe(xn, w_router, b_router, w1, b1, w2, b2):
    b, s, d = xn.shape
    n = b * s
    xt = xn.reshape(n, d)
    logits = (xt @ w_router + b_router).astype(jnp.float32)
    top_val, top_idx = lax.top_k(logits, TOP_K)
    top_w = jax.nn.softmax(top_val, axis=-1).astype(xn.dtype)
    nk = n * TOP_K
    e_flat = top_idx.reshape(nk).astype(jnp.int32)
    tok_flat = jnp.arange(nk, dtype=jnp.int32) // TOP_K
    w_flat = top_w.reshape(nk)
    order = jnp.argsort(e_flat)
    e_sorted = e_flat[order]
    counts = jnp.bincount(e_flat, length=N_EXPERTS).astype(jnp.int32)
    padded = (counts + EXPERT_BLOCK - 1) // EXPERT_BLOCK * EXPERT_BLOCK
    start = jnp.cumsum(counts) - counts
    pend = jnp.cumsum(padded)
    pstart = pend - padded
    dest = pstart[e_sorted] + jnp.arange(nk, dtype=jnp.int32) - start[e_sorted]
    n_rows = (-(-nk // EXPERT_BLOCK) + N_EXPERTS) * EXPERT_BLOCK
    row_tok = jnp.full((n_rows,), n, jnp.int32).at[dest].set(tok_flat[order])
    row_w = jnp.zeros((n_rows,), xn.dtype).at[dest].set(w_flat[order])
    n_blk = n_rows // EXPERT_BLOCK
    blk_start = jnp.arange(n_blk, dtype=jnp.int32) * EXPERT_BLOCK
    blk_expert = jnp.minimum(jnp.searchsorted(pend, blk_start, side='right'), N_EXPERTS - 1)
    x_rows = jnp.concatenate([xt, jnp.zeros((1, d), xt.dtype)], axis=0)[row_tok]
    x_rows = x_rows.reshape(n_blk, EXPERT_BLOCK, d)

    def expert_block(args):
        xb, e = args
        hmid = xb @ w1[e] + b1[e]
        glu, lin = jnp.split(hmid, 2, axis=-1)
        glu = jnp.minimum(glu, SWIGLU_LIMIT)
        lin = jnp.clip(lin, -SWIGLU_LIMIT, SWIGLU_LIMIT)
        act = glu * jax.nn.sigmoid(SWIGLU_ALPHA * glu) * (lin + 1.0)
        return act @ w2[e] + b2[e]

    y_rows = lax.map(expert_block, (x_rows, blk_expert)).reshape(n_rows, d)
    y = jax.ops.segment_sum(y_rows * row_w[:, None], row_tok, num_segments=n + 1)[:n]
    return y.reshape(b, s, d)


def setup_inputs(seed: int = 0) -> dict:
    key = jax.random.key(seed)
    ks = jax.random.split(key, 32)
    L, D = DEPTH, D_MODEL

    def nrm(k, shape, scale):
        return jax.random.normal(k, shape, jnp.float32) * scale

    a_c = jax.random.uniform(ks[10], (L, LRU_WIDTH), jnp.float32, 0.9, 0.999)
    s_base = a_c ** (1.0 / LRU_C)
    lam = jnp.log(s_base) - jnp.log1p(-s_base)
    return {
        'x': nrm(ks[0], (BATCH, SEQ, D), 1.0),
        'p': nrm(ks[1], (L, BATCH, SEQ, PLE_DIM), 1.0),
        'norm_mix_g': 1.0 + nrm(ks[2], (L, D), 0.01),
        'w_in': nrm(ks[3], (L, D, IN_WIDTH), D ** -0.5),
        'b_in': nrm(ks[4], (L, IN_WIDTH), 0.01),
        'conv_w': nrm(ks[5], (L, CONV_WIDTH, LRU_WIDTH), CONV_WIDTH ** -0.5),
        'conv_b': nrm(ks[6], (L, LRU_WIDTH), 0.01),
        'w_rg_a': nrm(ks[7], (L, LRU_BLOCKS, LRU_BLOCK_WIDTH, LRU_BLOCK_WIDTH), LRU_BLOCK_WIDTH ** -0.5),
        'b_rg_a': nrm(ks[8], (L, LRU_BLOCKS, LRU_BLOCK_WIDTH), 0.01),
        'w_rg_x': nrm(ks[9], (L, LRU_BLOCKS, LRU_BLOCK_WIDTH, LRU_BLOCK_WIDTH), LRU_BLOCK_WIDTH ** -0.5),
        'b_rg_x': nrm(ks[11], (L, LRU_BLOCKS, LRU_BLOCK_WIDTH), 0.01),
        'lru_lambda': lam,
        'attn_sinks': nrm(ks[12], (L, N_Q_HEADS), 1.0),
        'w_attn_proj': nrm(ks[13], (L, Q_WIDTH, D), Q_WIDTH ** -0.5),
        'w_lru_proj': nrm(ks[14], (L, LRU_WIDTH, D), LRU_WIDTH ** -0.5),
        'w_out': nrm(ks[15], (L, D, D), D ** -0.5),
        'norm_ffn_g': 1.0 + nrm(ks[16], (L, D), 0.01),
        'w_router': nrm(ks[17], (L, D, N_EXPERTS), D ** -0.5),
        'b_router': nrm(ks[18], (L, N_EXPERTS), 0.01),
        'w_mlp1': nrm(ks[19], (L, N_EXPERTS, D, 2 * D_EXPERT), D ** -0.5),
        'b_mlp1': nrm(ks[20], (L, N_EXPERTS, 2 * D_EXPERT), 0.01),
        'w_mlp2': nrm(ks[21], (L, N_EXPERTS, D_EXPERT, D), D_EXPERT ** -0.5),
        'b_mlp2': nrm(ks[22], (L, N_EXPERTS, D), 0.01),
        'norm_ple_g': 1.0 + nrm(ks[23], (L, D), 0.01),
        'w_ple': nrm(ks[24], (L, PLE_DIM, D), PLE_DIM ** -0.5),
        'w_ple_gate': nrm(ks[25], (L, D, D), D ** -0.5),
        'norm_final_g': 1.0 + nrm(ks[26], (D,), 0.01),
    }


def reference(x, p, norm_mix_g, w_in, b_in, conv_w, conv_b, w_rg_a, b_rg_a, w_rg_x, b_rg_x,
              lru_lambda, attn_sinks, w_attn_proj, w_lru_proj, w_out, norm_ffn_g, w_router,
              b_router, w_mlp1, b_mlp1, w_mlp2, b_mlp2, norm_ple_g, w_ple, w_ple_gate,
              norm_final_g):
    h = x
    for l in range(DEPTH):
        xn = rmsnorm(h, norm_mix_g[l])
        z = xn @ w_in[l] + b_in[l]
        q, k, v, u, lru_gate, g_attn, g_lru = jnp.split(z, IN_CUTS, axis=-1)
        y_attn = sliding_window_attention(q, k, v, attn_sinks[l])
        y_lru = rglru_branch(u, lru_gate, conv_w[l], conv_b[l], w_rg_a[l], b_rg_a[l],
                             w_rg_x[l], b_rg_x[l], lru_lambda[l])
        merged = (jax.nn.sigmoid(g_attn) * (y_attn @ w_attn_proj[l])
                  + jax.nn.sigmoid(g_lru) * (y_lru @ w_lru_proj[l]))
        h = h + merged @ w_out[l]
        h = h + moe(rmsnorm(h, norm_ffn_g[l]), w_router[l], b_router[l],
                    w_mlp1[l], b_mlp1[l], w_mlp2[l], b_mlp2[l])
        ple_gate = jax.nn.sigmoid(rmsnorm(h, norm_ple_g[l]) @ w_ple_gate[l])
        h = h + ple_gate * (p[l] @ w_ple[l])
    return rmsnorm(h, norm_final_g)
```

```python
import functools

import jax
import jax.numpy as jnp
from jax import lax
from jax.experimental import pallas as pl
from jax.experimental.pallas import tpu as pltpu

D_MODEL = 2048
N_Q_HEADS = 32
N_KV_HEADS = 4
HEAD_DIM = 64
Q_WIDTH = N_Q_HEADS * HEAD_DIM
KV_WIDTH = N_KV_HEADS * HEAD_DIM
WINDOW = 128
LRU_WIDTH = D_MODEL
LRU_BLOCKS = 8
LRU_BLOCK_WIDTH = LRU_WIDTH // LRU_BLOCKS
CONV_WIDTH = 4
LRU_C = 8.0
IN_WIDTH = Q_WIDTH + 2 * KV_WIDTH + 2 * LRU_WIDTH + 2 * D_MODEL
N_EXPERTS = 32
TOP_K = 4
D_EXPERT = D_MODEL
SWIGLU_LIMIT = 7.0
SWIGLU_ALPHA = 1.702
EXPERT_BLOCK = 256
PLE_DIM = 256
RMS_EPS = 1e-6

VMEM_LIMIT_BYTES = 56 * 1024 * 1024
BF16_SUBLANES = 16

F32 = jnp.float32
BF16 = jnp.bfloat16


def _params(*sem):
    return pltpu.CompilerParams(dimension_semantics=sem, vmem_limit_bytes=VMEM_LIMIT_BYTES)


def _resident(shape, index_map):
    return pl.BlockSpec(shape, index_map, pipeline_mode=pl.Buffered(1))


def _rms_scale(x):
    var = jnp.mean(x * x, axis=-1, keepdims=True)
    return x * lax.rsqrt(var + RMS_EPS)


def _inproj_body(x_ref, g_ref, w_ref, b_ref, o_ref, xn_ref):
    @pl.when(pl.program_id(1) == 0)
    def _():
        xn_ref[...] = (_rms_scale(x_ref[...]) * g_ref[...]).astype(BF16)

    acc = jnp.dot(xn_ref[...], w_ref[...], preferred_element_type=F32)
    o_ref[...] = (acc + b_ref[...]).astype(o_ref.dtype)


def _inproj(x2, g, w, b, *, tm=1024, tn=1536):
    n, d = x2.shape
    width = w.shape[1]
    return pl.pallas_call(
        _inproj_body,
        grid=(n // tm, width // tn),
        in_specs=[
            pl.BlockSpec((tm, d), lambda i, j: (i, 0)),
            pl.BlockSpec((1, d), lambda i, j: (0, 0)),
            pl.BlockSpec((d, tn), lambda i, j: (0, j)),
            pl.BlockSpec((1, tn), lambda i, j: (0, j)),
        ],
        out_specs=pl.BlockSpec((tm, tn), lambda i, j: (i, j)),
        out_shape=jax.ShapeDtypeStruct((n, width), BF16),
        scratch_shapes=[pltpu.VMEM((tm, d), BF16)],
        compiler_params=_params("arbitrary", "arbitrary"),
        name="inproj",
    )(x2, g, w, b)


def _block_diag_pair(pair, odd, lo_mask):
    swapped = jnp.concatenate([pair[:, HEAD_DIM:], pair[:, :HEAD_DIM]], axis=1)
    zero = jnp.zeros_like(pair)
    if odd:
        lo = jnp.where(lo_mask, swapped, zero)
        hi = jnp.where(lo_mask, zero, pair)
    else:
        lo = jnp.where(lo_mask, pair, zero)
        hi = jnp.where(lo_mask, zero, swapped)
    return jnp.concatenate([lo, hi], axis=0)


def _attn_body(sinks_ref, q_ref, kvp_ref, kvc_ref, o_ref, *, tq):
    nblk = tq // WINDOW
    group = N_Q_HEADS // N_KV_HEADS
    pairs = group // 2
    rows = pairs * WINDOW
    band_w = 2 * WINDOW
    s_idx = pl.program_id(1)

    lo_mask = lax.broadcasted_iota(jnp.int32, (1, 2 * HEAD_DIM), 1) < HEAD_DIM
    qi = lax.broadcasted_iota(jnp.int32, (rows, band_w), 0) % WINDOW
    kj = lax.broadcasted_iota(jnp.int32, (rows, band_w), 1)
    in_band = (kj > qi) & (kj <= qi + WINDOW)
    first_key = jnp.where(s_idx > 0, 0, WINDOW)
    row_pair = lax.broadcasted_iota(jnp.int32, (rows, 1), 0) // WINDOW
    neg = jnp.finfo(F32).min

    for i in range(nblk):
        if i == 0:
            band = jnp.concatenate([kvp_ref[...], kvc_ref[0:WINDOW, :]], axis=0)
            mask = in_band & (kj >= first_key)
        else:
            band = kvc_ref[(i - 1) * WINDOW:(i + 1) * WINDOW, :]
            mask = in_band
        for kh in range(N_KV_HEADS):
            col = (kh // 2) * 2 * HEAD_DIM
            kk = _block_diag_pair(band[:, col:col + 2 * HEAD_DIM], kh % 2 == 1, lo_mask)
            vv = _block_diag_pair(band[:, KV_WIDTH + col:KV_WIDTH + col + 2 * HEAD_DIM], kh % 2 == 1, lo_mask)
            qbase = kh * group * HEAD_DIM
            q4 = jnp.concatenate(
                [q_ref[i * WINDOW:(i + 1) * WINDOW, qbase + 2 * HEAD_DIM * j:qbase + 2 * HEAD_DIM * (j + 1)]
                 for j in range(pairs)], axis=0)
            s = lax.dot_general(q4, kk, (((1,), (1,)), ((), ())), preferred_element_type=F32)
            s = s * (HEAD_DIM ** -0.5)
            sink_lo = jnp.zeros((rows, 1), F32)
            sink_hi = jnp.zeros((rows, 1), F32)
            for j in range(pairs):
                sink_lo = jnp.where(row_pair == j, sinks_ref[kh * group + 2 * j], sink_lo)
                sink_hi = jnp.where(row_pair == j, sinks_ref[kh * group + 2 * j + 1], sink_hi)
            halves = []
            inv = []
            for half, sink in ((0, sink_lo), (1, sink_hi)):
                sh = jnp.where(mask, s[:, half * band_w:(half + 1) * band_w], neg)
                m = jnp.maximum(jnp.max(sh, axis=1, keepdims=True), sink)
                p = jnp.exp(sh - m)
                denom = jnp.sum(p, axis=1, keepdims=True) + jnp.exp(sink - m)
                halves.append(p.astype(BF16))
                inv.append(1.0 / denom)
            pcat = jnp.concatenate(halves, axis=1)
            o = jnp.dot(pcat, vv, preferred_element_type=F32)
            o = o * jnp.where(lo_mask, inv[0], inv[1])
            for j in range(pairs):
                o_ref[i * WINDOW:(i + 1) * WINDOW, qbase + 2 * HEAD_DIM * j:qbase + 2 * HEAD_DIM * (j + 1)] = (
                    o[j * WINDOW:(j + 1) * WINDOW, :].astype(o_ref.dtype))


def _attention(z, sinks, *, batch, seq, q_col, kv_col, tq=512):
    n = z.shape[0]
    spb = seq // tq
    wpb = seq // WINDOW
    kvw = 2 * KV_WIDTH

    def prev_map(b, s, sinks_ref):
        return (jnp.maximum(b * wpb + s * (tq // WINDOW) - 1, 0), kv_col)

    return pl.pallas_call(
        functools.partial(_attn_body, tq=tq),
        grid_spec=pltpu.PrefetchScalarGridSpec(
            num_scalar_prefetch=1,
            grid=(batch, spb),
            in_specs=[
                pl.BlockSpec((tq, Q_WIDTH), lambda b, s, sinks_ref: (b * spb + s, q_col)),
                pl.BlockSpec((WINDOW, kvw), prev_map),
                pl.BlockSpec((tq, kvw), lambda b, s, sinks_ref: (b * spb + s, kv_col)),
            ],
            out_specs=pl.BlockSpec((tq, Q_WIDTH), lambda b, s, sinks_ref: (b * spb + s, 0)),
        ),
        out_shape=jax.ShapeDtypeStruct((n, Q_WIDTH), BF16),
        compiler_params=_params("arbitrary", "arbitrary"),
        name="attn",
    )(sinks, z, z, z)


def _gelu_tanh(x):
    c = (2.0 / jnp.pi) ** 0.5
    return x * (0.5 * (1.0 + jnp.tanh(c * (x + 0.044715 * (x * x * x)))))


def _lru_body(u_ref, up_ref, gate_ref, cw_ref, cb_ref, wrg_ref, brg_ref, lam_ref, y_ref,
              ext_ref, a_ref, b_ref, h_ref):
    s_idx = pl.program_id(2)
    tl, w = u_ref.shape
    pad = 8

    @pl.when(s_idx == 0)
    def _():
        ext_ref[0:pad, :] = jnp.zeros((pad, w), F32)
        h_ref[...] = jnp.zeros_like(h_ref)

    @pl.when(s_idx > 0)
    def _():
        ext_ref[0:pad, :] = up_ref[...].astype(F32)[BF16_SUBLANES - pad:BF16_SUBLANES, :]

    ext_ref[pad:pad + tl, :] = u_ref[...].astype(F32)
    first = pad - (CONV_WIDTH - 1)
    uc = cb_ref[...] + ext_ref[first:first + tl, :] * cw_ref[0:1, :]
    for tap in range(1, CONV_WIDTH):
        uc = uc + ext_ref[first + tap:first + tap + tl, :] * cw_ref[tap:tap + 1, :]

    gates = jnp.dot(uc.astype(BF16), wrg_ref[...], preferred_element_type=F32) + brg_ref[...]
    r = jax.nn.sigmoid(gates[:, :w])
    ig = jax.nn.sigmoid(gates[:, w:])
    neg_lam = -lam_ref[...]
    softplus = jnp.maximum(neg_lam, 0.0) + jnp.log1p(jnp.exp(-jnp.abs(neg_lam)))
    log_a = (-LRU_C * r) * softplus
    a = jnp.exp(log_a)
    mult = jnp.sqrt(-jnp.tanh(log_a) * (a * a + 1.0))
    row = lax.broadcasted_iota(jnp.int32, (tl, 1), 0)
    mult = jnp.where((row == 0) & (s_idx == 0), 1.0, mult)
    b = mult * (ig * uc)

    sub = row % 8
    for d in (1, 2, 4):
        keep = sub >= d
        b_prev = jnp.where(keep, pltpu.roll(b, d, 0), 0.0)
        a_prev = jnp.where(keep, pltpu.roll(a, d, 0), 1.0)
        b = b + a * b_prev
        a = a * a_prev
    a_ref[...] = a
    b_ref[...] = b

    def group_step(gi, h):
        off = pl.multiple_of(gi * 8, 8)
        hg = b_ref[pl.ds(off, 8), :] + a_ref[pl.ds(off, 8), :] * h
        b_ref[pl.ds(off, 8), :] = hg
        return hg[7:8, :]

    h_ref[...] = lax.fori_loop(0, tl // 8, group_step, h_ref[...])
    y_ref[...] = (b_ref[...] * _gelu_tanh(gate_ref[...].astype(F32))).astype(y_ref.dtype)


def _rglru(z, conv_w, conv_b, w_rg, b_rg, lam, *, batch, seq, u_col, gate_col, tl=512):
    n = z.shape[0]
    w = LRU_BLOCK_WIDTH
    spb = seq // tl
    rpb = seq // BF16_SUBLANES

    def prev_map(b, c, s):
        return (jnp.maximum(b * rpb + s * (tl // BF16_SUBLANES) - 1, 0), u_col + c)

    return pl.pallas_call(
        _lru_body,
        grid=(batch, LRU_BLOCKS, spb),
        in_specs=[
            pl.BlockSpec((tl, w), lambda b, c, s: (b * spb + s, u_col + c)),
            pl.BlockSpec((BF16_SUBLANES, w), prev_map),
            pl.BlockSpec((tl, w), lambda b, c, s: (b * spb + s, gate_col + c)),
            pl.BlockSpec((CONV_WIDTH, w), lambda b, c, s: (0, c)),
            pl.BlockSpec((1, w), lambda b, c, s: (0, c)),
            pl.BlockSpec((None, w, 2 * w), lambda b, c, s: (c, 0, 0)),
            pl.BlockSpec((None, 1, 2 * w), lambda b, c, s: (c, 0, 0)),
            pl.BlockSpec((1, w), lambda b, c, s: (0, c)),
        ],
        out_specs=pl.BlockSpec((tl, w), lambda b, c, s: (b * spb + s, c)),
        out_shape=jax.ShapeDtypeStruct((n, LRU_WIDTH), BF16),
        scratch_shapes=[
            pltpu.VMEM((tl + 8, w), F32),
            pltpu.VMEM((tl, w), F32),
            pltpu.VMEM((tl, w), F32),
            pltpu.VMEM((1, w), F32),
        ],
        compiler_params=_params("arbitrary", "arbitrary", "arbitrary"),
        name="rglru",
    )(z, z, z, conv_w, conv_b, w_rg, b_rg, lam)


def _merge_body(ya_ref, yl_ref, ga_ref, gl_ref, pa_ref, pl_ref, o_ref):
    pa = jnp.dot(ya_ref[...], pa_ref[...], preferred_element_type=F32)
    pr = jnp.dot(yl_ref[...], pl_ref[...], preferred_element_type=F32)
    merged = jax.nn.sigmoid(ga_ref[...].astype(F32)) * pa + jax.nn.sigmoid(gl_ref[...].astype(F32)) * pr
    o_ref[...] = merged.astype(o_ref.dtype)


def _merge(y_attn, y_lru, z, w_attn_proj, w_lru_proj, *, ga_col, gl_col, tm=512):
    n, d = y_attn.shape
    return pl.pallas_call(
        _merge_body,
        grid=(n // tm,),
        in_specs=[
            pl.BlockSpec((tm, d), lambda i: (i, 0)),
            pl.BlockSpec((tm, d), lambda i: (i, 0)),
            pl.BlockSpec((tm, d), lambda i: (i, ga_col)),
            pl.BlockSpec((tm, d), lambda i: (i, gl_col)),
            _resident((d, d), lambda i: (0, 0)),
            _resident((d, d), lambda i: (0, 0)),
        ],
        out_specs=pl.BlockSpec((tm, d), lambda i: (i, 0)),
        out_shape=jax.ShapeDtypeStruct((n, d), BF16),
        compiler_params=_params("arbitrary"),
        name="merge",
    )(y_attn, y_lru, z, z, w_attn_proj, w_lru_proj)


def _outproj_body(x_ref, m_ref, wo_ref, g_ref, wr_ref, br_ref, h_ref, xn_ref, idx_ref, wt_ref):
    h = x_ref[...] + jnp.dot(m_ref[...], wo_ref[...], preferred_element_type=F32)
    h_ref[...] = h
    xn = _rms_scale(h) * g_ref[...]
    xn_ref[...] = xn
    logits = lax.dot_general(wr_ref[...], xn.astype(BF16), (((1,), (1,)), ((), ())),
                             preferred_element_type=F32) + br_ref[...]
    expert = lax.broadcasted_iota(jnp.int32, logits.shape, 0).astype(F32)
    vals = []
    for k in range(TOP_K):
        m = jnp.max(logits, axis=0, keepdims=True)
        sel = jnp.min(jnp.where(logits == m, expert, float(N_EXPERTS)), axis=0, keepdims=True)
        vals.append(m)
        idx_ref[k:k + 1, :] = sel.astype(jnp.int32)
        logits = jnp.where(expert == sel, -jnp.inf, logits)
    exps = [jnp.exp(v - vals[0]) for v in vals]
    denom = exps[0] + exps[1] + exps[2] + exps[3]
    for k in range(TOP_K):
        wt_ref[k:k + 1, :] = exps[k] / denom


def _outproj_router(x2, merged, w_out, g_ffn, w_router_t, b_router, *, tm=512):
    n, d = x2.shape
    return pl.pallas_call(
        _outproj_body,
        grid=(n // tm,),
        in_specs=[
            pl.BlockSpec((tm, d), lambda i: (i, 0)),
            pl.BlockSpec((tm, d), lambda i: (i, 0)),
            _resident((d, d), lambda i: (0, 0)),
            pl.BlockSpec((1, d), lambda i: (0, 0)),
            pl.BlockSpec((N_EXPERTS, d), lambda i: (0, 0)),
            pl.BlockSpec((N_EXPERTS, 1), lambda i: (0, 0)),
        ],
        out_specs=[
            pl.BlockSpec((tm, d), lambda i: (i, 0)),
            pl.BlockSpec((tm, d), lambda i: (i, 0)),
            pl.BlockSpec((TOP_K, tm), lambda i: (0, i)),
            pl.BlockSpec((TOP_K, tm), lambda i: (0, i)),
        ],
        out_shape=[
            jax.ShapeDtypeStruct((n, d), F32),
            jax.ShapeDtypeStruct((n, d), F32),
            jax.ShapeDtypeStruct((TOP_K, n), jnp.int32),
            jax.ShapeDtypeStruct((TOP_K, n), F32),
        ],
        compiler_params=_params("arbitrary"),
        name="outproj_router",
    )(x2, merged, w_out, g_ffn, w_router_t, b_router)


def _row_copy(src_hbm, buf, sem, src_row, dst_row):
    return pltpu.make_async_copy(src_hbm.at[pl.ds(src_row, 1), :], buf.at[pl.ds(dst_row, 1), :], sem)


def _gather_chunk(idx_ref, src_hbm, buf, sem, count):
    for r in range(count):
        _row_copy(src_hbm, buf, sem, idx_ref[0, r], r).start()
    for r in range(count):
        _row_copy(src_hbm, buf, sem, 0, r).wait()


def _gather_body(idx_ref, src_hbm, o_ref, buf, sem):
    _gather_chunk(idx_ref, src_hbm, buf, sem, buf.shape[0])
    o_ref[...] = buf[...].astype(o_ref.dtype)


def _gather_rows(src, row_idx, *, chunk=256):
    n_out = row_idx.shape[0]
    d = src.shape[1]
    idx3 = row_idx.reshape(n_out // chunk, 1, chunk)
    return pl.pallas_call(
        _gather_body,
        grid=(n_out // chunk,),
        in_specs=[
            pl.BlockSpec((None, 1, chunk), lambda i: (i, 0, 0), memory_space=pltpu.SMEM),
            pl.BlockSpec(memory_space=pl.ANY),
        ],
        out_specs=pl.BlockSpec((chunk, d), lambda i: (i, 0)),
        out_shape=jax.ShapeDtypeStruct((n_out, d), BF16),
        scratch_shapes=[pltpu.VMEM((chunk, d), src.dtype), pltpu.SemaphoreType.DMA(())],
        compiler_params=_params("arbitrary"),
        name="gather_rows",
    )(idx3, src)


def _combine_body(idx_ref, wt_ref, src_hbm, o_ref, buf, sem):
    tm = o_ref.shape[0]
    _gather_chunk(idx_ref, src_hbm, buf, sem, TOP_K * tm)
    acc = wt_ref[:, 0:1] * buf[0:tm, :]
    for k in range(1, TOP_K):
        acc = acc + wt_ref[:, k:k + 1] * buf[k * tm:(k + 1) * tm, :]
    o_ref[...] = acc


def _combine(y_sorted, pos, weights, *, tm=64):
    n = pos.shape[0]
    d = y_sorted.shape[1]
    idx3 = pos.reshape(n // tm, tm, TOP_K).transpose(0, 2, 1).reshape(n // tm, 1, TOP_K * tm)
    return pl.pallas_call(
        _combine_body,
        grid=(n // tm,),
        in_specs=[
            pl.BlockSpec((None, 1, TOP_K * tm), lambda i: (i, 0, 0), memory_space=pltpu.SMEM),
            pl.BlockSpec((tm, TOP_K), lambda i: (i, 0)),
            pl.BlockSpec(memory_space=pl.ANY),
        ],
        out_specs=pl.BlockSpec((tm, d), lambda i: (i, 0)),
        out_shape=jax.ShapeDtypeStruct((n, d), F32),
        scratch_shapes=[pltpu.VMEM((TOP_K * tm, d), y_sorted.dtype), pltpu.SemaphoreType.DMA(())],
        compiler_params=_params("arbitrary"),
        name="combine",
    )(idx3, weights, y_sorted)


def _mlp1_body(be_ref, nu_ref, x_ref, w_ref, b_ref, o_ref):
    @pl.when(pl.program_id(0) < nu_ref[0])
    def _():
        h = jnp.dot(x_ref[...], w_ref[...], preferred_element_type=F32) + b_ref[...]
        f = h.shape[1] // 2
        glu = jnp.minimum(h[:, :f], SWIGLU_LIMIT)
        lin = jnp.clip(h[:, f:], -SWIGLU_LIMIT, SWIGLU_LIMIT)
        act = glu * jax.nn.sigmoid(SWIGLU_ALPHA * glu) * (lin + 1.0)
        o_ref[...] = act.astype(o_ref.dtype)

    @pl.when(pl.program_id(0) >= nu_ref[0])
    def _():
        o_ref[...] = jnp.zeros_like(o_ref)


def _mlp2_body(be_ref, nu_ref, a_ref, w_ref, b_ref, o_ref):
    @pl.when(pl.program_id(0) < nu_ref[0])
    def _():
        o_ref[...] = jnp.dot(a_ref[...], w_ref[...], preferred_element_type=F32) + b_ref[...]

    @pl.when(pl.program_id(0) >= nu_ref[0])
    def _():
        o_ref[...] = jnp.zeros_like(o_ref)


def _expert_layer(body, rows, w, b, blk_expert, n_used, d_out, out_dtype, name):
    n_rows, d_in = rows.shape
    n_blk = n_rows // EXPERT_BLOCK

    def row_map(i, be, nu):
        return (jnp.minimum(i, nu[0] - 1), 0)

    return pl.pallas_call(
        body,
        grid_spec=pltpu.PrefetchScalarGridSpec(
            num_scalar_prefetch=2,
            grid=(n_blk,),
            in_specs=[
                pl.BlockSpec((EXPERT_BLOCK, d_in), row_map),
                pl.BlockSpec((None, d_in, w.shape[2]), lambda i, be, nu: (be[i], 0, 0)),
                pl.BlockSpec((None, 1, w.shape[2]), lambda i, be, nu: (be[i], 0, 0)),
            ],
            out_specs=pl.BlockSpec((EXPERT_BLOCK, d_out), lambda i, be, nu: (i, 0)),
        ),
        out_shape=jax.ShapeDtypeStruct((n_rows, d_out), out_dtype),
        compiler_params=_params("arbitrary"),
        name=name,
    )(blk_expert, n_used, rows, w, b)


def _final_body(h_ref, moe_ref, p_ref, gp_ref, wg_ref, wp_ref, gf_ref, o_ref, *, final_norm):
    h = h_ref[...] + moe_ref[...]
    xn = (_rms_scale(h) * gp_ref[...]).astype(BF16)
    gate = jax.nn.sigmoid(jnp.dot(xn, wg_ref[...], preferred_element_type=F32))
    ple = jnp.dot(p_ref[...].astype(BF16), wp_ref[...], preferred_element_type=F32)
    h = h + gate * ple
    o_ref[...] = _rms_scale(h) * gf_ref[...] if final_norm else h


def _final(h1, moe, p2, g_ple, w_ple_gate, w_ple, g_final, *, final_norm, tm=512):
    n, d = h1.shape
    return pl.pallas_call(
        functools.partial(_final_body, final_norm=final_norm),
        grid=(n // tm,),
        in_specs=[
            pl.BlockSpec((tm, d), lambda i: (i, 0)),
            pl.BlockSpec((tm, d), lambda i: (i, 0)),
            pl.BlockSpec((tm, PLE_DIM), lambda i: (i, 0)),
            pl.BlockSpec((1, d), lambda i: (0, 0)),
            _resident((d, d), lambda i: (0, 0)),
            _resident((PLE_DIM, d), lambda i: (0, 0)),
            pl.BlockSpec((1, d), lambda i: (0, 0)),
        ],
        out_specs=pl.BlockSpec((tm, d), lambda i: (i, 0)),
        out_shape=jax.ShapeDtypeStruct((n, d), F32),
        compiler_params=_params("arbitrary"),
        name="final",
    )(h1, moe, p2, g_ple, w_ple_gate, w_ple, g_final)


def _routing_tables(expert_ids, n_tokens):
    nk = n_tokens * TOP_K
    e_flat = expert_ids.T.reshape(nk)
    onehot = (e_flat[:, None] == jnp.arange(N_EXPERTS, dtype=jnp.int32)[None, :]).astype(jnp.int32)
    csum = jnp.cumsum(onehot, axis=0)
    rank = jnp.sum(onehot * csum, axis=1) - 1
    counts = csum[-1]
    padded = (counts + EXPERT_BLOCK - 1) // EXPERT_BLOCK * EXPERT_BLOCK
    pend = jnp.cumsum(padded)
    pstart = pend - padded
    dest = pstart[e_flat] + rank
    n_blk = nk // EXPERT_BLOCK + N_EXPERTS
    n_rows = n_blk * EXPERT_BLOCK
    tok_flat = jnp.arange(nk, dtype=jnp.int32) // TOP_K
    row_tok = jnp.zeros((n_rows,), jnp.int32).at[dest].set(tok_flat)
    n_used = (pend[-1] // EXPERT_BLOCK).astype(jnp.int32)
    blk_start = jnp.arange(n_blk, dtype=jnp.int32) * EXPERT_BLOCK
    blk_expert = jnp.minimum(jnp.searchsorted(pend, blk_start, side='right'), N_EXPERTS - 1).astype(jnp.int32)
    blk_expert = jnp.where(jnp.arange(n_blk) < n_used, blk_expert, blk_expert[n_used - 1])
    return row_tok, dest.reshape(n_tokens, TOP_K).astype(jnp.int32), blk_expert, n_used.reshape(1)


def _layer(h, p_l, norm_mix_g, w_in, b_in, conv_w, conv_b, w_rg_a, b_rg_a, w_rg_x, b_rg_x, lru_lambda,
           attn_sinks, w_attn_proj, w_lru_proj, w_out, norm_ffn_g, w_router, b_router, w_mlp1, b_mlp1,
           w_mlp2, b_mlp2, norm_ple_g, w_ple, w_ple_gate, norm_final_g, *, batch, seq, final_norm):
    n = h.shape[0]
    row = lambda v: v.reshape(1, -1)

    kv0 = Q_WIDTH
    rest0 = Q_WIDTH + 2 * KV_WIDTH
    reorder = lambda a: jnp.concatenate([a[..., :kv0], a[..., rest0:], a[..., kv0:rest0]], axis=-1)
    z = _inproj(h, row(norm_mix_g), reorder(w_in).astype(BF16), row(reorder(b_in)))

    kv_col = (Q_WIDTH + 2 * LRU_WIDTH + 2 * D_MODEL) // (2 * KV_WIDTH)
    y_attn = _attention(z, attn_sinks, batch=batch, seq=seq, q_col=0, kv_col=kv_col)

    w_rg = jnp.concatenate([w_rg_a, w_rg_x], axis=-1).astype(BF16)
    b_rg = jnp.concatenate([b_rg_a, b_rg_x], axis=-1).reshape(LRU_BLOCKS, 1, 2 * LRU_BLOCK_WIDTH)
    y_lru = _rglru(z, conv_w, row(conv_b), w_rg, b_rg, row(lru_lambda), batch=batch, seq=seq,
                   u_col=Q_WIDTH // LRU_BLOCK_WIDTH, gate_col=(Q_WIDTH + LRU_WIDTH) // LRU_BLOCK_WIDTH)

    merged = _merge(y_attn, y_lru, z, w_attn_proj.astype(BF16), w_lru_proj.astype(BF16),
                    ga_col=(Q_WIDTH + 2 * LRU_WIDTH) // D_MODEL, gl_col=(Q_WIDTH + 2 * LRU_WIDTH + D_MODEL) // D_MODEL)
    h1, xn2, expert_ids, expert_w = _outproj_router(
        h, merged, w_out.astype(BF16), row(norm_ffn_g), w_router.T.astype(BF16), b_router.reshape(N_EXPERTS, 1))

    row_tok, pos, blk_expert, n_used = _routing_tables(expert_ids, n)
    x_sorted = _gather_rows(xn2, row_tok)
    act = _expert_layer(_mlp1_body, x_sorted, w_mlp1.astype(BF16), b_mlp1[:, None, :], blk_expert, n_used,
                        D_EXPERT, BF16, "mlp1")
    y_sorted = _expert_layer(_mlp2_body, act, w_mlp2.astype(BF16), b_mlp2[:, None, :], blk_expert, n_used,
                             D_MODEL, F32, "mlp2")
    moe = _combine(y_sorted, pos, expert_w.T)

    return _final(h1, moe, p_l, row(norm_ple_g), w_ple_gate.astype(BF16), w_ple.astype(BF16), row(norm_final_g),
                  final_norm=final_norm)


def kernel(x, p, norm_mix_g, w_in, b_in, conv_w, conv_b, w_rg_a, b_rg_a, w_rg_x, b_rg_x, lru_lambda, attn_sinks, w_attn_proj, w_lru_proj, w_out, norm_ffn_g, w_router, b_router, w_mlp1, b_mlp1, w_mlp2, b_mlp2, norm_ple_g, w_ple, w_ple_gate, norm_final_g):
    batch, seq, d = x.shape
    depth = p.shape[0]
    h = x.reshape(batch * seq, d)
    for l in range(depth):
        h = _layer(h, p[l].reshape(batch * seq, -1), norm_mix_g[l], w_in[l], b_in[l], conv_w[l], conv_b[l],
                   w_rg_a[l], b_rg_a[l], w_rg_x[l], b_rg_x[l], lru_lambda[l], attn_sinks[l], w_attn_proj[l],
                   w_lru_proj[l], w_out[l], norm_ffn_g[l], w_router[l], b_router[l], w_mlp1[l], b_mlp1[l],
                   w_mlp2[l], b_mlp2[l], norm_ple_g[l], w_ple[l], w_ple_gate[l], norm_final_g,
                   batch=batch, seq=seq, final_norm=(l == depth - 1))
    return h.reshape(batch, seq, d)
```

```python
import functools

import jax
import jax.numpy as jnp
from jax import lax
from jax.experimental import pallas as pl
from jax.experimental.pallas import tpu as pltpu

D_MODEL = 2048
N_Q_HEADS = 32
N_KV_HEADS = 4
HEAD_DIM = 64
Q_WIDTH = N_Q_HEADS * HEAD_DIM
KV_WIDTH = N_KV_HEADS * HEAD_DIM
WINDOW = 128
LRU_WIDTH = D_MODEL
LRU_BLOCKS = 8
LRU_BLOCK_WIDTH = LRU_WIDTH // LRU_BLOCKS
CONV_WIDTH = 4
LRU_C = 8.0
IN_WIDTH = Q_WIDTH + 2 * KV_WIDTH + 2 * LRU_WIDTH + 2 * D_MODEL
N_EXPERTS = 32
TOP_K = 4
D_EXPERT = D_MODEL
SWIGLU_LIMIT = 7.0
SWIGLU_ALPHA = 1.702
EXPERT_BLOCK = 256
EXPERT_ITEM = 768
FF_CHUNK = 512
PLE_DIM = 256
RMS_EPS = 1e-6

VMEM_LIMIT_BYTES = 56 * 1024 * 1024
BF16_SUBLANES = 16

F32 = jnp.float32
BF16 = jnp.bfloat16


def _params(*sem):
    return pltpu.CompilerParams(dimension_semantics=sem, vmem_limit_bytes=VMEM_LIMIT_BYTES)


def _resident(shape, index_map):
    return pl.BlockSpec(shape, index_map, pipeline_mode=pl.Buffered(1))


def _rms_scale(x):
    var = jnp.mean(x * x, axis=-1, keepdims=True)
    return x * lax.rsqrt(var + RMS_EPS)


def _inproj_body(x_ref, g_ref, w_ref, b_ref, o_ref, xn_ref):
    @pl.when(pl.program_id(1) == 0)
    def _():
        xn_ref[...] = (_rms_scale(x_ref[...]) * g_ref[...]).astype(BF16)

    acc = jnp.dot(xn_ref[...], w_ref[...], preferred_element_type=F32)
    o_ref[...] = (acc + b_ref[...]).astype(o_ref.dtype)


def _inproj(x2, g, w, b, *, tm=1024, tn=1536):
    n, d = x2.shape
    width = w.shape[1]
    return pl.pallas_call(
        _inproj_body,
        grid=(n // tm, width // tn),
        in_specs=[
            pl.BlockSpec((tm, d), lambda i, j: (i, 0)),
            pl.BlockSpec((1, d), lambda i, j: (0, 0)),
            pl.BlockSpec((d, tn), lambda i, j: (0, j)),
            pl.BlockSpec((1, tn), lambda i, j: (0, j)),
        ],
        out_specs=pl.BlockSpec((tm, tn), lambda i, j: (i, j)),
        out_shape=jax.ShapeDtypeStruct((n, width), BF16),
        scratch_shapes=[pltpu.VMEM((tm, d), BF16)],
        compiler_params=_params("arbitrary", "arbitrary"),
        name="inproj",
    )(x2, g, w, b)


def _block_diag_pair(pair, odd, lo_mask):
    swapped = jnp.concatenate([pair[:, HEAD_DIM:], pair[:, :HEAD_DIM]], axis=1)
    zero = jnp.zeros_like(pair)
    if odd:
        lo = jnp.where(lo_mask, swapped, zero)
        hi = jnp.where(lo_mask, zero, pair)
    else:
        lo = jnp.where(lo_mask, pair, zero)
        hi = jnp.where(lo_mask, zero, swapped)
    return jnp.concatenate([lo, hi], axis=0)


def _attn_body(sinks_ref, q_ref, kvp_ref, kvc_ref, o_ref, *, tq):
    nblk = tq // WINDOW
    group = N_Q_HEADS // N_KV_HEADS
    pairs = group // 2
    rows = pairs * WINDOW
    band_w = 2 * WINDOW
    s_idx = pl.program_id(1)

    lo_mask = lax.broadcasted_iota(jnp.int32, (1, 2 * HEAD_DIM), 1) < HEAD_DIM
    qi = lax.broadcasted_iota(jnp.int32, (rows, band_w), 0) % WINDOW
    kj = lax.broadcasted_iota(jnp.int32, (rows, band_w), 1)
    in_band = (kj > qi) & (kj <= qi + WINDOW)
    first_key = jnp.where(s_idx > 0, 0, WINDOW)
    row_pair = lax.broadcasted_iota(jnp.int32, (rows, 1), 0) // WINDOW
    neg = jnp.finfo(F32).min

    for i in range(nblk):
        if i == 0:
            band = jnp.concatenate([kvp_ref[...], kvc_ref[0:WINDOW, :]], axis=0)
            mask = in_band & (kj >= first_key)
        else:
            band = kvc_ref[(i - 1) * WINDOW:(i + 1) * WINDOW, :]
            mask = in_band
        for kh in range(N_KV_HEADS):
            col = (kh // 2) * 2 * HEAD_DIM
            kk = _block_diag_pair(band[:, col:col + 2 * HEAD_DIM], kh % 2 == 1, lo_mask)
            vv = _block_diag_pair(band[:, KV_WIDTH + col:KV_WIDTH + col + 2 * HEAD_DIM], kh % 2 == 1, lo_mask)
            qbase = kh * group * HEAD_DIM
            q4 = jnp.concatenate(
                [q_ref[i * WINDOW:(i + 1) * WINDOW, qbase + 2 * HEAD_DIM * j:qbase + 2 * HEAD_DIM * (j + 1)]
                 for j in range(pairs)], axis=0)
            s = lax.dot_general(q4, kk, (((1,), (1,)), ((), ())), preferred_element_type=F32)
            s = s * (HEAD_DIM ** -0.5)
            sink_lo = jnp.zeros((rows, 1), F32)
            sink_hi = jnp.zeros((rows, 1), F32)
            for j in range(pairs):
                sink_lo = jnp.where(row_pair == j, sinks_ref[kh * group + 2 * j], sink_lo)
                sink_hi = jnp.where(row_pair == j, sinks_ref[kh * group + 2 * j + 1], sink_hi)
            halves = []
            inv = []
            for half, sink in ((0, sink_lo), (1, sink_hi)):
                sh = jnp.where(mask, s[:, half * band_w:(half + 1) * band_w], neg)
                m = jnp.maximum(jnp.max(sh, axis=1, keepdims=True), sink)
                p = jnp.exp(sh - m)
                denom = jnp.sum(p, axis=1, keepdims=True) + jnp.exp(sink - m)
                halves.append(p.astype(BF16))
                inv.append(1.0 / denom)
            pcat = jnp.concatenate(halves, axis=1)
            o = jnp.dot(pcat, vv, preferred_element_type=F32)
            o = o * jnp.where(lo_mask, inv[0], inv[1])
            for j in range(pairs):
                o_ref[i * WINDOW:(i + 1) * WINDOW, qbase + 2 * HEAD_DIM * j:qbase + 2 * HEAD_DIM * (j + 1)] = (
                    o[j * WINDOW:(j + 1) * WINDOW, :].astype(o_ref.dtype))


def _attention(z, sinks, *, batch, seq, q_col, kv_col, tq=512):
    n = z.shape[0]
    spb = seq // tq
    wpb = seq // WINDOW
    kvw = 2 * KV_WIDTH

    def prev_map(b, s, sinks_ref):
        return (jnp.maximum(b * wpb + s * (tq // WINDOW) - 1, 0), kv_col)

    return pl.pallas_call(
        functools.partial(_attn_body, tq=tq),
        grid_spec=pltpu.PrefetchScalarGridSpec(
            num_scalar_prefetch=1,
            grid=(batch, spb),
            in_specs=[
                pl.BlockSpec((tq, Q_WIDTH), lambda b, s, sinks_ref: (b * spb + s, q_col)),
                pl.BlockSpec((WINDOW, kvw), prev_map),
                pl.BlockSpec((tq, kvw), lambda b, s, sinks_ref: (b * spb + s, kv_col)),
            ],
            out_specs=pl.BlockSpec((tq, Q_WIDTH), lambda b, s, sinks_ref: (b * spb + s, 0)),
        ),
        out_shape=jax.ShapeDtypeStruct((n, Q_WIDTH), BF16),
        compiler_params=_params("arbitrary", "arbitrary"),
        name="attn",
    )(sinks, z, z, z)


def _gelu_tanh(x):
    c = (2.0 / jnp.pi) ** 0.5
    return x * (0.5 * (1.0 + jnp.tanh(c * (x + 0.044715 * (x * x * x)))))


def _lru_body(u_ref, up_ref, gate_ref, cw_ref, cb_ref, wrg_ref, brg_ref, lam_ref, y_ref,
              ext_ref, a_ref, b_ref, h_ref):
    s_idx = pl.program_id(2)
    tl, w = u_ref.shape
    pad = 8

    @pl.when(s_idx == 0)
    def _():
        ext_ref[0:pad, :] = jnp.zeros((pad, w), F32)
        h_ref[...] = jnp.zeros_like(h_ref)

    @pl.when(s_idx > 0)
    def _():
        ext_ref[0:pad, :] = up_ref[...].astype(F32)[BF16_SUBLANES - pad:BF16_SUBLANES, :]

    ext_ref[pad:pad + tl, :] = u_ref[...].astype(F32)
    first = pad - (CONV_WIDTH - 1)
    uc = cb_ref[...] + ext_ref[first:first + tl, :] * cw_ref[0:1, :]
    for tap in range(1, CONV_WIDTH):
        uc = uc + ext_ref[first + tap:first + tap + tl, :] * cw_ref[tap:tap + 1, :]

    gates = jnp.dot(uc.astype(BF16), wrg_ref[...], preferred_element_type=F32) + brg_ref[...]
    r = jax.nn.sigmoid(gates[:, :w])
    ig = jax.nn.sigmoid(gates[:, w:])
    neg_lam = -lam_ref[...]
    softplus = jnp.maximum(neg_lam, 0.0) + jnp.log1p(jnp.exp(-jnp.abs(neg_lam)))
    log_a = (-LRU_C * r) * softplus
    a = jnp.exp(log_a)
    mult = jnp.sqrt(-jnp.tanh(log_a) * (a * a + 1.0))
    row = lax.broadcasted_iota(jnp.int32, (tl, 1), 0)
    mult = jnp.where((row == 0) & (s_idx == 0), 1.0, mult)
    b = mult * (ig * uc)

    sub = row % 8
    for d in (1, 2, 4):
        keep = sub >= d
        b_prev = jnp.where(keep, pltpu.roll(b, d, 0), 0.0)
        a_prev = jnp.where(keep, pltpu.roll(a, d, 0), 1.0)
        b = b + a * b_prev
        a = a * a_prev
    a_ref[...] = a
    b_ref[...] = b

    def group_step(gi, h):
        off = pl.multiple_of(gi * 8, 8)
        hg = b_ref[pl.ds(off, 8), :] + a_ref[pl.ds(off, 8), :] * h
        b_ref[pl.ds(off, 8), :] = hg
        return hg[7:8, :]

    h_ref[...] = lax.fori_loop(0, tl // 8, group_step, h_ref[...])
    y_ref[...] = (b_ref[...] * _gelu_tanh(gate_ref[...].astype(F32))).astype(y_ref.dtype)


def _rglru(z, conv_w, conv_b, w_rg, b_rg, lam, *, batch, seq, u_col, gate_col, tl=512):
    n = z.shape[0]
    w = LRU_BLOCK_WIDTH
    spb = seq // tl
    rpb = seq // BF16_SUBLANES

    def prev_map(b, c, s):
        return (jnp.maximum(b * rpb + s * (tl // BF16_SUBLANES) - 1, 0), u_col + c)

    return pl.pallas_call(
        _lru_body,
        grid=(batch, LRU_BLOCKS, spb),
        in_specs=[
            pl.BlockSpec((tl, w), lambda b, c, s: (b * spb + s, u_col + c)),
            pl.BlockSpec((BF16_SUBLANES, w), prev_map),
            pl.BlockSpec((tl, w), lambda b, c, s: (b * spb + s, gate_col + c)),
            pl.BlockSpec((CONV_WIDTH, w), lambda b, c, s: (0, c)),
            pl.BlockSpec((1, w), lambda b, c, s: (0, c)),
            pl.BlockSpec((None, w, 2 * w), lambda b, c, s: (c, 0, 0)),
            pl.BlockSpec((None, 1, 2 * w), lambda b, c, s: (c, 0, 0)),
            pl.BlockSpec((1, w), lambda b, c, s: (0, c)),
        ],
        out_specs=pl.BlockSpec((tl, w), lambda b, c, s: (b * spb + s, c)),
        out_shape=jax.ShapeDtypeStruct((n, LRU_WIDTH), BF16),
        scratch_shapes=[
            pltpu.VMEM((tl + 8, w), F32),
            pltpu.VMEM((tl, w), F32),
            pltpu.VMEM((tl, w), F32),
            pltpu.VMEM((1, w), F32),
        ],
        compiler_params=_params("arbitrary", "arbitrary", "arbitrary"),
        name="rglru",
    )(z, z, z, conv_w, conv_b, w_rg, b_rg, lam)


def _merge_body(ya_ref, yl_ref, ga_ref, gl_ref, pa_ref, pl_ref, o_ref):
    pa = jnp.dot(ya_ref[...], pa_ref[...], preferred_element_type=F32)
    pr = jnp.dot(yl_ref[...], pl_ref[...], preferred_element_type=F32)
    merged = jax.nn.sigmoid(ga_ref[...].astype(F32)) * pa + jax.nn.sigmoid(gl_ref[...].astype(F32)) * pr
    o_ref[...] = merged.astype(o_ref.dtype)


def _merge(y_attn, y_lru, z, w_attn_proj, w_lru_proj, *, ga_col, gl_col, tm=512):
    n, d = y_attn.shape
    return pl.pallas_call(
        _merge_body,
        grid=(n // tm,),
        in_specs=[
            pl.BlockSpec((tm, d), lambda i: (i, 0)),
            pl.BlockSpec((tm, d), lambda i: (i, 0)),
            pl.BlockSpec((tm, d), lambda i: (i, ga_col)),
            pl.BlockSpec((tm, d), lambda i: (i, gl_col)),
            _resident((d, d), lambda i: (0, 0)),
            _resident((d, d), lambda i: (0, 0)),
        ],
        out_specs=pl.BlockSpec((tm, d), lambda i: (i, 0)),
        out_shape=jax.ShapeDtypeStruct((n, d), BF16),
        compiler_params=_params("arbitrary"),
        name="merge",
    )(y_attn, y_lru, z, z, w_attn_proj, w_lru_proj)


def _outproj_body(x_ref, m_ref, wo_ref, g_ref, wr_ref, br_ref, h_ref, xn_ref, idx_ref, wt_ref):
    h = x_ref[...] + jnp.dot(m_ref[...], wo_ref[...], preferred_element_type=F32)
    h_ref[...] = h
    xn = _rms_scale(h) * g_ref[...]
    xn_ref[...] = xn
    logits = lax.dot_general(wr_ref[...], xn.astype(BF16), (((1,), (1,)), ((), ())),
                             preferred_element_type=F32) + br_ref[...]
    expert = lax.broadcasted_iota(jnp.int32, logits.shape, 0).astype(F32)
    vals = []
    for k in range(TOP_K):
        m = jnp.max(logits, axis=0, keepdims=True)
        sel = jnp.min(jnp.where(logits == m, expert, float(N_EXPERTS)), axis=0, keepdims=True)
        vals.append(m)
        idx_ref[k:k + 1, :] = sel.astype(jnp.int32)
        logits = jnp.where(expert == sel, -jnp.inf, logits)
    exps = [jnp.exp(v - vals[0]) for v in vals]
    denom = exps[0] + exps[1] + exps[2] + exps[3]
    for k in range(TOP_K):
        wt_ref[k:k + 1, :] = exps[k] / denom


def _outproj_router(x2, merged, w_out, g_ffn, w_router_t, b_router, *, tm=512):
    n, d = x2.shape
    return pl.pallas_call(
        _outproj_body,
        grid=(n // tm,),
        in_specs=[
            pl.BlockSpec((tm, d), lambda i: (i, 0)),
            pl.BlockSpec((tm, d), lambda i: (i, 0)),
            _resident((d, d), lambda i: (0, 0)),
            pl.BlockSpec((1, d), lambda i: (0, 0)),
            pl.BlockSpec((N_EXPERTS, d), lambda i: (0, 0)),
            pl.BlockSpec((N_EXPERTS, 1), lambda i: (0, 0)),
        ],
        out_specs=[
            pl.BlockSpec((tm, d), lambda i: (i, 0)),
            pl.BlockSpec((tm, d), lambda i: (i, 0)),
            pl.BlockSpec((TOP_K, tm), lambda i: (0, i)),
            pl.BlockSpec((TOP_K, tm), lambda i: (0, i)),
        ],
        out_shape=[
            jax.ShapeDtypeStruct((n, d), F32),
            jax.ShapeDtypeStruct((n, d), F32),
            jax.ShapeDtypeStruct((TOP_K, n), jnp.int32),
            jax.ShapeDtypeStruct((TOP_K, n), F32),
        ],
        compiler_params=_params("arbitrary"),
        name="outproj_router",
    )(x2, merged, w_out, g_ffn, w_router_t, b_router)


def _row_copy(src_hbm, buf, sem, src_row, dst_row):
    return pltpu.make_async_copy(src_hbm.at[pl.ds(src_row, 1), :], buf.at[pl.ds(dst_row, 1), :], sem)


def _gather_chunk(idx_ref, src_hbm, buf, sem, count):
    for r in range(count):
        _row_copy(src_hbm, buf, sem, idx_ref[0, r], r).start()
    for r in range(count):
        _row_copy(src_hbm, buf, sem, 0, r).wait()


def _gather_body(valid_ref, idx_ref, src_hbm, o_ref, buf, sem):
    @pl.when(valid_ref[pl.program_id(0)] > 0)
    def _():
        _gather_chunk(idx_ref, src_hbm, buf, sem, buf.shape[0])
        o_ref[...] = buf[...].astype(o_ref.dtype)

    @pl.when(valid_ref[pl.program_id(0)] == 0)
    def _():
        o_ref[...] = jnp.zeros_like(o_ref)


def _gather_rows(src, row_idx, chunk_valid):
    chunk = EXPERT_BLOCK
    n_out = row_idx.shape[0]
    d = src.shape[1]
    idx3 = row_idx.reshape(n_out // chunk, 1, chunk)
    return pl.pallas_call(
        _gather_body,
        grid_spec=pltpu.PrefetchScalarGridSpec(
            num_scalar_prefetch=1,
            grid=(n_out // chunk,),
            in_specs=[
                pl.BlockSpec((None, 1, chunk), lambda i, v: (i, 0, 0), memory_space=pltpu.SMEM),
                pl.BlockSpec(memory_space=pl.ANY),
            ],
            out_specs=pl.BlockSpec((chunk, d), lambda i, v: (i, 0)),
            scratch_shapes=[pltpu.VMEM((chunk, d), src.dtype), pltpu.SemaphoreType.DMA(())],
        ),
        out_shape=jax.ShapeDtypeStruct((n_out, d), BF16),
        compiler_params=_params("arbitrary"),
        name="gather_rows",
    )(chunk_valid, idx3, src)


def _combine_body(idx_ref, wt_ref, src_hbm, o_ref, buf, sem):
    tm = o_ref.shape[0]
    _gather_chunk(idx_ref, src_hbm, buf, sem, TOP_K * tm)
    acc = wt_ref[:, 0:1] * buf[0:tm, :]
    for k in range(1, TOP_K):
        acc = acc + wt_ref[:, k:k + 1] * buf[k * tm:(k + 1) * tm, :]
    o_ref[...] = acc


def _combine(y_sorted, pos, weights, *, tm=64):
    n = pos.shape[0]
    d = y_sorted.shape[1]
    idx3 = pos.reshape(n // tm, tm, TOP_K).transpose(0, 2, 1).reshape(n // tm, 1, TOP_K * tm)
    return pl.pallas_call(
        _combine_body,
        grid=(n // tm,),
        in_specs=[
            pl.BlockSpec((None, 1, TOP_K * tm), lambda i: (i, 0, 0), memory_space=pltpu.SMEM),
            pl.BlockSpec((tm, TOP_K), lambda i: (i, 0)),
            pl.BlockSpec(memory_space=pl.ANY),
        ],
        out_specs=pl.BlockSpec((tm, d), lambda i: (i, 0)),
        out_shape=jax.ShapeDtypeStruct((n, d), F32),
        scratch_shapes=[pltpu.VMEM((TOP_K * tm, d), y_sorted.dtype), pltpu.SemaphoreType.DMA(())],
        compiler_params=_params("arbitrary"),
        name="combine",
    )(idx3, weights, y_sorted)


def _moe_mlp_body(ie_ref, nvb_ref, nu_ref, x_ref, wg_ref, wl_ref, w2_ref, bg_ref, bl_ref, b2_ref, o_ref):
    item = pl.program_id(0)
    chunk = pl.program_id(1)
    nvb = nvb_ref[item]

    @pl.when(chunk == 0)
    def _():
        o_ref[...] = jnp.broadcast_to(b2_ref[...], o_ref.shape)

    @pl.when(nvb > 0)
    def _():
        wg = wg_ref[...].astype(BF16)
        wl = wl_ref[...].astype(BF16)
        w2 = w2_ref[...].astype(BF16)
        for j in range(x_ref.shape[0] // EXPERT_BLOCK):
            @pl.when(j < nvb)
            def _():
                rows = slice(j * EXPERT_BLOCK, (j + 1) * EXPERT_BLOCK)
                xj = x_ref[rows, :]
                glu = jnp.dot(xj, wg, preferred_element_type=F32) + bg_ref[...]
                lin = jnp.dot(xj, wl, preferred_element_type=F32) + bl_ref[...]
                glu = jnp.minimum(glu, SWIGLU_LIMIT)
                lin = jnp.clip(lin, -SWIGLU_LIMIT, SWIGLU_LIMIT)
                act = glu * jax.nn.sigmoid(SWIGLU_ALPHA * glu) * (lin + 1.0)
                o_ref[rows, :] += jnp.dot(act.astype(BF16), w2, preferred_element_type=F32)


def _moe_mlp(x_sorted, w1, b1, w2, b2, item_expert, item_nvb, n_used):
    n_rows, d = x_sorted.shape
    n_items = n_rows // EXPERT_ITEM
    f = w2.shape[1]
    nc = f // FF_CHUNK

    def live_chunk(i, c, nu):
        return jnp.where(i < nu[0], c, nc - 1)

    def rows_map(i, c, ie, nvb, nu):
        return (jnp.minimum(i, nu[0] - 1), 0)

    return pl.pallas_call(
        _moe_mlp_body,
        grid_spec=pltpu.PrefetchScalarGridSpec(
            num_scalar_prefetch=3,
            grid=(n_items, nc),
            in_specs=[
                pl.BlockSpec((EXPERT_ITEM, d), rows_map),
                pl.BlockSpec((None, d, FF_CHUNK), lambda i, c, ie, nvb, nu: (ie[i], 0, live_chunk(i, c, nu))),
                pl.BlockSpec((None, d, FF_CHUNK), lambda i, c, ie, nvb, nu: (ie[i], 0, nc + live_chunk(i, c, nu))),
                pl.BlockSpec((None, FF_CHUNK, d), lambda i, c, ie, nvb, nu: (ie[i], live_chunk(i, c, nu), 0)),
                pl.BlockSpec((None, 1, FF_CHUNK), lambda i, c, ie, nvb, nu: (ie[i], 0, live_chunk(i, c, nu))),
                pl.BlockSpec((None, 1, FF_CHUNK), lambda i, c, ie, nvb, nu: (ie[i], 0, nc + live_chunk(i, c, nu))),
                pl.BlockSpec((None, 1, d), lambda i, c, ie, nvb, nu: (ie[i], 0, 0)),
            ],
            out_specs=pl.BlockSpec((EXPERT_ITEM, d), lambda i, c, ie, nvb, nu: (i, 0)),
        ),
        out_shape=jax.ShapeDtypeStruct((n_rows, d), F32),
        compiler_params=_params("arbitrary", "arbitrary"),
        name="moe_mlp",
    )(item_expert, item_nvb, n_used, x_sorted, w1, w1, w2, b1, b1, b2)


def _final_body(h_ref, moe_ref, p_ref, gp_ref, wg_ref, wp_ref, gf_ref, o_ref, *, final_norm):
    h = h_ref[...] + moe_ref[...]
    xn = (_rms_scale(h) * gp_ref[...]).astype(BF16)
    gate = jax.nn.sigmoid(jnp.dot(xn, wg_ref[...], preferred_element_type=F32))
    ple = jnp.dot(p_ref[...].astype(BF16), wp_ref[...], preferred_element_type=F32)
    h = h + gate * ple
    o_ref[...] = _rms_scale(h) * gf_ref[...] if final_norm else h


def _final(h1, moe, p2, g_ple, w_ple_gate, w_ple, g_final, *, final_norm, tm=512):
    n, d = h1.shape
    return pl.pallas_call(
        functools.partial(_final_body, final_norm=final_norm),
        grid=(n // tm,),
        in_specs=[
            pl.BlockSpec((tm, d), lambda i: (i, 0)),
            pl.BlockSpec((tm, d), lambda i: (i, 0)),
            pl.BlockSpec((tm, PLE_DIM), lambda i: (i, 0)),
            pl.BlockSpec((1, d), lambda i: (0, 0)),
            _resident((d, d), lambda i: (0, 0)),
            _resident((PLE_DIM, d), lambda i: (0, 0)),
            pl.BlockSpec((1, d), lambda i: (0, 0)),
        ],
        out_specs=pl.BlockSpec((tm, d), lambda i: (i, 0)),
        out_shape=jax.ShapeDtypeStruct((n, d), F32),
        compiler_params=_params("arbitrary"),
        name="final",
    )(h1, moe, p2, g_ple, w_ple_gate, w_ple, g_final)


def _routing_tables(expert_ids, n_tokens):
    nk = n_tokens * TOP_K
    sub = EXPERT_ITEM // EXPERT_BLOCK
    e_flat = expert_ids.T.reshape(nk)
    onehot = (e_flat[:, None] == jnp.arange(N_EXPERTS, dtype=jnp.int32)[None, :]).astype(jnp.int32)
    csum = jnp.cumsum(onehot, axis=0)
    rank = jnp.sum(onehot * csum, axis=1) - 1
    counts = csum[-1]
    padded = (counts + EXPERT_ITEM - 1) // EXPERT_ITEM * EXPERT_ITEM
    pend = jnp.cumsum(padded)
    pstart = pend - padded
    dest = pstart[e_flat] + rank
    n_items = nk // EXPERT_ITEM + N_EXPERTS
    tok_flat = jnp.arange(nk, dtype=jnp.int32) // TOP_K
    row_tok = jnp.zeros((n_items * EXPERT_ITEM,), jnp.int32).at[dest].set(tok_flat)
    n_used = (pend[-1] // EXPERT_ITEM).astype(jnp.int32)
    item_start = jnp.arange(n_items, dtype=jnp.int32) * EXPERT_ITEM
    item_expert = jnp.minimum(jnp.searchsorted(pend, item_start, side='right'), N_EXPERTS - 1).astype(jnp.int32)
    used = jnp.arange(n_items) < n_used
    item_expert = jnp.where(used, item_expert, item_expert[n_used - 1])
    rows_left = counts[item_expert] - (item_start - pstart[item_expert])
    item_nvb = jnp.where(used, jnp.clip((rows_left + EXPERT_BLOCK - 1) // EXPERT_BLOCK, 0, sub), 0).astype(jnp.int32)
    chunk_valid = (jnp.arange(sub, dtype=jnp.int32)[None, :] < item_nvb[:, None]).astype(jnp.int32).reshape(-1)
    pos = dest.reshape(n_tokens, TOP_K).astype(jnp.int32)
    return row_tok, pos, item_expert, item_nvb, n_used.reshape(1), chunk_valid


def _layer(h, p_l, norm_mix_g, w_in, b_in, conv_w, conv_b, w_rg_a, b_rg_a, w_rg_x, b_rg_x, lru_lambda,
           attn_sinks, w_attn_proj, w_lru_proj, w_out, norm_ffn_g, w_router, b_router, w_mlp1, b_mlp1,
           w_mlp2, b_mlp2, norm_ple_g, w_ple, w_ple_gate, norm_final_g, *, batch, seq, final_norm):
    n = h.shape[0]
    row = lambda v: v.reshape(1, -1)

    kv0 = Q_WIDTH
    rest0 = Q_WIDTH + 2 * KV_WIDTH
    reorder = lambda a: jnp.concatenate([a[..., :kv0], a[..., rest0:], a[..., kv0:rest0]], axis=-1)
    z = _inproj(h, row(norm_mix_g), reorder(w_in).astype(BF16), row(reorder(b_in)))

    kv_col = (Q_WIDTH + 2 * LRU_WIDTH + 2 * D_MODEL) // (2 * KV_WIDTH)
    y_attn = _attention(z, attn_sinks, batch=batch, seq=seq, q_col=0, kv_col=kv_col)

    w_rg = jnp.concatenate([w_rg_a, w_rg_x], axis=-1).astype(BF16)
    b_rg = jnp.concatenate([b_rg_a, b_rg_x], axis=-1).reshape(LRU_BLOCKS, 1, 2 * LRU_BLOCK_WIDTH)
    y_lru = _rglru(z, conv_w, row(conv_b), w_rg, b_rg, row(lru_lambda), batch=batch, seq=seq,
                   u_col=Q_WIDTH // LRU_BLOCK_WIDTH, gate_col=(Q_WIDTH + LRU_WIDTH) // LRU_BLOCK_WIDTH)

    merged = _merge(y_attn, y_lru, z, w_attn_proj.astype(BF16), w_lru_proj.astype(BF16),
                    ga_col=(Q_WIDTH + 2 * LRU_WIDTH) // D_MODEL, gl_col=(Q_WIDTH + 2 * LRU_WIDTH + D_MODEL) // D_MODEL)
    h1, xn2, expert_ids, expert_w = _outproj_router(
        h, merged, w_out.astype(BF16), row(norm_ffn_g), w_router.T.astype(BF16), b_router.reshape(N_EXPERTS, 1))

    row_tok, pos, item_expert, item_nvb, n_used, chunk_valid = _routing_tables(expert_ids, n)
    x_sorted = _gather_rows(xn2, row_tok, chunk_valid)
    y_sorted = _moe_mlp(x_sorted, w_mlp1, b_mlp1[:, None, :], w_mlp2, b_mlp2[:, None, :], item_expert, item_nvb, n_used)
    moe = _combine(y_sorted, pos, expert_w.T)

    return _final(h1, moe, p_l, row(norm_ple_g), w_ple_gate.astype(BF16), w_ple.astype(BF16), row(norm_final_g),
                  final_norm=final_norm)


def kernel(x, p, norm_mix_g, w_in, b_in, conv_w, conv_b, w_rg_a, b_rg_a, w_rg_x, b_rg_x, lru_lambda, attn_sinks, w_attn_proj, w_lru_proj, w_out, norm_ffn_g, w_router, b_router, w_mlp1, b_mlp1, w_mlp2, b_mlp2, norm_ple_g, w_ple, w_ple_gate, norm_final_g):
    batch, seq, d = x.shape
    depth = p.shape[0]
    h = x.reshape(batch * seq, d)
    for l in range(depth):
        h = _layer(h, p[l].reshape(batch * seq, -1), norm_mix_g[l], w_in[l], b_in[l], conv_w[l], conv_b[l],
                   w_rg_a[l], b_rg_a[l], w_rg_x[l], b_rg_x[l], lru_lambda[l], attn_sinks[l], w_attn_proj[l],
                   w_lru_proj[l], w_out[l], norm_ffn_g[l], w_router[l], b_router[l], w_mlp1[l], b_mlp1[l],
                   w_mlp2[l], b_mlp2[l], norm_ple_g[l], w_ple[l], w_ple_gate[l], norm_final_g,
                   batch=batch, seq=seq, final_norm=(l == depth - 1))
    return h.reshape(batch, seq, d)
```

```python
import functools

import jax
import jax.numpy as jnp
from jax import lax
from jax.experimental import pallas as pl
from jax.experimental.pallas import tpu as pltpu

D_MODEL = 2048
N_Q_HEADS = 32
N_KV_HEADS = 4
HEAD_DIM = 64
Q_WIDTH = N_Q_HEADS * HEAD_DIM
KV_WIDTH = N_KV_HEADS * HEAD_DIM
WINDOW = 128
LRU_WIDTH = D_MODEL
LRU_BLOCKS = 8
LRU_BLOCK_WIDTH = LRU_WIDTH // LRU_BLOCKS
CONV_WIDTH = 4
LRU_C = 8.0
IN_WIDTH = Q_WIDTH + 2 * KV_WIDTH + 2 * LRU_WIDTH + 2 * D_MODEL
N_EXPERTS = 32
TOP_K = 4
D_EXPERT = D_MODEL
SWIGLU_LIMIT = 7.0
SWIGLU_ALPHA = 1.702
EXPERT_BLOCK = 256
EXPERT_ITEM = 768
FF_CHUNK = 512
PLE_DIM = 256
RMS_EPS = 1e-6

VMEM_LIMIT_BYTES = 56 * 1024 * 1024
BF16_SUBLANES = 16

F32 = jnp.float32
BF16 = jnp.bfloat16


def _params(*sem):
    return pltpu.CompilerParams(dimension_semantics=sem, vmem_limit_bytes=VMEM_LIMIT_BYTES)


def _resident(shape, index_map):
    return pl.BlockSpec(shape, index_map, pipeline_mode=pl.Buffered(1))


def _rms_scale(x):
    var = jnp.mean(x * x, axis=-1, keepdims=True)
    return x * lax.rsqrt(var + RMS_EPS)


def _pack_bf16_pairs(x):
    c = x.shape[1] // 2
    xb = x.astype(BF16).astype(F32)
    lo = lax.bitcast_convert_type(xb[:, :c], jnp.uint32) >> 16
    hi = lax.bitcast_convert_type(xb[:, c:], jnp.uint32) & jnp.uint32(0xFFFF0000)
    return hi | lo


def _unpack_bf16_pairs(w):
    lo = lax.bitcast_convert_type(w << 16, F32)
    hi = lax.bitcast_convert_type(w & jnp.uint32(0xFFFF0000), F32)
    return lo, hi


def _inproj_body(x_ref, g_ref, w_ref, b_ref, o_ref, xn_ref):
    @pl.when(pl.program_id(1) == 0)
    def _():
        xn_ref[...] = (_rms_scale(x_ref[...]) * g_ref[...]).astype(BF16)

    acc = jnp.dot(xn_ref[...], w_ref[...], preferred_element_type=F32)
    o_ref[...] = (acc + b_ref[...]).astype(o_ref.dtype)


def _inproj(x2, g, w, b, *, tm=1024, tn=1536):
    n, d = x2.shape
    width = w.shape[1]
    return pl.pallas_call(
        _inproj_body,
        grid=(n // tm, width // tn),
        in_specs=[
            pl.BlockSpec((tm, d), lambda i, j: (i, 0)),
            pl.BlockSpec((1, d), lambda i, j: (0, 0)),
            pl.BlockSpec((d, tn), lambda i, j: (0, j)),
            pl.BlockSpec((1, tn), lambda i, j: (0, j)),
        ],
        out_specs=pl.BlockSpec((tm, tn), lambda i, j: (i, j)),
        out_shape=jax.ShapeDtypeStruct((n, width), BF16),
        scratch_shapes=[pltpu.VMEM((tm, d), BF16)],
        compiler_params=_params("arbitrary", "arbitrary"),
        name="inproj",
    )(x2, g, w, b)


def _block_diag_pair(pair, odd, lo_mask):
    swapped = jnp.concatenate([pair[:, HEAD_DIM:], pair[:, :HEAD_DIM]], axis=1)
    zero = jnp.zeros_like(pair)
    if odd:
        lo = jnp.where(lo_mask, swapped, zero)
        hi = jnp.where(lo_mask, zero, pair)
    else:
        lo = jnp.where(lo_mask, pair, zero)
        hi = jnp.where(lo_mask, zero, swapped)
    return jnp.concatenate([lo, hi], axis=0)


def _attn_body(sinks_ref, q_ref, kvp_ref, kvc_ref, o_ref, *, tq):
    nblk = tq // WINDOW
    group = N_Q_HEADS // N_KV_HEADS
    pairs = group // 2
    rows = pairs * WINDOW
    band_w = 2 * WINDOW
    s_idx = pl.program_id(1)

    lo_mask = lax.broadcasted_iota(jnp.int32, (1, 2 * HEAD_DIM), 1) < HEAD_DIM
    qi = lax.broadcasted_iota(jnp.int32, (rows, band_w), 0) % WINDOW
    kj = lax.broadcasted_iota(jnp.int32, (rows, band_w), 1)
    in_band = (kj > qi) & (kj <= qi + WINDOW)
    first_key = jnp.where(s_idx > 0, 0, WINDOW)
    row_pair = lax.broadcasted_iota(jnp.int32, (rows, 1), 0) // WINDOW
    neg = jnp.finfo(F32).min

    for i in range(nblk):
        if i == 0:
            band = jnp.concatenate([kvp_ref[...], kvc_ref[0:WINDOW, :]], axis=0)
            mask = in_band & (kj >= first_key)
        else:
            band = kvc_ref[(i - 1) * WINDOW:(i + 1) * WINDOW, :]
            mask = in_band
        for kh in range(N_KV_HEADS):
            col = (kh // 2) * 2 * HEAD_DIM
            kk = _block_diag_pair(band[:, col:col + 2 * HEAD_DIM], kh % 2 == 1, lo_mask)
            vv = _block_diag_pair(band[:, KV_WIDTH + col:KV_WIDTH + col + 2 * HEAD_DIM], kh % 2 == 1, lo_mask)
            qbase = kh * group * HEAD_DIM
            q4 = jnp.concatenate(
                [q_ref[i * WINDOW:(i + 1) * WINDOW, qbase + 2 * HEAD_DIM * j:qbase + 2 * HEAD_DIM * (j + 1)]
                 for j in range(pairs)], axis=0)
            s = lax.dot_general(q4, kk, (((1,), (1,)), ((), ())), preferred_element_type=F32)
            s = s * (HEAD_DIM ** -0.5)
            sink_lo = jnp.zeros((rows, 1), F32)
            sink_hi = jnp.zeros((rows, 1), F32)
            for j in range(pairs):
                sink_lo = jnp.where(row_pair == j, sinks_ref[kh * group + 2 * j], sink_lo)
                sink_hi = jnp.where(row_pair == j, sinks_ref[kh * group + 2 * j + 1], sink_hi)
            halves = []
            inv = []
            for half, sink in ((0, sink_lo), (1, sink_hi)):
                sh = jnp.where(mask, s[:, half * band_w:(half + 1) * band_w], neg)
                m = jnp.maximum(jnp.max(sh, axis=1, keepdims=True), sink)
                p = jnp.exp(sh - m)
                denom = jnp.sum(p, axis=1, keepdims=True) + jnp.exp(sink - m)
                halves.append(p.astype(BF16))
                inv.append(1.0 / denom)
            pcat = jnp.concatenate(halves, axis=1)
            o = jnp.dot(pcat, vv, preferred_element_type=F32)
            o = o * jnp.where(lo_mask, inv[0], inv[1])
            for j in range(pairs):
                o_ref[i * WINDOW:(i + 1) * WINDOW, qbase + 2 * HEAD_DIM * j:qbase + 2 * HEAD_DIM * (j + 1)] = (
                    o[j * WINDOW:(j + 1) * WINDOW, :].astype(o_ref.dtype))


def _attention(z, sinks, *, batch, seq, q_col, kv_col, tq=512):
    n = z.shape[0]
    spb = seq // tq
    wpb = seq // WINDOW
    kvw = 2 * KV_WIDTH

    def prev_map(b, s, sinks_ref):
        return (jnp.maximum(b * wpb + s * (tq // WINDOW) - 1, 0), kv_col)

    return pl.pallas_call(
        functools.partial(_attn_body, tq=tq),
        grid_spec=pltpu.PrefetchScalarGridSpec(
            num_scalar_prefetch=1,
            grid=(batch, spb),
            in_specs=[
                pl.BlockSpec((tq, Q_WIDTH), lambda b, s, sinks_ref: (b * spb + s, q_col)),
                pl.BlockSpec((WINDOW, kvw), prev_map),
                pl.BlockSpec((tq, kvw), lambda b, s, sinks_ref: (b * spb + s, kv_col)),
            ],
            out_specs=pl.BlockSpec((tq, Q_WIDTH), lambda b, s, sinks_ref: (b * spb + s, 0)),
        ),
        out_shape=jax.ShapeDtypeStruct((n, Q_WIDTH), BF16),
        compiler_params=_params("arbitrary", "arbitrary"),
        name="attn",
    )(sinks, z, z, z)


def _gelu_tanh(x):
    c = (2.0 / jnp.pi) ** 0.5
    return x * (0.5 * (1.0 + jnp.tanh(c * (x + 0.044715 * (x * x * x)))))


def _lru_body(u_ref, up_ref, gate_ref, cw_ref, cb_ref, wrg_ref, brg_ref, lam_ref, y_ref,
              ext_ref, a_ref, b_ref, h_ref):
    s_idx = pl.program_id(2)
    tl, w = u_ref.shape
    pad = 8

    @pl.when(s_idx == 0)
    def _():
        ext_ref[0:pad, :] = jnp.zeros((pad, w), F32)
        h_ref[...] = jnp.zeros_like(h_ref)

    @pl.when(s_idx > 0)
    def _():
        ext_ref[0:pad, :] = up_ref[...].astype(F32)[BF16_SUBLANES - pad:BF16_SUBLANES, :]

    ext_ref[pad:pad + tl, :] = u_ref[...].astype(F32)
    first = pad - (CONV_WIDTH - 1)
    uc = cb_ref[...] + ext_ref[first:first + tl, :] * cw_ref[0:1, :]
    for tap in range(1, CONV_WIDTH):
        uc = uc + ext_ref[first + tap:first + tap + tl, :] * cw_ref[tap:tap + 1, :]

    gates = jnp.dot(uc.astype(BF16), wrg_ref[...], preferred_element_type=F32) + brg_ref[...]
    r = jax.nn.sigmoid(gates[:, :w])
    ig = jax.nn.sigmoid(gates[:, w:])
    neg_lam = -lam_ref[...]
    softplus = jnp.maximum(neg_lam, 0.0) + jnp.log1p(jnp.exp(-jnp.abs(neg_lam)))
    log_a = (-LRU_C * r) * softplus
    a = jnp.exp(log_a)
    mult = jnp.sqrt(-jnp.tanh(log_a) * (a * a + 1.0))
    row = lax.broadcasted_iota(jnp.int32, (tl, 1), 0)
    mult = jnp.where((row == 0) & (s_idx == 0), 1.0, mult)
    b = mult * (ig * uc)

    sub = row % 8
    for d in (1, 2, 4):
        keep = sub >= d
        b_prev = jnp.where(keep, pltpu.roll(b, d, 0), 0.0)
        a_prev = jnp.where(keep, pltpu.roll(a, d, 0), 1.0)
        b = b + a * b_prev
        a = a * a_prev
    a_ref[...] = a
    b_ref[...] = b

    def group_step(gi, h):
        off = pl.multiple_of(gi * 8, 8)
        hg = b_ref[pl.ds(off, 8), :] + a_ref[pl.ds(off, 8), :] * h
        b_ref[pl.ds(off, 8), :] = hg
        return hg[7:8, :]

    h_ref[...] = lax.fori_loop(0, tl // 8, group_step, h_ref[...])
    y_ref[...] = (b_ref[...] * _gelu_tanh(gate_ref[...].astype(F32))).astype(y_ref.dtype)


def _rglru(z, conv_w, conv_b, w_rg, b_rg, lam, *, batch, seq, u_col, gate_col, tl=512):
    n = z.shape[0]
    w = LRU_BLOCK_WIDTH
    spb = seq // tl
    rpb = seq // BF16_SUBLANES

    def prev_map(b, c, s):
        return (jnp.maximum(b * rpb + s * (tl // BF16_SUBLANES) - 1, 0), u_col + c)

    return pl.pallas_call(
        _lru_body,
        grid=(batch, LRU_BLOCKS, spb),
        in_specs=[
            pl.BlockSpec((tl, w), lambda b, c, s: (b * spb + s, u_col + c)),
            pl.BlockSpec((BF16_SUBLANES, w), prev_map),
            pl.BlockSpec((tl, w), lambda b, c, s: (b * spb + s, gate_col + c)),
            pl.BlockSpec((CONV_WIDTH, w), lambda b, c, s: (0, c)),
            pl.BlockSpec((1, w), lambda b, c, s: (0, c)),
            pl.BlockSpec((None, w, 2 * w), lambda b, c, s: (c, 0, 0)),
            pl.BlockSpec((None, 1, 2 * w), lambda b, c, s: (c, 0, 0)),
            pl.BlockSpec((1, w), lambda b, c, s: (0, c)),
        ],
        out_specs=pl.BlockSpec((tl, w), lambda b, c, s: (b * spb + s, c)),
        out_shape=jax.ShapeDtypeStruct((n, LRU_WIDTH), BF16),
        scratch_shapes=[
            pltpu.VMEM((tl + 8, w), F32),
            pltpu.VMEM((tl, w), F32),
            pltpu.VMEM((tl, w), F32),
            pltpu.VMEM((1, w), F32),
        ],
        compiler_params=_params("arbitrary", "arbitrary", "arbitrary"),
        name="rglru",
    )(z, z, z, conv_w, conv_b, w_rg, b_rg, lam)


def _merge_body(ya_ref, yl_ref, ga_ref, gl_ref, pa_ref, pl_ref, o_ref):
    pa = jnp.dot(ya_ref[...], pa_ref[...], preferred_element_type=F32)
    pr = jnp.dot(yl_ref[...], pl_ref[...], preferred_element_type=F32)
    merged = jax.nn.sigmoid(ga_ref[...].astype(F32)) * pa + jax.nn.sigmoid(gl_ref[...].astype(F32)) * pr
    o_ref[...] = merged.astype(o_ref.dtype)


def _merge(y_attn, y_lru, z, w_attn_proj, w_lru_proj, *, ga_col, gl_col, tm=512):
    n, d = y_attn.shape
    return pl.pallas_call(
        _merge_body,
        grid=(n // tm,),
        in_specs=[
            pl.BlockSpec((tm, d), lambda i: (i, 0)),
            pl.BlockSpec((tm, d), lambda i: (i, 0)),
            pl.BlockSpec((tm, d), lambda i: (i, ga_col)),
            pl.BlockSpec((tm, d), lambda i: (i, gl_col)),
            _resident((d, d), lambda i: (0, 0)),
            _resident((d, d), lambda i: (0, 0)),
        ],
        out_specs=pl.BlockSpec((tm, d), lambda i: (i, 0)),
        out_shape=jax.ShapeDtypeStruct((n, d), BF16),
        compiler_params=_params("arbitrary"),
        name="merge",
    )(y_attn, y_lru, z, z, w_attn_proj, w_lru_proj)


def _outproj_body(x_ref, m_ref, wo_ref, g_ref, wr_ref, br_ref, h_ref, xn_ref, idx_ref, wt_ref):
    h = x_ref[...] + jnp.dot(m_ref[...], wo_ref[...], preferred_element_type=F32)
    h_ref[...] = h
    xn = _rms_scale(h) * g_ref[...]
    xn_ref[...] = _pack_bf16_pairs(xn)
    logits = lax.dot_general(wr_ref[...], xn.astype(BF16), (((1,), (1,)), ((), ())),
                             preferred_element_type=F32) + br_ref[...]
    expert = lax.broadcasted_iota(jnp.int32, logits.shape, 0).astype(F32)
    vals = []
    for k in range(TOP_K):
        m = jnp.max(logits, axis=0, keepdims=True)
        sel = jnp.min(jnp.where(logits == m, expert, float(N_EXPERTS)), axis=0, keepdims=True)
        vals.append(m)
        idx_ref[k:k + 1, :] = sel.astype(jnp.int32)
        logits = jnp.where(expert == sel, -jnp.inf, logits)
    exps = [jnp.exp(v - vals[0]) for v in vals]
    denom = exps[0] + exps[1] + exps[2] + exps[3]
    for k in range(TOP_K):
        wt_ref[k:k + 1, :] = exps[k] / denom


def _outproj_router(x2, merged, w_out, g_ffn, w_router_t, b_router, *, tm=512):
    n, d = x2.shape
    return pl.pallas_call(
        _outproj_body,
        grid=(n // tm,),
        in_specs=[
            pl.BlockSpec((tm, d), lambda i: (i, 0)),
            pl.BlockSpec((tm, d), lambda i: (i, 0)),
            _resident((d, d), lambda i: (0, 0)),
            pl.BlockSpec((1, d), lambda i: (0, 0)),
            pl.BlockSpec((N_EXPERTS, d), lambda i: (0, 0)),
            pl.BlockSpec((N_EXPERTS, 1), lambda i: (0, 0)),
        ],
        out_specs=[
            pl.BlockSpec((tm, d), lambda i: (i, 0)),
            pl.BlockSpec((tm, d // 2), lambda i: (i, 0)),
            pl.BlockSpec((TOP_K, tm), lambda i: (0, i)),
            pl.BlockSpec((TOP_K, tm), lambda i: (0, i)),
        ],
        out_shape=[
            jax.ShapeDtypeStruct((n, d), F32),
            jax.ShapeDtypeStruct((n, d // 2), jnp.uint32),
            jax.ShapeDtypeStruct((TOP_K, n), jnp.int32),
            jax.ShapeDtypeStruct((TOP_K, n), F32),
        ],
        compiler_params=_params("arbitrary"),
        name="outproj_router",
    )(x2, merged, w_out, g_ffn, w_router_t, b_router)


def _row_copy(src, dst, sem, src_row, dst_row):
    return pltpu.make_async_copy(src.at[pl.ds(src_row, 1), :], dst.at[pl.ds(dst_row, 1), :], sem)


def _dispatch_body(dst_ref, src_hbm, init_hbm, out_hbm, sems):
    del init_hbm
    i = pl.program_id(0)
    n = pl.num_programs(0)
    c = dst_ref.shape[1]
    slot = i % 2
    tok0 = i * (c // TOP_K)
    for r in range(c):
        _row_copy(src_hbm, out_hbm, sems.at[slot], tok0 + r // TOP_K, dst_ref[0, r]).start()

    @pl.when(i > 0)
    def _():
        for r in range(c):
            _row_copy(src_hbm, out_hbm, sems.at[1 - slot], 0, 0).wait()

    @pl.when(i == n - 1)
    def _():
        for r in range(c):
            _row_copy(src_hbm, out_hbm, sems.at[slot], 0, 0).wait()


def _dispatch(rows, pos, n_sorted, *, chunk=512):
    n, w = rows.shape
    dst3 = pos.reshape(n * TOP_K // chunk, 1, chunk)
    init = jnp.zeros((n_sorted, w), rows.dtype)
    return pl.pallas_call(
        _dispatch_body,
        grid=(n * TOP_K // chunk,),
        in_specs=[
            pl.BlockSpec((None, 1, chunk), lambda i: (i, 0, 0), memory_space=pltpu.SMEM),
            pl.BlockSpec(memory_space=pl.ANY),
            pl.BlockSpec(memory_space=pl.ANY),
        ],
        out_specs=pl.BlockSpec(memory_space=pl.ANY),
        out_shape=jax.ShapeDtypeStruct((n_sorted, w), rows.dtype),
        scratch_shapes=[pltpu.SemaphoreType.DMA((2,))],
        input_output_aliases={2: 0},
        compiler_params=_params("arbitrary"),
        name="dispatch",
    )(dst3, rows, init)


def _combine_body(idx_ref, nxt_ref, wt_ref, src_hbm, o_ref, buf, sems):
    i = pl.program_id(0)
    n = pl.num_programs(0)
    tm = o_ref.shape[0]
    count = TOP_K * tm
    slot = i % 2

    @pl.when(i == 0)
    def _():
        for r in range(count):
            _row_copy(src_hbm, buf.at[0], sems.at[0], idx_ref[0, r], r).start()

    @pl.when(i + 1 < n)
    def _():
        for r in range(count):
            _row_copy(src_hbm, buf.at[1 - slot], sems.at[1 - slot], nxt_ref[0, r], r).start()

    for r in range(count):
        _row_copy(src_hbm, buf.at[slot], sems.at[slot], 0, r).wait()

    c = o_ref.shape[1] // 2
    acc_lo = jnp.zeros((tm, c), F32)
    acc_hi = jnp.zeros((tm, c), F32)
    for k in range(TOP_K):
        lo, hi = _unpack_bf16_pairs(buf[slot, k * tm:(k + 1) * tm, :])
        acc_lo = acc_lo + wt_ref[:, k:k + 1] * lo
        acc_hi = acc_hi + wt_ref[:, k:k + 1] * hi
    o_ref[:, :c] = acc_lo
    o_ref[:, c:] = acc_hi


def _combine(y_sorted, pos, weights, *, tm=128):
    n = pos.shape[0]
    w = y_sorted.shape[1]
    steps = n // tm
    idx3 = pos.reshape(steps, tm, TOP_K).transpose(0, 2, 1).reshape(steps, 1, TOP_K * tm)
    return pl.pallas_call(
        _combine_body,
        grid=(steps,),
        in_specs=[
            pl.BlockSpec((None, 1, TOP_K * tm), lambda i: (i, 0, 0), memory_space=pltpu.SMEM),
            pl.BlockSpec((None, 1, TOP_K * tm), lambda i: (jnp.minimum(i + 1, steps - 1), 0, 0),
                         memory_space=pltpu.SMEM),
            pl.BlockSpec((tm, TOP_K), lambda i: (i, 0)),
            pl.BlockSpec(memory_space=pl.ANY),
        ],
        out_specs=pl.BlockSpec((tm, 2 * w), lambda i: (i, 0)),
        out_shape=jax.ShapeDtypeStruct((n, 2 * w), F32),
        scratch_shapes=[pltpu.VMEM((2, TOP_K * tm, w), y_sorted.dtype), pltpu.SemaphoreType.DMA((2,))],
        compiler_params=_params("arbitrary"),
        name="combine",
    )(idx3, idx3, weights, y_sorted)


def _moe_mlp_body(ie_ref, nvb_ref, nu_ref, x_ref, wg_ref, wl_ref, w2_ref, bg_ref, bl_ref, b2_ref, o_ref, acc_ref):
    item = pl.program_id(0)
    chunk = pl.program_id(1)
    nvb = nvb_ref[item]

    @pl.when(chunk == 0)
    def _():
        acc_ref[...] = jnp.broadcast_to(b2_ref[...], acc_ref.shape)

    @pl.when(nvb > 0)
    def _():
        wg = wg_ref[...].astype(BF16)
        wl = wl_ref[...].astype(BF16)
        w2 = w2_ref[...].astype(BF16)
        for j in range(x_ref.shape[0] // EXPERT_BLOCK):
            @pl.when(j < nvb)
            def _():
                rows = slice(j * EXPERT_BLOCK, (j + 1) * EXPERT_BLOCK)
                lo, hi = _unpack_bf16_pairs(x_ref[rows, :])
                xj = jnp.concatenate([lo.astype(BF16), hi.astype(BF16)], axis=1)
                glu = jnp.dot(xj, wg, preferred_element_type=F32) + bg_ref[...]
                lin = jnp.dot(xj, wl, preferred_element_type=F32) + bl_ref[...]
                glu = jnp.minimum(glu, SWIGLU_LIMIT)
                lin = jnp.clip(lin, -SWIGLU_LIMIT, SWIGLU_LIMIT)
                act = glu * jax.nn.sigmoid(SWIGLU_ALPHA * glu) * (lin + 1.0)
                acc_ref[rows, :] += jnp.dot(act.astype(BF16), w2, preferred_element_type=F32)

    @pl.when(chunk == pl.num_programs(1) - 1)
    def _():
        o_ref[...] = _pack_bf16_pairs(acc_ref[...])


def _moe_mlp(x_sorted, w1, b1, w2, b2, item_expert, item_nvb, n_used):
    n_rows, half = x_sorted.shape
    d = 2 * half
    n_items = n_rows // EXPERT_ITEM
    f = w2.shape[1]
    nc = f // FF_CHUNK

    def live_chunk(i, c, nu):
        return jnp.where(i < nu[0], c, nc - 1)

    def rows_map(i, c, ie, nvb, nu):
        return (jnp.minimum(i, nu[0] - 1), 0)

    return pl.pallas_call(
        _moe_mlp_body,
        grid_spec=pltpu.PrefetchScalarGridSpec(
            num_scalar_prefetch=3,
            grid=(n_items, nc),
            in_specs=[
                pl.BlockSpec((EXPERT_ITEM, half), rows_map),
                pl.BlockSpec((None, d, FF_CHUNK), lambda i, c, ie, nvb, nu: (ie[i], 0, live_chunk(i, c, nu))),
                pl.BlockSpec((None, d, FF_CHUNK), lambda i, c, ie, nvb, nu: (ie[i], 0, nc + live_chunk(i, c, nu))),
                pl.BlockSpec((None, FF_CHUNK, d), lambda i, c, ie, nvb, nu: (ie[i], live_chunk(i, c, nu), 0)),
                pl.BlockSpec((None, 1, FF_CHUNK), lambda i, c, ie, nvb, nu: (ie[i], 0, live_chunk(i, c, nu))),
                pl.BlockSpec((None, 1, FF_CHUNK), lambda i, c, ie, nvb, nu: (ie[i], 0, nc + live_chunk(i, c, nu))),
                pl.BlockSpec((None, 1, d), lambda i, c, ie, nvb, nu: (ie[i], 0, 0)),
            ],
            out_specs=pl.BlockSpec((EXPERT_ITEM, half), lambda i, c, ie, nvb, nu: (i, 0)),
            scratch_shapes=[pltpu.VMEM((EXPERT_ITEM, d), F32)],
        ),
        out_shape=jax.ShapeDtypeStruct((n_rows, half), jnp.uint32),
        compiler_params=_params("arbitrary", "arbitrary"),
        name="moe_mlp",
    )(item_expert, item_nvb, n_used, x_sorted, w1, w1, w2, b1, b1, b2)


def _final_body(h_ref, moe_ref, p_ref, gp_ref, wg_ref, wp_ref, gf_ref, o_ref, *, final_norm):
    h = h_ref[...] + moe_ref[...]
    xn = (_rms_scale(h) * gp_ref[...]).astype(BF16)
    gate = jax.nn.sigmoid(jnp.dot(xn, wg_ref[...], preferred_element_type=F32))
    ple = jnp.dot(p_ref[...].astype(BF16), wp_ref[...], preferred_element_type=F32)
    h = h + gate * ple
    o_ref[...] = _rms_scale(h) * gf_ref[...] if final_norm else h


def _final(h1, moe, p2, g_ple, w_ple_gate, w_ple, g_final, *, final_norm, tm=512):
    n, d = h1.shape
    return pl.pallas_call(
        functools.partial(_final_body, final_norm=final_norm),
        grid=(n // tm,),
        in_specs=[
            pl.BlockSpec((tm, d), lambda i: (i, 0)),
            pl.BlockSpec((tm, d), lambda i: (i, 0)),
            pl.BlockSpec((tm, PLE_DIM), lambda i: (i, 0)),
            pl.BlockSpec((1, d), lambda i: (0, 0)),
            _resident((d, d), lambda i: (0, 0)),
            _resident((PLE_DIM, d), lambda i: (0, 0)),
            pl.BlockSpec((1, d), lambda i: (0, 0)),
        ],
        out_specs=pl.BlockSpec((tm, d), lambda i: (i, 0)),
        out_shape=jax.ShapeDtypeStruct((n, d), F32),
        compiler_params=_params("arbitrary"),
        name="final",
    )(h1, moe, p2, g_ple, w_ple_gate, w_ple, g_final)


def _routing_tables(expert_ids, n_tokens):
    nk = n_tokens * TOP_K
    sub = EXPERT_ITEM // EXPERT_BLOCK
    e_flat = expert_ids.T.reshape(nk)
    onehot = (e_flat[:, None] == jnp.arange(N_EXPERTS, dtype=jnp.int32)[None, :]).astype(jnp.int32)
    csum = jnp.cumsum(onehot, axis=0)
    rank = jnp.sum(onehot * csum, axis=1) - 1
    counts = csum[-1]
    padded = (counts + EXPERT_ITEM - 1) // EXPERT_ITEM * EXPERT_ITEM
    pend = jnp.cumsum(padded)
    pstart = pend - padded
    dest = pstart[e_flat] + rank
    n_items = nk // EXPERT_ITEM + N_EXPERTS
    n_used = (pend[-1] // EXPERT_ITEM).astype(jnp.int32)
    item_start = jnp.arange(n_items, dtype=jnp.int32) * EXPERT_ITEM
    item_expert = jnp.minimum(jnp.searchsorted(pend, item_start, side='right'), N_EXPERTS - 1).astype(jnp.int32)
    used = jnp.arange(n_items) < n_used
    item_expert = jnp.where(used, item_expert, item_expert[n_used - 1])
    rows_left = counts[item_expert] - (item_start - pstart[item_expert])
    item_nvb = jnp.where(used, jnp.clip((rows_left + EXPERT_BLOCK - 1) // EXPERT_BLOCK, 0, sub), 0).astype(jnp.int32)
    pos = dest.reshape(n_tokens, TOP_K).astype(jnp.int32)
    return pos, item_expert, item_nvb, n_used.reshape(1), n_items * EXPERT_ITEM


def _layer(h, p_l, norm_mix_g, w_in, b_in, conv_w, conv_b, w_rg_a, b_rg_a, w_rg_x, b_rg_x, lru_lambda,
           attn_sinks, w_attn_proj, w_lru_proj, w_out, norm_ffn_g, w_router, b_router, w_mlp1, b_mlp1,
           w_mlp2, b_mlp2, norm_ple_g, w_ple, w_ple_gate, norm_final_g, *, batch, seq, final_norm):
    n = h.shape[0]
    row = lambda v: v.reshape(1, -1)

    kv0 = Q_WIDTH
    rest0 = Q_WIDTH + 2 * KV_WIDTH
    reorder = lambda a: jnp.concatenate([a[..., :kv0], a[..., rest0:], a[..., kv0:rest0]], axis=-1)
    z = _inproj(h, row(norm_mix_g), reorder(w_in).astype(BF16), row(reorder(b_in)))

    kv_col = (Q_WIDTH + 2 * LRU_WIDTH + 2 * D_MODEL) // (2 * KV_WIDTH)
    y_attn = _attention(z, attn_sinks, batch=batch, seq=seq, q_col=0, kv_col=kv_col)

    w_rg = jnp.concatenate([w_rg_a, w_rg_x], axis=-1).astype(BF16)
    b_rg = jnp.concatenate([b_rg_a, b_rg_x], axis=-1).reshape(LRU_BLOCKS, 1, 2 * LRU_BLOCK_WIDTH)
    y_lru = _rglru(z, conv_w, row(conv_b), w_rg, b_rg, row(lru_lambda), batch=batch, seq=seq,
                   u_col=Q_WIDTH // LRU_BLOCK_WIDTH, gate_col=(Q_WIDTH + LRU_WIDTH) // LRU_BLOCK_WIDTH)

    merged = _merge(y_attn, y_lru, z, w_attn_proj.astype(BF16), w_lru_proj.astype(BF16),
                    ga_col=(Q_WIDTH + 2 * LRU_WIDTH) // D_MODEL, gl_col=(Q_WIDTH + 2 * LRU_WIDTH + D_MODEL) // D_MODEL)
    h1, xn2, expert_ids, expert_w = _outproj_router(
        h, merged, w_out.astype(BF16), row(norm_ffn_g), w_router.T.astype(BF16), b_router.reshape(N_EXPERTS, 1))

    pos, item_expert, item_nvb, n_used, n_sorted = _routing_tables(expert_ids, n)
    x_sorted = _dispatch(xn2, pos, n_sorted)
    y_sorted = _moe_mlp(x_sorted, w_mlp1, b_mlp1[:, None, :], w_mlp2, b_mlp2[:, None, :], item_expert, item_nvb, n_used)
    moe = _combine(y_sorted, pos, expert_w.T)

    return _final(h1, moe, p_l, row(norm_ple_g), w_ple_gate.astype(BF16), w_ple.astype(BF16), row(norm_final_g),
                  final_norm=final_norm)


def kernel(x, p, norm_mix_g, w_in, b_in, conv_w, conv_b, w_rg_a, b_rg_a, w_rg_x, b_rg_x, lru_lambda, attn_sinks, w_attn_proj, w_lru_proj, w_out, norm_ffn_g, w_router, b_router, w_mlp1, b_mlp1, w_mlp2, b_mlp2, norm_ple_g, w_ple, w_ple_gate, norm_final_g):
    batch, seq, d = x.shape
    depth = p.shape[0]
    h = x.reshape(batch * seq, d)
    for l in range(depth):
        h = _layer(h, p[l].reshape(batch * seq, -1), norm_mix_g[l], w_in[l], b_in[l], conv_w[l], conv_b[l],
                   w_rg_a[l], b_rg_a[l], w_rg_x[l], b_rg_x[l], lru_lambda[l], attn_sinks[l], w_attn_proj[l],
                   w_lru_proj[l], w_out[l], norm_ffn_g[l], w_router[l], b_router[l], w_mlp1[l], b_mlp1[l],
                   w_mlp2[l], b_mlp2[l], norm_ple_g[l], w_ple[l], w_ple_gate[l], norm_final_g,
                   batch=batch, seq=seq, final_norm=(l == depth - 1))
    return h.reshape(batch, seq, d)
```

```python
import functools

import jax
import jax.numpy as jnp
from jax import lax
from jax.experimental import pallas as pl
from jax.experimental.pallas import tpu as pltpu

D_MODEL = 2048
N_Q_HEADS = 32
N_KV_HEADS = 4
HEAD_DIM = 64
Q_WIDTH = N_Q_HEADS * HEAD_DIM
KV_WIDTH = N_KV_HEADS * HEAD_DIM
WINDOW = 128
LRU_WIDTH = D_MODEL
LRU_BLOCKS = 8
LRU_BLOCK_WIDTH = LRU_WIDTH // LRU_BLOCKS
CONV_WIDTH = 4
LRU_C = 8.0
IN_WIDTH = Q_WIDTH + 2 * KV_WIDTH + 2 * LRU_WIDTH + 2 * D_MODEL
N_EXPERTS = 32
TOP_K = 4
D_EXPERT = D_MODEL
SWIGLU_LIMIT = 7.0
SWIGLU_ALPHA = 1.702
EXPERT_BLOCK = 256
EXPERT_ITEM = 768
FF_CHUNK = 512
PLE_DIM = 256
RMS_EPS = 1e-6

VMEM_LIMIT_BYTES = 56 * 1024 * 1024
BF16_SUBLANES = 16

F32 = jnp.float32
BF16 = jnp.bfloat16


def _params(*sem):
    return pltpu.CompilerParams(dimension_semantics=sem, vmem_limit_bytes=VMEM_LIMIT_BYTES)


def _resident(shape, index_map):
    return pl.BlockSpec(shape, index_map, pipeline_mode=pl.Buffered(1))


def _rms_scale(x):
    var = jnp.mean(x * x, axis=-1, keepdims=True)
    return x * lax.rsqrt(var + RMS_EPS)


def _pack_bf16_pairs(x):
    c = x.shape[1] // 2
    xb = x.astype(BF16).astype(F32)
    lo = lax.bitcast_convert_type(xb[:, :c], jnp.uint32) >> 16
    hi = lax.bitcast_convert_type(xb[:, c:], jnp.uint32) & jnp.uint32(0xFFFF0000)
    return hi | lo


def _unpack_bf16_pairs(w):
    lo = lax.bitcast_convert_type(w << 16, F32)
    hi = lax.bitcast_convert_type(w & jnp.uint32(0xFFFF0000), F32)
    return lo, hi


def _inproj_body(x_ref, g_ref, w_ref, b_ref, o_ref, xn_ref):
    @pl.when(pl.program_id(1) == 0)
    def _():
        xn_ref[...] = (_rms_scale(x_ref[...]) * g_ref[...]).astype(BF16)

    acc = jnp.dot(xn_ref[...], w_ref[...], preferred_element_type=F32)
    o_ref[...] = (acc + b_ref[...]).astype(o_ref.dtype)


def _inproj(x2, g, w, b, *, tm=1024, tn=1536):
    n, d = x2.shape
    width = w.shape[1]
    return pl.pallas_call(
        _inproj_body,
        grid=(n // tm, width // tn),
        in_specs=[
            pl.BlockSpec((tm, d), lambda i, j: (i, 0)),
            pl.BlockSpec((1, d), lambda i, j: (0, 0)),
            pl.BlockSpec((d, tn), lambda i, j: (0, j)),
            pl.BlockSpec((1, tn), lambda i, j: (0, j)),
        ],
        out_specs=pl.BlockSpec((tm, tn), lambda i, j: (i, j)),
        out_shape=jax.ShapeDtypeStruct((n, width), BF16),
        scratch_shapes=[pltpu.VMEM((tm, d), BF16)],
        compiler_params=_params("arbitrary", "arbitrary"),
        name="inproj",
    )(x2, g, w, b)


def _block_diag_pair(pair, odd, lo_mask):
    swapped = jnp.concatenate([pair[:, HEAD_DIM:], pair[:, :HEAD_DIM]], axis=1)
    zero = jnp.zeros_like(pair)
    if odd:
        lo = jnp.where(lo_mask, swapped, zero)
        hi = jnp.where(lo_mask, zero, pair)
    else:
        lo = jnp.where(lo_mask, pair, zero)
        hi = jnp.where(lo_mask, zero, swapped)
    return jnp.concatenate([lo, hi], axis=0)


def _attn_body(sinks_ref, q_ref, kvp_ref, kvc_ref, o_ref, *, tq):
    nblk = tq // WINDOW
    group = N_Q_HEADS // N_KV_HEADS
    pairs = group // 2
    rows = pairs * WINDOW
    band_w = 2 * WINDOW
    s_idx = pl.program_id(1)

    lo_mask = lax.broadcasted_iota(jnp.int32, (1, 2 * HEAD_DIM), 1) < HEAD_DIM
    qi = lax.broadcasted_iota(jnp.int32, (rows, band_w), 0) % WINDOW
    kj = lax.broadcasted_iota(jnp.int32, (rows, band_w), 1)
    in_band = (kj > qi) & (kj <= qi + WINDOW)
    first_key = jnp.where(s_idx > 0, 0, WINDOW)
    row_pair = lax.broadcasted_iota(jnp.int32, (rows, 1), 0) // WINDOW
    neg = jnp.finfo(F32).min

    for i in range(nblk):
        if i == 0:
            band = jnp.concatenate([kvp_ref[...], kvc_ref[0:WINDOW, :]], axis=0)
            mask = in_band & (kj >= first_key)
        else:
            band = kvc_ref[(i - 1) * WINDOW:(i + 1) * WINDOW, :]
            mask = in_band
        for kh in range(N_KV_HEADS):
            col = (kh // 2) * 2 * HEAD_DIM
            kk = _block_diag_pair(band[:, col:col + 2 * HEAD_DIM], kh % 2 == 1, lo_mask)
            vv = _block_diag_pair(band[:, KV_WIDTH + col:KV_WIDTH + col + 2 * HEAD_DIM], kh % 2 == 1, lo_mask)
            qbase = kh * group * HEAD_DIM
            q4 = jnp.concatenate(
                [q_ref[i * WINDOW:(i + 1) * WINDOW, qbase + 2 * HEAD_DIM * j:qbase + 2 * HEAD_DIM * (j + 1)]
                 for j in range(pairs)], axis=0)
            s = lax.dot_general(q4, kk, (((1,), (1,)), ((), ())), preferred_element_type=F32)
            s = s * (HEAD_DIM ** -0.5)
            sink_lo = jnp.zeros((rows, 1), F32)
            sink_hi = jnp.zeros((rows, 1), F32)
            for j in range(pairs):
                sink_lo = jnp.where(row_pair == j, sinks_ref[kh * group + 2 * j], sink_lo)
                sink_hi = jnp.where(row_pair == j, sinks_ref[kh * group + 2 * j + 1], sink_hi)
            halves = []
            inv = []
            for half, sink in ((0, sink_lo), (1, sink_hi)):
                sh = jnp.where(mask, s[:, half * band_w:(half + 1) * band_w], neg)
                m = jnp.maximum(jnp.max(sh, axis=1, keepdims=True), sink)
                p = jnp.exp(sh - m)
                denom = jnp.sum(p, axis=1, keepdims=True) + jnp.exp(sink - m)
                halves.append(p.astype(BF16))
                inv.append(1.0 / denom)
            pcat = jnp.concatenate(halves, axis=1)
            o = jnp.dot(pcat, vv, preferred_element_type=F32)
            o = o * jnp.where(lo_mask, inv[0], inv[1])
            for j in range(pairs):
                o_ref[i * WINDOW:(i + 1) * WINDOW, qbase + 2 * HEAD_DIM * j:qbase + 2 * HEAD_DIM * (j + 1)] = (
                    o[j * WINDOW:(j + 1) * WINDOW, :].astype(o_ref.dtype))


def _attention(z, sinks, *, batch, seq, q_col, kv_col, tq=512):
    n = z.shape[0]
    spb = seq // tq
    wpb = seq // WINDOW
    kvw = 2 * KV_WIDTH

    def prev_map(b, s, sinks_ref):
        return (jnp.maximum(b * wpb + s * (tq // WINDOW) - 1, 0), kv_col)

    return pl.pallas_call(
        functools.partial(_attn_body, tq=tq),
        grid_spec=pltpu.PrefetchScalarGridSpec(
            num_scalar_prefetch=1,
            grid=(batch, spb),
            in_specs=[
                pl.BlockSpec((tq, Q_WIDTH), lambda b, s, sinks_ref: (b * spb + s, q_col)),
                pl.BlockSpec((WINDOW, kvw), prev_map),
                pl.BlockSpec((tq, kvw), lambda b, s, sinks_ref: (b * spb + s, kv_col)),
            ],
            out_specs=pl.BlockSpec((tq, Q_WIDTH), lambda b, s, sinks_ref: (b * spb + s, 0)),
        ),
        out_shape=jax.ShapeDtypeStruct((n, Q_WIDTH), BF16),
        compiler_params=_params("arbitrary", "arbitrary"),
        name="attn",
    )(sinks, z, z, z)


def _gelu_tanh(x):
    c = (2.0 / jnp.pi) ** 0.5
    return x * (0.5 * (1.0 + jnp.tanh(c * (x + 0.044715 * (x * x * x)))))


def _lru_body(u_ref, up_ref, gate_ref, cw_ref, cb_ref, wrg_ref, brg_ref, lam_ref, y_ref,
              ext_ref, a_ref, b_ref, h_ref):
    s_idx = pl.program_id(2)
    tl, w = u_ref.shape
    pad = 8

    @pl.when(s_idx == 0)
    def _():
        ext_ref[0:pad, :] = jnp.zeros((pad, w), F32)
        h_ref[...] = jnp.zeros_like(h_ref)

    @pl.when(s_idx > 0)
    def _():
        ext_ref[0:pad, :] = up_ref[...].astype(F32)[BF16_SUBLANES - pad:BF16_SUBLANES, :]

    ext_ref[pad:pad + tl, :] = u_ref[...].astype(F32)
    first = pad - (CONV_WIDTH - 1)
    uc = cb_ref[...] + ext_ref[first:first + tl, :] * cw_ref[0:1, :]
    for tap in range(1, CONV_WIDTH):
        uc = uc + ext_ref[first + tap:first + tap + tl, :] * cw_ref[tap:tap + 1, :]

    gates = jnp.dot(uc.astype(BF16), wrg_ref[...], preferred_element_type=F32) + brg_ref[...]
    r = jax.nn.sigmoid(gates[:, :w])
    ig = jax.nn.sigmoid(gates[:, w:])
    neg_lam = -lam_ref[...]
    softplus = jnp.maximum(neg_lam, 0.0) + jnp.log1p(jnp.exp(-jnp.abs(neg_lam)))
    log_a = (-LRU_C * r) * softplus
    a = jnp.exp(log_a)
    mult = jnp.sqrt(-jnp.tanh(log_a) * (a * a + 1.0))
    row = lax.broadcasted_iota(jnp.int32, (tl, 1), 0)
    mult = jnp.where((row == 0) & (s_idx == 0), 1.0, mult)
    b = mult * (ig * uc)

    sub = row % 8
    for d in (1, 2, 4):
        keep = sub >= d
        b_prev = jnp.where(keep, pltpu.roll(b, d, 0), 0.0)
        a_prev = jnp.where(keep, pltpu.roll(a, d, 0), 1.0)
        b = b + a * b_prev
        a = a * a_prev
    a_ref[...] = a
    b_ref[...] = b

    def group_step(gi, h):
        off = pl.multiple_of(gi * 8, 8)
        hg = b_ref[pl.ds(off, 8), :] + a_ref[pl.ds(off, 8), :] * h
        b_ref[pl.ds(off, 8), :] = hg
        return hg[7:8, :]

    h_ref[...] = lax.fori_loop(0, tl // 8, group_step, h_ref[...])
    y_ref[...] = (b_ref[...] * _gelu_tanh(gate_ref[...].astype(F32))).astype(y_ref.dtype)


def _rglru(z, conv_w, conv_b, w_rg, b_rg, lam, *, batch, seq, u_col, gate_col, tl=512):
    n = z.shape[0]
    w = LRU_BLOCK_WIDTH
    spb = seq // tl
    rpb = seq // BF16_SUBLANES

    def prev_map(b, c, s):
        return (jnp.maximum(b * rpb + s * (tl // BF16_SUBLANES) - 1, 0), u_col + c)

    return pl.pallas_call(
        _lru_body,
        grid=(batch, LRU_BLOCKS, spb),
        in_specs=[
            pl.BlockSpec((tl, w), lambda b, c, s: (b * spb + s, u_col + c)),
            pl.BlockSpec((BF16_SUBLANES, w), prev_map),
            pl.BlockSpec((tl, w), lambda b, c, s: (b * spb + s, gate_col + c)),
            pl.BlockSpec((CONV_WIDTH, w), lambda b, c, s: (0, c)),
            pl.BlockSpec((1, w), lambda b, c, s: (0, c)),
            pl.BlockSpec((None, w, 2 * w), lambda b, c, s: (c, 0, 0)),
            pl.BlockSpec((None, 1, 2 * w), lambda b, c, s: (c, 0, 0)),
            pl.BlockSpec((1, w), lambda b, c, s: (0, c)),
        ],
        out_specs=pl.BlockSpec((tl, w), lambda b, c, s: (b * spb + s, c)),
        out_shape=jax.ShapeDtypeStruct((n, LRU_WIDTH), BF16),
        scratch_shapes=[
            pltpu.VMEM((tl + 8, w), F32),
            pltpu.VMEM((tl, w), F32),
            pltpu.VMEM((tl, w), F32),
            pltpu.VMEM((1, w), F32),
        ],
        compiler_params=_params("arbitrary", "arbitrary", "arbitrary"),
        name="rglru",
    )(z, z, z, conv_w, conv_b, w_rg, b_rg, lam)


def _merge_body(ya_ref, yl_ref, ga_ref, gl_ref, pa_ref, pl_ref, o_ref):
    pa = jnp.dot(ya_ref[...], pa_ref[...], preferred_element_type=F32)
    pr = jnp.dot(yl_ref[...], pl_ref[...], preferred_element_type=F32)
    merged = jax.nn.sigmoid(ga_ref[...].astype(F32)) * pa + jax.nn.sigmoid(gl_ref[...].astype(F32)) * pr
    o_ref[...] = merged.astype(o_ref.dtype)


def _merge(y_attn, y_lru, z, w_attn_proj, w_lru_proj, *, ga_col, gl_col, tm=512):
    n, d = y_attn.shape
    return pl.pallas_call(
        _merge_body,
        grid=(n // tm,),
        in_specs=[
            pl.BlockSpec((tm, d), lambda i: (i, 0)),
            pl.BlockSpec((tm, d), lambda i: (i, 0)),
            pl.BlockSpec((tm, d), lambda i: (i, ga_col)),
            pl.BlockSpec((tm, d), lambda i: (i, gl_col)),
            _resident((d, d), lambda i: (0, 0)),
            _resident((d, d), lambda i: (0, 0)),
        ],
        out_specs=pl.BlockSpec((tm, d), lambda i: (i, 0)),
        out_shape=jax.ShapeDtypeStruct((n, d), BF16),
        compiler_params=_params("arbitrary"),
        name="merge",
    )(y_attn, y_lru, z, z, w_attn_proj, w_lru_proj)


def _outproj_body(x_ref, m_ref, wo_ref, g_ref, wr_ref, br_ref, h_ref, xn_ref, idx_ref, wt_ref):
    h = x_ref[...] + jnp.dot(m_ref[...], wo_ref[...], preferred_element_type=F32)
    h_ref[...] = h
    xn = _rms_scale(h) * g_ref[...]
    xn_ref[...] = _pack_bf16_pairs(xn)
    logits = lax.dot_general(wr_ref[...], xn.astype(BF16), (((1,), (1,)), ((), ())),
                             preferred_element_type=F32) + br_ref[...]
    expert = lax.broadcasted_iota(jnp.int32, logits.shape, 0).astype(F32)
    vals = []
    for k in range(TOP_K):
        m = jnp.max(logits, axis=0, keepdims=True)
        sel = jnp.min(jnp.where(logits == m, expert, float(N_EXPERTS)), axis=0, keepdims=True)
        vals.append(m)
        idx_ref[k:k + 1, :] = sel.astype(jnp.int32)
        logits = jnp.where(expert == sel, -jnp.inf, logits)
    exps = [jnp.exp(v - vals[0]) for v in vals]
    denom = exps[0] + exps[1] + exps[2] + exps[3]
    for k in range(TOP_K):
        wt_ref[k:k + 1, :] = exps[k] / denom


def _outproj_router(x2, merged, w_out, g_ffn, w_router_t, b_router, *, tm=512):
    n, d = x2.shape
    return pl.pallas_call(
        _outproj_body,
        grid=(n // tm,),
        in_specs=[
            pl.BlockSpec((tm, d), lambda i: (i, 0)),
            pl.BlockSpec((tm, d), lambda i: (i, 0)),
            _resident((d, d), lambda i: (0, 0)),
            pl.BlockSpec((1, d), lambda i: (0, 0)),
            pl.BlockSpec((N_EXPERTS, d), lambda i: (0, 0)),
            pl.BlockSpec((N_EXPERTS, 1), lambda i: (0, 0)),
        ],
        out_specs=[
            pl.BlockSpec((tm, d), lambda i: (i, 0)),
            pl.BlockSpec((tm, d // 2), lambda i: (i, 0)),
            pl.BlockSpec((TOP_K, tm), lambda i: (0, i)),
            pl.BlockSpec((TOP_K, tm), lambda i: (0, i)),
        ],
        out_shape=[
            jax.ShapeDtypeStruct((n, d), F32),
            jax.ShapeDtypeStruct((n, d // 2), jnp.uint32),
            jax.ShapeDtypeStruct((TOP_K, n), jnp.int32),
            jax.ShapeDtypeStruct((TOP_K, n), F32),
        ],
        compiler_params=_params("arbitrary"),
        name="outproj_router",
    )(x2, merged, w_out, g_ffn, w_router_t, b_router)


def _row_copy(src, dst, sem, src_row, dst_row):
    return pltpu.make_async_copy(src.at[pl.ds(src_row, 1), :], dst.at[pl.ds(dst_row, 1), :], sem)


def _dispatch_body(dst_ref, x_ref, init_hbm, out_hbm, stage, sems):
    del init_hbm
    i = pl.program_id(0)
    n = pl.num_programs(0)
    c = dst_ref.shape[1]
    slot = i % 2
    stage[slot] = x_ref[...]
    for r in range(c):
        _row_copy(stage.at[slot], out_hbm, sems.at[slot], r // TOP_K, dst_ref[0, r]).start()

    @pl.when(i > 0)
    def _():
        for r in range(c):
            _row_copy(stage.at[1 - slot], out_hbm, sems.at[1 - slot], 0, 0).wait()

    @pl.when(i == n - 1)
    def _():
        for r in range(c):
            _row_copy(stage.at[slot], out_hbm, sems.at[slot], 0, 0).wait()


def _dispatch(rows, pos, n_sorted, *, tm=128):
    n, w = rows.shape
    dst3 = pos.reshape(n // tm, 1, tm * TOP_K)
    init = jnp.zeros((n_sorted, w), rows.dtype)
    return pl.pallas_call(
        _dispatch_body,
        grid=(n // tm,),
        in_specs=[
            pl.BlockSpec((None, 1, tm * TOP_K), lambda i: (i, 0, 0), memory_space=pltpu.SMEM),
            pl.BlockSpec((tm, w), lambda i: (i, 0)),
            pl.BlockSpec(memory_space=pl.ANY),
        ],
        out_specs=pl.BlockSpec(memory_space=pl.ANY),
        out_shape=jax.ShapeDtypeStruct((n_sorted, w), rows.dtype),
        scratch_shapes=[pltpu.VMEM((2, tm, w), rows.dtype), pltpu.SemaphoreType.DMA((2,))],
        input_output_aliases={2: 0},
        compiler_params=_params("arbitrary"),
        name="dispatch",
    )(dst3, rows, init)


def _combine_body(idx_ref, nxt_ref, wt_ref, src_hbm, o_ref, buf, sems):
    i = pl.program_id(0)
    n = pl.num_programs(0)
    tm = o_ref.shape[0]
    count = TOP_K * tm
    slot = i % 2

    @pl.when(i == 0)
    def _():
        for r in range(count):
            _row_copy(src_hbm, buf.at[0], sems.at[0], idx_ref[0, r], r).start()

    @pl.when(i + 1 < n)
    def _():
        for r in range(count):
            _row_copy(src_hbm, buf.at[1 - slot], sems.at[1 - slot], nxt_ref[0, r], r).start()

    for r in range(count):
        _row_copy(src_hbm, buf.at[slot], sems.at[slot], 0, r).wait()

    c = o_ref.shape[1] // 2
    acc_lo = jnp.zeros((tm, c), F32)
    acc_hi = jnp.zeros((tm, c), F32)
    for k in range(TOP_K):
        lo, hi = _unpack_bf16_pairs(buf[slot, k * tm:(k + 1) * tm, :])
        acc_lo = acc_lo + wt_ref[:, k:k + 1] * lo
        acc_hi = acc_hi + wt_ref[:, k:k + 1] * hi
    o_ref[:, :c] = acc_lo
    o_ref[:, c:] = acc_hi


def _combine(y_sorted, pos, weights, *, tm=128):
    n = pos.shape[0]
    w = y_sorted.shape[1]
    steps = n // tm
    idx3 = pos.reshape(steps, tm, TOP_K).transpose(0, 2, 1).reshape(steps, 1, TOP_K * tm)
    return pl.pallas_call(
        _combine_body,
        grid=(steps,),
        in_specs=[
            pl.BlockSpec((None, 1, TOP_K * tm), lambda i: (i, 0, 0), memory_space=pltpu.SMEM),
            pl.BlockSpec((None, 1, TOP_K * tm), lambda i: (jnp.minimum(i + 1, steps - 1), 0, 0),
                         memory_space=pltpu.SMEM),
            pl.BlockSpec((tm, TOP_K), lambda i: (i, 0)),
            pl.BlockSpec(memory_space=pl.ANY),
        ],
        out_specs=pl.BlockSpec((tm, 2 * w), lambda i: (i, 0)),
        out_shape=jax.ShapeDtypeStruct((n, 2 * w), F32),
        scratch_shapes=[pltpu.VMEM((2, TOP_K * tm, w), y_sorted.dtype), pltpu.SemaphoreType.DMA((2,))],
        compiler_params=_params("arbitrary"),
        name="combine",
    )(idx3, idx3, weights, y_sorted)


def _moe_mlp_body(ie_ref, nvb_ref, nu_ref, x_ref, wg_ref, wl_ref, w2_ref, bg_ref, bl_ref, b2_ref, o_ref, acc_ref):
    item = pl.program_id(0)
    chunk = pl.program_id(1)
    nvb = nvb_ref[item]

    @pl.when(chunk == 0)
    def _():
        acc_ref[...] = jnp.broadcast_to(b2_ref[...], acc_ref.shape)

    @pl.when(nvb > 0)
    def _():
        wg = wg_ref[...].astype(BF16)
        wl = wl_ref[...].astype(BF16)
        w2 = w2_ref[...].astype(BF16)
        for j in range(x_ref.shape[0] // EXPERT_BLOCK):
            @pl.when(j < nvb)
            def _():
                rows = slice(j * EXPERT_BLOCK, (j + 1) * EXPERT_BLOCK)
                lo, hi = _unpack_bf16_pairs(x_ref[rows, :])
                xj = jnp.concatenate([lo.astype(BF16), hi.astype(BF16)], axis=1)
                glu = jnp.dot(xj, wg, preferred_element_type=F32) + bg_ref[...]
                lin = jnp.dot(xj, wl, preferred_element_type=F32) + bl_ref[...]
                glu = jnp.minimum(glu, SWIGLU_LIMIT)
                lin = jnp.clip(lin, -SWIGLU_LIMIT, SWIGLU_LIMIT)
                act = glu * jax.nn.sigmoid(SWIGLU_ALPHA * glu) * (lin + 1.0)
                acc_ref[rows, :] += jnp.dot(act.astype(BF16), w2, preferred_element_type=F32)

    @pl.when(chunk == pl.num_programs(1) - 1)
    def _():
        o_ref[...] = _pack_bf16_pairs(acc_ref[...])


def _moe_mlp(x_sorted, w1, b1, w2, b2, item_expert, item_nvb, n_used):
    n_rows, half = x_sorted.shape
    d = 2 * half
    n_items = n_rows // EXPERT_ITEM
    f = w2.shape[1]
    nc = f // FF_CHUNK

    def live_chunk(i, c, nu):
        return jnp.where(i < nu[0], c, nc - 1)

    def rows_map(i, c, ie, nvb, nu):
        return (jnp.minimum(i, nu[0] - 1), 0)

    return pl.pallas_call(
        _moe_mlp_body,
        grid_spec=pltpu.PrefetchScalarGridSpec(
            num_scalar_prefetch=3,
            grid=(n_items, nc),
            in_specs=[
                pl.BlockSpec((EXPERT_ITEM, half), rows_map),
                pl.BlockSpec((None, d, FF_CHUNK), lambda i, c, ie, nvb, nu: (ie[i], 0, live_chunk(i, c, nu))),
                pl.BlockSpec((None, d, FF_CHUNK), lambda i, c, ie, nvb, nu: (ie[i], 0, nc + live_chunk(i, c, nu))),
                pl.BlockSpec((None, FF_CHUNK, d), lambda i, c, ie, nvb, nu: (ie[i], live_chunk(i, c, nu), 0)),
                pl.BlockSpec((None, 1, FF_CHUNK), lambda i, c, ie, nvb, nu: (ie[i], 0, live_chunk(i, c, nu))),
                pl.BlockSpec((None, 1, FF_CHUNK), lambda i, c, ie, nvb, nu: (ie[i], 0, nc + live_chunk(i, c, nu))),
                pl.BlockSpec((None, 1, d), lambda i, c, ie, nvb, nu: (ie[i], 0, 0)),
            ],
            out_specs=pl.BlockSpec((EXPERT_ITEM, half), lambda i, c, ie, nvb, nu: (i, 0)),
            scratch_shapes=[pltpu.VMEM((EXPERT_ITEM, d), F32)],
        ),
        out_shape=jax.ShapeDtypeStruct((n_rows, half), jnp.uint32),
        compiler_params=_params("arbitrary", "arbitrary"),
        name="moe_mlp",
    )(item_expert, item_nvb, n_used, x_sorted, w1, w1, w2, b1, b1, b2)


def _final_body(h_ref, moe_ref, p_ref, gp_ref, wg_ref, wp_ref, gf_ref, o_ref, *, final_norm):
    h = h_ref[...] + moe_ref[...]
    xn = (_rms_scale(h) * gp_ref[...]).astype(BF16)
    gate = jax.nn.sigmoid(jnp.dot(xn, wg_ref[...], preferred_element_type=F32))
    ple = jnp.dot(p_ref[...].astype(BF16), wp_ref[...], preferred_element_type=F32)
    h = h + gate * ple
    o_ref[...] = _rms_scale(h) * gf_ref[...] if final_norm else h


def _final(h1, moe, p2, g_ple, w_ple_gate, w_ple, g_final, *, final_norm, tm=512):
    n, d = h1.shape
    return pl.pallas_call(
        functools.partial(_final_body, final_norm=final_norm),
        grid=(n // tm,),
        in_specs=[
            pl.BlockSpec((tm, d), lambda i: (i, 0)),
            pl.BlockSpec((tm, d), lambda i: (i, 0)),
            pl.BlockSpec((tm, PLE_DIM), lambda i: (i, 0)),
            pl.BlockSpec((1, d), lambda i: (0, 0)),
            _resident((d, d), lambda i: (0, 0)),
            _resident((PLE_DIM, d), lambda i: (0, 0)),
            pl.BlockSpec((1, d), lambda i: (0, 0)),
        ],
        out_specs=pl.BlockSpec((tm, d), lambda i: (i, 0)),
        out_shape=jax.ShapeDtypeStruct((n, d), F32),
        compiler_params=_params("arbitrary"),
        name="final",
    )(h1, moe, p2, g_ple, w_ple_gate, w_ple, g_final)


def _routing_tables(expert_ids, n_tokens):
    nk = n_tokens * TOP_K
    sub = EXPERT_ITEM // EXPERT_BLOCK
    e_flat = expert_ids.T.reshape(nk)
    onehot = (e_flat[:, None] == jnp.arange(N_EXPERTS, dtype=jnp.int32)[None, :]).astype(jnp.int32)
    csum = jnp.cumsum(onehot, axis=0)
    rank = jnp.sum(onehot * csum, axis=1) - 1
    counts = csum[-1]
    padded = (counts + EXPERT_ITEM - 1) // EXPERT_ITEM * EXPERT_ITEM
    pend = jnp.cumsum(padded)
    pstart = pend - padded
    dest = pstart[e_flat] + rank
    n_items = nk // EXPERT_ITEM + N_EXPERTS
    n_used = (pend[-1] // EXPERT_ITEM).astype(jnp.int32)
    item_start = jnp.arange(n_items, dtype=jnp.int32) * EXPERT_ITEM
    item_expert = jnp.minimum(jnp.searchsorted(pend, item_start, side='right'), N_EXPERTS - 1).astype(jnp.int32)
    used = jnp.arange(n_items) < n_used
    item_expert = jnp.where(used, item_expert, item_expert[n_used - 1])
    rows_left = counts[item_expert] - (item_start - pstart[item_expert])
    item_nvb = jnp.where(used, jnp.clip((rows_left + EXPERT_BLOCK - 1) // EXPERT_BLOCK, 0, sub), 0).astype(jnp.int32)
    pos = dest.reshape(n_tokens, TOP_K).astype(jnp.int32)
    return pos, item_expert, item_nvb, n_used.reshape(1), n_items * EXPERT_ITEM


def _layer(h, p_l, norm_mix_g, w_in, b_in, conv_w, conv_b, w_rg_a, b_rg_a, w_rg_x, b_rg_x, lru_lambda,
           attn_sinks, w_attn_proj, w_lru_proj, w_out, norm_ffn_g, w_router, b_router, w_mlp1, b_mlp1,
           w_mlp2, b_mlp2, norm_ple_g, w_ple, w_ple_gate, norm_final_g, *, batch, seq, final_norm):
    n = h.shape[0]
    row = lambda v: v.reshape(1, -1)

    kv0 = Q_WIDTH
    rest0 = Q_WIDTH + 2 * KV_WIDTH
    reorder = lambda a: jnp.concatenate([a[..., :kv0], a[..., rest0:], a[..., kv0:rest0]], axis=-1)
    z = _inproj(h, row(norm_mix_g), reorder(w_in).astype(BF16), row(reorder(b_in)))

    kv_col = (Q_WIDTH + 2 * LRU_WIDTH + 2 * D_MODEL) // (2 * KV_WIDTH)
    y_attn = _attention(z, attn_sinks, batch=batch, seq=seq, q_col=0, kv_col=kv_col)

    w_rg = jnp.concatenate([w_rg_a, w_rg_x], axis=-1).astype(BF16)
    b_rg = jnp.concatenate([b_rg_a, b_rg_x], axis=-1).reshape(LRU_BLOCKS, 1, 2 * LRU_BLOCK_WIDTH)
    y_lru = _rglru(z, conv_w, row(conv_b), w_rg, b_rg, row(lru_lambda), batch=batch, seq=seq,
                   u_col=Q_WIDTH // LRU_BLOCK_WIDTH, gate_col=(Q_WIDTH + LRU_WIDTH) // LRU_BLOCK_WIDTH)

    merged = _merge(y_attn, y_lru, z, w_attn_proj.astype(BF16), w_lru_proj.astype(BF16),
                    ga_col=(Q_WIDTH + 2 * LRU_WIDTH) // D_MODEL, gl_col=(Q_WIDTH + 2 * LRU_WIDTH + D_MODEL) // D_MODEL)
    h1, xn2, expert_ids, expert_w = _outproj_router(
        h, merged, w_out.astype(BF16), row(norm_ffn_g), w_router.T.astype(BF16), b_router.reshape(N_EXPERTS, 1))

    pos, item_expert, item_nvb, n_used, n_sorted = _routing_tables(expert_ids, n)
    x_sorted = _dispatch(xn2, pos, n_sorted)
    y_sorted = _moe_mlp(x_sorted, w_mlp1, b_mlp1[:, None, :], w_mlp2, b_mlp2[:, None, :], item_expert, item_nvb, n_used)
    moe = _combine(y_sorted, pos, expert_w.T)

    return _final(h1, moe, p_l, row(norm_ple_g), w_ple_gate.astype(BF16), w_ple.astype(BF16), row(norm_final_g),
                  final_norm=final_norm)


def kernel(x, p, norm_mix_g, w_in, b_in, conv_w, conv_b, w_rg_a, b_rg_a, w_rg_x, b_rg_x, lru_lambda, attn_sinks, w_attn_proj, w_lru_proj, w_out, norm_ffn_g, w_router, b_router, w_mlp1, b_mlp1, w_mlp2, b_mlp2, norm_ple_g, w_ple, w_ple_gate, norm_final_g):
    batch, seq, d = x.shape
    depth = p.shape[0]
    h = x.reshape(batch * seq, d)
    for l in range(depth):
        h = _layer(h, p[l].reshape(batch * seq, -1), norm_mix_g[l], w_in[l], b_in[l], conv_w[l], conv_b[l],
                   w_rg_a[l], b_rg_a[l], w_rg_x[l], b_rg_x[l], lru_lambda[l], attn_sinks[l], w_attn_proj[l],
                   w_lru_proj[l], w_out[l], norm_ffn_g[l], w_router[l], b_router[l], w_mlp1[l], b_mlp1[l],
                   w_mlp2[l], b_mlp2[l], norm_ple_g[l], w_ple[l], w_ple_gate[l], norm_final_g,
                   batch=batch, seq=seq, final_norm=(l == depth - 1))
    return h.reshape(batch, seq, d)
```

```python
import functools

import jax
import jax.numpy as jnp
from jax import lax
from jax.experimental import pallas as pl
from jax.experimental.pallas import tpu as pltpu

D_MODEL = 2048
N_Q_HEADS = 32
N_KV_HEADS = 4
HEAD_DIM = 64
Q_WIDTH = N_Q_HEADS * HEAD_DIM
KV_WIDTH = N_KV_HEADS * HEAD_DIM
WINDOW = 128
LRU_WIDTH = D_MODEL
LRU_BLOCKS = 8
LRU_BLOCK_WIDTH = LRU_WIDTH // LRU_BLOCKS
CONV_WIDTH = 4
LRU_C = 8.0
IN_WIDTH = Q_WIDTH + 2 * KV_WIDTH + 2 * LRU_WIDTH + 2 * D_MODEL
N_EXPERTS = 32
TOP_K = 4
D_EXPERT = D_MODEL
SWIGLU_LIMIT = 7.0
SWIGLU_ALPHA = 1.702
EXPERT_BLOCK = 256
ITEM_BLOCKS = 9
FF_CHUNK = 512
PLE_DIM = 256
RMS_EPS = 1e-6

VMEM_LIMIT_BYTES = 56 * 1024 * 1024
MOE_VMEM_LIMIT_BYTES = 60 * 1024 * 1024
BF16_SUBLANES = 16

F32 = jnp.float32
BF16 = jnp.bfloat16


def _params(*sem):
    return pltpu.CompilerParams(dimension_semantics=sem, vmem_limit_bytes=VMEM_LIMIT_BYTES)


def _resident(shape, index_map):
    return pl.BlockSpec(shape, index_map, pipeline_mode=pl.Buffered(1))


def _rms_scale(x):
    var = jnp.mean(x * x, axis=-1, keepdims=True)
    return x * lax.rsqrt(var + RMS_EPS)


def _pack_bf16_pairs(x):
    c = x.shape[1] // 2
    xb = x.astype(BF16).astype(F32)
    lo = lax.bitcast_convert_type(xb[:, :c], jnp.uint32) >> 16
    hi = lax.bitcast_convert_type(xb[:, c:], jnp.uint32) & jnp.uint32(0xFFFF0000)
    return hi | lo


def _unpack_bf16_pairs(w):
    lo = lax.bitcast_convert_type(w << 16, F32)
    hi = lax.bitcast_convert_type(w & jnp.uint32(0xFFFF0000), F32)
    return lo, hi


def _inproj_body(x_ref, g_ref, w_ref, b_ref, o_ref, xn_ref):
    @pl.when(pl.program_id(1) == 0)
    def _():
        xn_ref[...] = (_rms_scale(x_ref[...]) * g_ref[...]).astype(BF16)

    acc = jnp.dot(xn_ref[...], w_ref[...], preferred_element_type=F32)
    o_ref[...] = (acc + b_ref[...]).astype(o_ref.dtype)


def _inproj(x2, g, w, b, *, tm=1024, tn=1536):
    n, d = x2.shape
    width = w.shape[1]
    return pl.pallas_call(
        _inproj_body,
        grid=(n // tm, width // tn),
        in_specs=[
            pl.BlockSpec((tm, d), lambda i, j: (i, 0)),
            pl.BlockSpec((1, d), lambda i, j: (0, 0)),
            pl.BlockSpec((d, tn), lambda i, j: (0, j)),
            pl.BlockSpec((1, tn), lambda i, j: (0, j)),
        ],
        out_specs=pl.BlockSpec((tm, tn), lambda i, j: (i, j)),
        out_shape=jax.ShapeDtypeStruct((n, width), BF16),
        scratch_shapes=[pltpu.VMEM((tm, d), BF16)],
        compiler_params=_params("arbitrary", "arbitrary"),
        name="inproj",
    )(x2, g, w, b)


def _block_diag_pair(pair, odd, lo_mask):
    swapped = jnp.concatenate([pair[:, HEAD_DIM:], pair[:, :HEAD_DIM]], axis=1)
    zero = jnp.zeros_like(pair)
    if odd:
        lo = jnp.where(lo_mask, swapped, zero)
        hi = jnp.where(lo_mask, zero, pair)
    else:
        lo = jnp.where(lo_mask, pair, zero)
        hi = jnp.where(lo_mask, zero, swapped)
    return jnp.concatenate([lo, hi], axis=0)


def _attn_body(sinks_ref, q_ref, kvp_ref, kvc_ref, o_ref, *, tq):
    nblk = tq // WINDOW
    group = N_Q_HEADS // N_KV_HEADS
    pairs = group // 2
    rows = pairs * WINDOW
    band_w = 2 * WINDOW
    s_idx = pl.program_id(1)

    lo_mask = lax.broadcasted_iota(jnp.int32, (1, 2 * HEAD_DIM), 1) < HEAD_DIM
    qi = lax.broadcasted_iota(jnp.int32, (rows, band_w), 0) % WINDOW
    kj = lax.broadcasted_iota(jnp.int32, (rows, band_w), 1)
    in_band = (kj > qi) & (kj <= qi + WINDOW)
    first_key = jnp.where(s_idx > 0, 0, WINDOW)
    row_pair = lax.broadcasted_iota(jnp.int32, (rows, 1), 0) // WINDOW
    neg = jnp.finfo(F32).min

    for i in range(nblk):
        if i == 0:
            band = jnp.concatenate([kvp_ref[...], kvc_ref[0:WINDOW, :]], axis=0)
            mask = in_band & (kj >= first_key)
        else:
            band = kvc_ref[(i - 1) * WINDOW:(i + 1) * WINDOW, :]
            mask = in_band
        for kh in range(N_KV_HEADS):
            col = (kh // 2) * 2 * HEAD_DIM
            kk = _block_diag_pair(band[:, col:col + 2 * HEAD_DIM], kh % 2 == 1, lo_mask)
            vv = _block_diag_pair(band[:, KV_WIDTH + col:KV_WIDTH + col + 2 * HEAD_DIM], kh % 2 == 1, lo_mask)
            qbase = kh * group * HEAD_DIM
            q4 = jnp.concatenate(
                [q_ref[i * WINDOW:(i + 1) * WINDOW, qbase + 2 * HEAD_DIM * j:qbase + 2 * HEAD_DIM * (j + 1)]
                 for j in range(pairs)], axis=0)
            s = lax.dot_general(q4, kk, (((1,), (1,)), ((), ())), preferred_element_type=F32)
            s = s * (HEAD_DIM ** -0.5)
            sink_lo = jnp.zeros((rows, 1), F32)
            sink_hi = jnp.zeros((rows, 1), F32)
            for j in range(pairs):
                sink_lo = jnp.where(row_pair == j, sinks_ref[kh * group + 2 * j], sink_lo)
                sink_hi = jnp.where(row_pair == j, sinks_ref[kh * group + 2 * j + 1], sink_hi)
            halves = []
            inv = []
            for half, sink in ((0, sink_lo), (1, sink_hi)):
                sh = jnp.where(mask, s[:, half * band_w:(half + 1) * band_w], neg)
                m = jnp.maximum(jnp.max(sh, axis=1, keepdims=True), sink)
                p = jnp.exp(sh - m)
                denom = jnp.sum(p, axis=1, keepdims=True) + jnp.exp(sink - m)
                halves.append(p.astype(BF16))
                inv.append(1.0 / denom)
            pcat = jnp.concatenate(halves, axis=1)
            o = jnp.dot(pcat, vv, preferred_element_type=F32)
            o = o * jnp.where(lo_mask, inv[0], inv[1])
            for j in range(pairs):
                o_ref[i * WINDOW:(i + 1) * WINDOW, qbase + 2 * HEAD_DIM * j:qbase + 2 * HEAD_DIM * (j + 1)] = (
                    o[j * WINDOW:(j + 1) * WINDOW, :].astype(o_ref.dtype))


def _attention(z, sinks, *, batch, seq, q_col, kv_col, tq=512):
    n = z.shape[0]
    spb = seq // tq
    wpb = seq // WINDOW
    kvw = 2 * KV_WIDTH

    def prev_map(b, s, sinks_ref):
        return (jnp.maximum(b * wpb + s * (tq // WINDOW) - 1, 0), kv_col)

    return pl.pallas_call(
        functools.partial(_attn_body, tq=tq),
        grid_spec=pltpu.PrefetchScalarGridSpec(
            num_scalar_prefetch=1,
            grid=(batch, spb),
            in_specs=[
                pl.BlockSpec((tq, Q_WIDTH), lambda b, s, sinks_ref: (b * spb + s, q_col)),
                pl.BlockSpec((WINDOW, kvw), prev_map),
                pl.BlockSpec((tq, kvw), lambda b, s, sinks_ref: (b * spb + s, kv_col)),
            ],
            out_specs=pl.BlockSpec((tq, Q_WIDTH), lambda b, s, sinks_ref: (b * spb + s, 0)),
        ),
        out_shape=jax.ShapeDtypeStruct((n, Q_WIDTH), BF16),
        compiler_params=_params("arbitrary", "arbitrary"),
        name="attn",
    )(sinks, z, z, z)


def _gelu_tanh(x):
    c = (2.0 / jnp.pi) ** 0.5
    return x * (0.5 * (1.0 + jnp.tanh(c * (x + 0.044715 * (x * x * x)))))


def _lru_body(u_ref, up_ref, gate_ref, cw_ref, cb_ref, wrg_ref, brg_ref, lam_ref, y_ref,
              ext_ref, a_ref, b_ref, h_ref):
    s_idx = pl.program_id(2)
    tl, w = u_ref.shape
    pad = 8

    @pl.when(s_idx == 0)
    def _():
        ext_ref[0:pad, :] = jnp.zeros((pad, w), F32)
        h_ref[...] = jnp.zeros_like(h_ref)

    @pl.when(s_idx > 0)
    def _():
        ext_ref[0:pad, :] = up_ref[...].astype(F32)[BF16_SUBLANES - pad:BF16_SUBLANES, :]

    ext_ref[pad:pad + tl, :] = u_ref[...].astype(F32)
    first = pad - (CONV_WIDTH - 1)
    uc = cb_ref[...] + ext_ref[first:first + tl, :] * cw_ref[0:1, :]
    for tap in range(1, CONV_WIDTH):
        uc = uc + ext_ref[first + tap:first + tap + tl, :] * cw_ref[tap:tap + 1, :]

    gates = jnp.dot(uc.astype(BF16), wrg_ref[...], preferred_element_type=F32) + brg_ref[...]
    r = jax.nn.sigmoid(gates[:, :w])
    ig = jax.nn.sigmoid(gates[:, w:])
    neg_lam = -lam_ref[...]
    softplus = jnp.maximum(neg_lam, 0.0) + jnp.log1p(jnp.exp(-jnp.abs(neg_lam)))
    log_a = (-LRU_C * r) * softplus
    a = jnp.exp(log_a)
    mult = jnp.sqrt(-jnp.tanh(log_a) * (a * a + 1.0))
    row = lax.broadcasted_iota(jnp.int32, (tl, 1), 0)
    mult = jnp.where((row == 0) & (s_idx == 0), 1.0, mult)
    b = mult * (ig * uc)

    sub = row % 8
    for d in (1, 2, 4):
        keep = sub >= d
        b_prev = jnp.where(keep, pltpu.roll(b, d, 0), 0.0)
        a_prev = jnp.where(keep, pltpu.roll(a, d, 0), 1.0)
        b = b + a * b_prev
        a = a * a_prev
    a_ref[...] = a
    b_ref[...] = b

    def group_step(gi, h):
        off = pl.multiple_of(gi * 8, 8)
        hg = b_ref[pl.ds(off, 8), :] + a_ref[pl.ds(off, 8), :] * h
        b_ref[pl.ds(off, 8), :] = hg
        return hg[7:8, :]

    h_ref[...] = lax.fori_loop(0, tl // 8, group_step, h_ref[...])
    y_ref[...] = (b_ref[...] * _gelu_tanh(gate_ref[...].astype(F32))).astype(y_ref.dtype)


def _rglru(z, conv_w, conv_b, w_rg, b_rg, lam, *, batch, seq, u_col, gate_col, tl=512):
    n = z.shape[0]
    w = LRU_BLOCK_WIDTH
    spb = seq // tl
    rpb = seq // BF16_SUBLANES

    def prev_map(b, c, s):
        return (jnp.maximum(b * rpb + s * (tl // BF16_SUBLANES) - 1, 0), u_col + c)

    return pl.pallas_call(
        _lru_body,
        grid=(batch, LRU_BLOCKS, spb),
        in_specs=[
            pl.BlockSpec((tl, w), lambda b, c, s: (b * spb + s, u_col + c)),
            pl.BlockSpec((BF16_SUBLANES, w), prev_map),
            pl.BlockSpec((tl, w), lambda b, c, s: (b * spb + s, gate_col + c)),
            pl.BlockSpec((CONV_WIDTH, w), lambda b, c, s: (0, c)),
            pl.BlockSpec((1, w), lambda b, c, s: (0, c)),
            pl.BlockSpec((None, w, 2 * w), lambda b, c, s: (c, 0, 0)),
            pl.BlockSpec((None, 1, 2 * w), lambda b, c, s: (c, 0, 0)),
            pl.BlockSpec((1, w), lambda b, c, s: (0, c)),
        ],
        out_specs=pl.BlockSpec((tl, w), lambda b, c, s: (b * spb + s, c)),
        out_shape=jax.ShapeDtypeStruct((n, LRU_WIDTH), BF16),
        scratch_shapes=[
            pltpu.VMEM((tl + 8, w), F32),
            pltpu.VMEM((tl, w), F32),
            pltpu.VMEM((tl, w), F32),
            pltpu.VMEM((1, w), F32),
        ],
        compiler_params=_params("arbitrary", "arbitrary", "arbitrary"),
        name="rglru",
    )(z, z, z, conv_w, conv_b, w_rg, b_rg, lam)


def _merge_body(ya_ref, yl_ref, ga_ref, gl_ref, pa_ref, pl_ref, o_ref):
    pa = jnp.dot(ya_ref[...], pa_ref[...], preferred_element_type=F32)
    pr = jnp.dot(yl_ref[...], pl_ref[...], preferred_element_type=F32)
    merged = jax.nn.sigmoid(ga_ref[...].astype(F32)) * pa + jax.nn.sigmoid(gl_ref[...].astype(F32)) * pr
    o_ref[...] = merged.astype(o_ref.dtype)


def _merge(y_attn, y_lru, z, w_attn_proj, w_lru_proj, *, ga_col, gl_col, tm=512):
    n, d = y_attn.shape
    return pl.pallas_call(
        _merge_body,
        grid=(n // tm,),
        in_specs=[
            pl.BlockSpec((tm, d), lambda i: (i, 0)),
            pl.BlockSpec((tm, d), lambda i: (i, 0)),
            pl.BlockSpec((tm, d), lambda i: (i, ga_col)),
            pl.BlockSpec((tm, d), lambda i: (i, gl_col)),
            _resident((d, d), lambda i: (0, 0)),
            _resident((d, d), lambda i: (0, 0)),
        ],
        out_specs=pl.BlockSpec((tm, d), lambda i: (i, 0)),
        out_shape=jax.ShapeDtypeStruct((n, d), BF16),
        compiler_params=_params("arbitrary"),
        name="merge",
    )(y_attn, y_lru, z, z, w_attn_proj, w_lru_proj)


def _outproj_body(x_ref, m_ref, wo_ref, g_ref, wr_ref, br_ref, h_ref, xn_ref, idx_ref, wt_ref):
    h = x_ref[...] + jnp.dot(m_ref[...], wo_ref[...], preferred_element_type=F32)
    h_ref[...] = h
    xn = _rms_scale(h) * g_ref[...]
    xn_ref[...] = _pack_bf16_pairs(xn)
    logits = lax.dot_general(wr_ref[...], xn.astype(BF16), (((1,), (1,)), ((), ())),
                             preferred_element_type=F32) + br_ref[...]
    expert = lax.broadcasted_iota(jnp.int32, logits.shape, 0).astype(F32)
    vals = []
    for k in range(TOP_K):
        m = jnp.max(logits, axis=0, keepdims=True)
        sel = jnp.min(jnp.where(logits == m, expert, float(N_EXPERTS)), axis=0, keepdims=True)
        vals.append(m)
        idx_ref[k:k + 1, :] = sel.astype(jnp.int32)
        logits = jnp.where(expert == sel, -jnp.inf, logits)
    exps = [jnp.exp(v - vals[0]) for v in vals]
    denom = exps[0] + exps[1] + exps[2] + exps[3]
    for k in range(TOP_K):
        wt_ref[k:k + 1, :] = exps[k] / denom


def _outproj_router(x2, merged, w_out, g_ffn, w_router_t, b_router, *, tm=512):
    n, d = x2.shape
    return pl.pallas_call(
        _outproj_body,
        grid=(n // tm,),
        in_specs=[
            pl.BlockSpec((tm, d), lambda i: (i, 0)),
            pl.BlockSpec((tm, d), lambda i: (i, 0)),
            _resident((d, d), lambda i: (0, 0)),
            pl.BlockSpec((1, d), lambda i: (0, 0)),
            pl.BlockSpec((N_EXPERTS, d), lambda i: (0, 0)),
            pl.BlockSpec((N_EXPERTS, 1), lambda i: (0, 0)),
        ],
        out_specs=[
            pl.BlockSpec((tm, d), lambda i: (i, 0)),
            pl.BlockSpec((tm, d // 2), lambda i: (i, 0)),
            pl.BlockSpec((TOP_K, tm), lambda i: (0, i)),
            pl.BlockSpec((TOP_K, tm), lambda i: (0, i)),
        ],
        out_shape=[
            jax.ShapeDtypeStruct((n, d), F32),
            jax.ShapeDtypeStruct((n, d // 2), jnp.uint32),
            jax.ShapeDtypeStruct((TOP_K, n), jnp.int32),
            jax.ShapeDtypeStruct((TOP_K, n), F32),
        ],
        compiler_params=_params("arbitrary"),
        name="outproj_router",
    )(x2, merged, w_out, g_ffn, w_router_t, b_router)


def _row_copy(src, dst, sem, src_row, dst_row):
    return pltpu.make_async_copy(src.at[pl.ds(src_row, 1), :], dst.at[pl.ds(dst_row, 1), :], sem)


def _dispatch_body(dst_ref, x_ref, init_hbm, out_hbm, stage, sems):
    del init_hbm
    i = pl.program_id(0)
    n = pl.num_programs(0)
    c = dst_ref.shape[1]
    slot = i % 2
    stage[slot] = x_ref[...]
    for r in range(c):
        _row_copy(stage.at[slot], out_hbm, sems.at[slot], r // TOP_K, dst_ref[0, r]).start()

    @pl.when(i > 0)
    def _():
        for r in range(c):
            _row_copy(stage.at[1 - slot], out_hbm, sems.at[1 - slot], 0, 0).wait()

    @pl.when(i == n - 1)
    def _():
        for r in range(c):
            _row_copy(stage.at[slot], out_hbm, sems.at[slot], 0, 0).wait()


def _dispatch(rows, pos, n_sorted, *, tm=128):
    n, w = rows.shape
    dst3 = pos.reshape(n // tm, 1, tm * TOP_K)
    init = jnp.zeros((n_sorted, w), rows.dtype)
    return pl.pallas_call(
        _dispatch_body,
        grid=(n // tm,),
        in_specs=[
            pl.BlockSpec((None, 1, tm * TOP_K), lambda i: (i, 0, 0), memory_space=pltpu.SMEM),
            pl.BlockSpec((tm, w), lambda i: (i, 0)),
            pl.BlockSpec(memory_space=pl.ANY),
        ],
        out_specs=pl.BlockSpec(memory_space=pl.ANY),
        out_shape=jax.ShapeDtypeStruct((n_sorted, w), rows.dtype),
        scratch_shapes=[pltpu.VMEM((2, tm, w), rows.dtype), pltpu.SemaphoreType.DMA((2,))],
        input_output_aliases={2: 0},
        compiler_params=_params("arbitrary"),
        name="dispatch",
    )(dst3, rows, init)


def _combine_body(idx_ref, nxt_ref, wt_ref, src_hbm, o_ref, buf, sems):
    i = pl.program_id(0)
    n = pl.num_programs(0)
    tm = o_ref.shape[0]
    count = TOP_K * tm
    slot = i % 2

    @pl.when(i == 0)
    def _():
        for r in range(count):
            _row_copy(src_hbm, buf.at[0], sems.at[0], idx_ref[0, r], r).start()

    @pl.when(i + 1 < n)
    def _():
        for r in range(count):
            _row_copy(src_hbm, buf.at[1 - slot], sems.at[1 - slot], nxt_ref[0, r], r).start()

    for r in range(count):
        _row_copy(src_hbm, buf.at[slot], sems.at[slot], 0, r).wait()

    c = o_ref.shape[1] // 2
    acc_lo = jnp.zeros((tm, c), F32)
    acc_hi = jnp.zeros((tm, c), F32)
    for k in range(TOP_K):
        lo, hi = _unpack_bf16_pairs(buf[slot, k * tm:(k + 1) * tm, :])
        acc_lo = acc_lo + wt_ref[:, k:k + 1] * lo
        acc_hi = acc_hi + wt_ref[:, k:k + 1] * hi
    o_ref[:, :c] = acc_lo
    o_ref[:, c:] = acc_hi


def _combine(y_sorted, pos, weights, *, tm=128):
    n = pos.shape[0]
    w = y_sorted.shape[1]
    steps = n // tm
    idx3 = pos.reshape(steps, tm, TOP_K).transpose(0, 2, 1).reshape(steps, 1, TOP_K * tm)
    return pl.pallas_call(
        _combine_body,
        grid=(steps,),
        in_specs=[
            pl.BlockSpec((None, 1, TOP_K * tm), lambda i: (i, 0, 0), memory_space=pltpu.SMEM),
            pl.BlockSpec((None, 1, TOP_K * tm), lambda i: (jnp.minimum(i + 1, steps - 1), 0, 0),
                         memory_space=pltpu.SMEM),
            pl.BlockSpec((tm, TOP_K), lambda i: (i, 0)),
            pl.BlockSpec(memory_space=pl.ANY),
        ],
        out_specs=pl.BlockSpec((tm, 2 * w), lambda i: (i, 0)),
        out_shape=jax.ShapeDtypeStruct((n, 2 * w), F32),
        scratch_shapes=[pltpu.VMEM((2, TOP_K * tm, w), y_sorted.dtype), pltpu.SemaphoreType.DMA((2,))],
        compiler_params=_params("arbitrary"),
        name="combine",
    )(idx3, idx3, weights, y_sorted)


def _moe_mlp_body(ie_ref, ib_ref, nvb_ref, nu_ref, x_hbm, wg_ref, wl_ref, w2_ref, bg_ref, bl_ref, b2_ref, y_hbm,
                  acc_ref, xbuf, obuf, xsem, osem, par_ref):
    item = pl.program_id(0)
    chunk = pl.program_id(1)
    n_items = pl.num_programs(0)
    n_chunks = pl.num_programs(1)
    nvb = nvb_ref[item]
    blk0 = ib_ref[item]
    last_chunk = chunk == n_chunks - 1

    def x_copy(block, slot):
        rows = pl.ds(pl.multiple_of(block * EXPERT_BLOCK, EXPERT_BLOCK), EXPERT_BLOCK)
        return pltpu.make_async_copy(x_hbm.at[rows, :], xbuf.at[slot], xsem.at[slot])

    def y_copy(block, slot):
        rows = pl.ds(pl.multiple_of(block * EXPERT_BLOCK, EXPERT_BLOCK), EXPERT_BLOCK)
        return pltpu.make_async_copy(obuf.at[slot], y_hbm.at[rows, :], osem.at[slot])

    @pl.when((item == 0) & (chunk == 0))
    def _():
        par_ref[0] = 0
        x_copy(blk0, 0).start()

    @pl.when(nvb > 0)
    def _():
        wg = wg_ref[...].astype(BF16)
        wl = wl_ref[...].astype(BF16)
        w2 = w2_ref[...].astype(BF16)
        same_item = chunk + 1 < n_chunks
        nxt = jnp.minimum(jnp.where(same_item, item, item + 1), n_items - 1)
        has_next = same_item | ((item + 1 < n_items) & (nvb_ref[nxt] > 0))
        nxt_blk0 = ib_ref[nxt]

        def block_step(j, carry):
            slot = par_ref[0]
            x_copy(blk0 + j, slot).wait()
            more = j + 1 < nvb

            @pl.when(more)
            def _():
                x_copy(blk0 + j + 1, 1 - slot).start()

            @pl.when(jnp.logical_not(more) & has_next)
            def _():
                x_copy(nxt_blk0, 1 - slot).start()

            par_ref[0] = 1 - slot
            rows = pl.ds(pl.multiple_of(j * EXPERT_BLOCK, EXPERT_BLOCK), EXPERT_BLOCK)

            @pl.when(chunk == 0)
            def _():
                acc_ref[rows, :] = jnp.broadcast_to(b2_ref[...], (EXPERT_BLOCK, acc_ref.shape[1]))

            lo, hi = _unpack_bf16_pairs(xbuf[slot])
            xj = jnp.concatenate([lo.astype(BF16), hi.astype(BF16)], axis=1)
            glu = jnp.dot(xj, wg, preferred_element_type=F32) + bg_ref[...]
            lin = jnp.dot(xj, wl, preferred_element_type=F32) + bl_ref[...]
            glu = jnp.minimum(glu, SWIGLU_LIMIT)
            lin = jnp.clip(lin, -SWIGLU_LIMIT, SWIGLU_LIMIT)
            act = glu * jax.nn.sigmoid(SWIGLU_ALPHA * glu) * (lin + 1.0)
            acc_ref[rows, :] += jnp.dot(act.astype(BF16), w2, preferred_element_type=F32)

            @pl.when(last_chunk)
            def _():
                oslot = j % 2

                @pl.when(j >= 2)
                def _():
                    y_copy(blk0 + j - 2, oslot).wait()

                obuf[oslot] = _pack_bf16_pairs(acc_ref[rows, :])
                y_copy(blk0 + j, oslot).start()

            return carry

        lax.fori_loop(0, nvb, block_step, 0)

        @pl.when(last_chunk)
        def _():
            y_copy(blk0 + nvb - 1, (nvb - 1) % 2).wait()

            @pl.when(nvb >= 2)
            def _():
                y_copy(blk0 + nvb - 2, nvb % 2).wait()

    @pl.when((item == n_items - 1) & last_chunk)
    def _():
        obuf[0] = jnp.zeros(obuf.shape[1:], obuf.dtype)

        def zero_step(b, carry):
            y_copy(b, 0).start()
            y_copy(b, 0).wait()
            return carry

        lax.fori_loop(nu_ref[1], y_hbm.shape[0] // EXPERT_BLOCK, zero_step, 0)


def _moe_mlp(x_sorted, w1, b1, w2, b2, item_expert, item_blk0, item_nvb, n_used):
    n_rows, half = x_sorted.shape
    d = 2 * half
    n_items = item_expert.shape[0]
    f = w2.shape[1]
    nc = f // FF_CHUNK

    def live_chunk(i, c, nu):
        return jnp.where(i < nu[0], c, nc - 1)

    return pl.pallas_call(
        _moe_mlp_body,
        grid_spec=pltpu.PrefetchScalarGridSpec(
            num_scalar_prefetch=4,
            grid=(n_items, nc),
            in_specs=[
                pl.BlockSpec(memory_space=pl.ANY),
                pl.BlockSpec((None, d, FF_CHUNK), lambda i, c, ie, ib, nvb, nu: (ie[i], 0, live_chunk(i, c, nu))),
                pl.BlockSpec((None, d, FF_CHUNK), lambda i, c, ie, ib, nvb, nu: (ie[i], 0, nc + live_chunk(i, c, nu))),
                pl.BlockSpec((None, FF_CHUNK, d), lambda i, c, ie, ib, nvb, nu: (ie[i], live_chunk(i, c, nu), 0)),
                pl.BlockSpec((None, 1, FF_CHUNK), lambda i, c, ie, ib, nvb, nu: (ie[i], 0, live_chunk(i, c, nu))),
                pl.BlockSpec((None, 1, FF_CHUNK), lambda i, c, ie, ib, nvb, nu: (ie[i], 0, nc + live_chunk(i, c, nu))),
                pl.BlockSpec((None, 1, d), lambda i, c, ie, ib, nvb, nu: (ie[i], 0, 0)),
            ],
            out_specs=pl.BlockSpec(memory_space=pl.ANY),
            scratch_shapes=[
                pltpu.VMEM((ITEM_BLOCKS * EXPERT_BLOCK, d), F32),
                pltpu.VMEM((2, EXPERT_BLOCK, half), x_sorted.dtype),
                pltpu.VMEM((2, EXPERT_BLOCK, half), x_sorted.dtype),
                pltpu.SemaphoreType.DMA((2,)),
                pltpu.SemaphoreType.DMA((2,)),
                pltpu.SMEM((1,), jnp.int32),
            ],
        ),
        out_shape=jax.ShapeDtypeStruct((n_rows, half), x_sorted.dtype),
        compiler_params=pltpu.CompilerParams(dimension_semantics=("arbitrary", "arbitrary"),
                                             vmem_limit_bytes=MOE_VMEM_LIMIT_BYTES),
        name="moe_mlp",
    )(item_expert, item_blk0, item_nvb, n_used, x_sorted, w1, w1, w2, b1, b1, b2)


def _final_body(h_ref, moe_ref, p_ref, gp_ref, wg_ref, wp_ref, gf_ref, o_ref, *, final_norm):
    h = h_ref[...] + moe_ref[...]
    xn = (_rms_scale(h) * gp_ref[...]).astype(BF16)
    gate = jax.nn.sigmoid(jnp.dot(xn, wg_ref[...], preferred_element_type=F32))
    ple = jnp.dot(p_ref[...].astype(BF16), wp_ref[...], preferred_element_type=F32)
    h = h + gate * ple
    o_ref[...] = _rms_scale(h) * gf_ref[...] if final_norm else h


def _final(h1, moe, p2, g_ple, w_ple_gate, w_ple, g_final, *, final_norm, tm=512):
    n, d = h1.shape
    return pl.pallas_call(
        functools.partial(_final_body, final_norm=final_norm),
        grid=(n // tm,),
        in_specs=[
            pl.BlockSpec((tm, d), lambda i: (i, 0)),
            pl.BlockSpec((tm, d), lambda i: (i, 0)),
            pl.BlockSpec((tm, PLE_DIM), lambda i: (i, 0)),
            pl.BlockSpec((1, d), lambda i: (0, 0)),
            _resident((d, d), lambda i: (0, 0)),
            _resident((PLE_DIM, d), lambda i: (0, 0)),
            pl.BlockSpec((1, d), lambda i: (0, 0)),
        ],
        out_specs=pl.BlockSpec((tm, d), lambda i: (i, 0)),
        out_shape=jax.ShapeDtypeStruct((n, d), F32),
        compiler_params=_params("arbitrary"),
        name="final",
    )(h1, moe, p2, g_ple, w_ple_gate, w_ple, g_final)


def _routing_tables(expert_ids, n_tokens):
    nk = n_tokens * TOP_K
    e_flat = expert_ids.T.reshape(nk)
    onehot = (e_flat[:, None] == jnp.arange(N_EXPERTS, dtype=jnp.int32)[None, :]).astype(jnp.int32)
    csum = jnp.cumsum(onehot, axis=0)
    rank = jnp.sum(onehot * csum, axis=1) - 1
    counts = csum[-1]
    nblk = (counts + EXPERT_BLOCK - 1) // EXPERT_BLOCK
    blk_end = jnp.cumsum(nblk)
    blk_start = blk_end - nblk
    dest = blk_start[e_flat] * EXPERT_BLOCK + rank
    n_blocks = nk // EXPERT_BLOCK + N_EXPERTS
    nitem = (nblk + ITEM_BLOCKS - 1) // ITEM_BLOCKS
    item_end = jnp.cumsum(nitem)
    item_start = item_end - nitem
    n_items = N_EXPERTS + n_blocks // ITEM_BLOCKS
    idx = jnp.arange(n_items, dtype=jnp.int32)
    used = idx < item_end[-1]
    e_of = jnp.minimum(jnp.searchsorted(item_end, idx, side='right'), N_EXPERTS - 1).astype(jnp.int32)
    e_of = jnp.where(used, e_of, e_of[item_end[-1] - 1])
    local = (idx - item_start[e_of]) * ITEM_BLOCKS
    item_blk0 = jnp.where(used, blk_start[e_of] + local, 0).astype(jnp.int32)
    item_nvb = jnp.where(used, jnp.clip(nblk[e_of] - local, 0, ITEM_BLOCKS), 0).astype(jnp.int32)
    n_used = jnp.stack([item_end[-1], blk_end[-1]]).astype(jnp.int32)
    pos = dest.reshape(n_tokens, TOP_K).astype(jnp.int32)
    return pos, e_of, item_blk0, item_nvb, n_used, n_blocks * EXPERT_BLOCK


def _layer(h, p_l, norm_mix_g, w_in, b_in, conv_w, conv_b, w_rg_a, b_rg_a, w_rg_x, b_rg_x, lru_lambda,
           attn_sinks, w_attn_proj, w_lru_proj, w_out, norm_ffn_g, w_router, b_router, w_mlp1, b_mlp1,
           w_mlp2, b_mlp2, norm_ple_g, w_ple, w_ple_gate, norm_final_g, *, batch, seq, final_norm):
    n = h.shape[0]
    row = lambda v: v.reshape(1, -1)

    kv0 = Q_WIDTH
    rest0 = Q_WIDTH + 2 * KV_WIDTH
    reorder = lambda a: jnp.concatenate([a[..., :kv0], a[..., rest0:], a[..., kv0:rest0]], axis=-1)
    z = _inproj(h, row(norm_mix_g), reorder(w_in).astype(BF16), row(reorder(b_in)))

    kv_col = (Q_WIDTH + 2 * LRU_WIDTH + 2 * D_MODEL) // (2 * KV_WIDTH)
    y_attn = _attention(z, attn_sinks, batch=batch, seq=seq, q_col=0, kv_col=kv_col)

    w_rg = jnp.concatenate([w_rg_a, w_rg_x], axis=-1).astype(BF16)
    b_rg = jnp.concatenate([b_rg_a, b_rg_x], axis=-1).reshape(LRU_BLOCKS, 1, 2 * LRU_BLOCK_WIDTH)
    y_lru = _rglru(z, conv_w, row(conv_b), w_rg, b_rg, row(lru_lambda), batch=batch, seq=seq,
                   u_col=Q_WIDTH // LRU_BLOCK_WIDTH, gate_col=(Q_WIDTH + LRU_WIDTH) // LRU_BLOCK_WIDTH)

    merged = _merge(y_attn, y_lru, z, w_attn_proj.astype(BF16), w_lru_proj.astype(BF16),
                    ga_col=(Q_WIDTH + 2 * LRU_WIDTH) // D_MODEL, gl_col=(Q_WIDTH + 2 * LRU_WIDTH + D_MODEL) // D_MODEL)
    h1, xn2, expert_ids, expert_w = _outproj_router(
        h, merged, w_out.astype(BF16), row(norm_ffn_g), w_router.T.astype(BF16), b_router.reshape(N_EXPERTS, 1))

    pos, item_expert, item_blk0, item_nvb, n_used, n_sorted = _routing_tables(expert_ids, n)
    x_sorted = _dispatch(xn2, pos, n_sorted)
    y_sorted = _moe_mlp(x_sorted, w_mlp1, b_mlp1[:, None, :], w_mlp2, b_mlp2[:, None, :],
                        item_expert, item_blk0, item_nvb, n_used)
    moe = _combine(y_sorted, pos, expert_w.T)

    return _final(h1, moe, p_l, row(norm_ple_g), w_ple_gate.astype(BF16), w_ple.astype(BF16), row(norm_final_g),
                  final_norm=final_norm)


def kernel(x, p, norm_mix_g, w_in, b_in, conv_w, conv_b, w_rg_a, b_rg_a, w_rg_x, b_rg_x, lru_lambda, attn_sinks, w_attn_proj, w_lru_proj, w_out, norm_ffn_g, w_router, b_router, w_mlp1, b_mlp1, w_mlp2, b_mlp2, norm_ple_g, w_ple, w_ple_gate, norm_final_g):
    batch, seq, d = x.shape
    depth = p.shape[0]
    h = x.reshape(batch * seq, d)
    for l in range(depth):
        h = _layer(h, p[l].reshape(batch * seq, -1), norm_mix_g[l], w_in[l], b_in[l], conv_w[l], conv_b[l],
                   w_rg_a[l], b_rg_a[l], w_rg_x[l], b_rg_x[l], lru_lambda[l], attn_sinks[l], w_attn_proj[l],
                   w_lru_proj[l], w_out[l], norm_ffn_g[l], w_router[l], b_router[l], w_mlp1[l], b_mlp1[l],
                   w_mlp2[l], b_mlp2[l], norm_ple_g[l], w_ple[l], w_ple_gate[l], norm_final_g,
                   batch=batch, seq=seq, final_norm=(l == depth - 1))
    return h.reshape(batch, seq, d)
```

```python
import functools

import jax
import jax.numpy as jnp
from jax import lax
from jax.experimental import pallas as pl
from jax.experimental.pallas import tpu as pltpu

D_MODEL = 2048
N_Q_HEADS = 32
N_KV_HEADS = 4
HEAD_DIM = 64
Q_WIDTH = N_Q_HEADS * HEAD_DIM
KV_WIDTH = N_KV_HEADS * HEAD_DIM
WINDOW = 128
LRU_WIDTH = D_MODEL
LRU_BLOCKS = 8
LRU_BLOCK_WIDTH = LRU_WIDTH // LRU_BLOCKS
CONV_WIDTH = 4
LRU_C = 8.0
IN_WIDTH = Q_WIDTH + 2 * KV_WIDTH + 2 * LRU_WIDTH + 2 * D_MODEL
N_EXPERTS = 32
TOP_K = 4
D_EXPERT = D_MODEL
SWIGLU_LIMIT = 7.0
SWIGLU_ALPHA = 1.702
EXPERT_BLOCK = 256
EXPERT_ITEM = 768
FF_CHUNK = 512
PLE_DIM = 256
RMS_EPS = 1e-6

VMEM_LIMIT_BYTES = 56 * 1024 * 1024
MOE_VMEM_LIMIT_BYTES = 60 * 1024 * 1024
BF16_SUBLANES = 16

F32 = jnp.float32
BF16 = jnp.bfloat16


def _params(*sem):
    return pltpu.CompilerParams(dimension_semantics=sem, vmem_limit_bytes=VMEM_LIMIT_BYTES)


def _resident(shape, index_map):
    return pl.BlockSpec(shape, index_map, pipeline_mode=pl.Buffered(1))


def _rms_scale(x):
    var = jnp.mean(x * x, axis=-1, keepdims=True)
    return x * lax.rsqrt(var + RMS_EPS)


def _pack_bf16_pairs(x):
    c = x.shape[1] // 2
    xb = x.astype(BF16).astype(F32)
    lo = lax.bitcast_convert_type(xb[:, :c], jnp.uint32) >> 16
    hi = lax.bitcast_convert_type(xb[:, c:], jnp.uint32) & jnp.uint32(0xFFFF0000)
    return hi | lo


def _unpack_bf16_pairs(w):
    lo = lax.bitcast_convert_type(w << 16, F32)
    hi = lax.bitcast_convert_type(w & jnp.uint32(0xFFFF0000), F32)
    return lo, hi


def _inproj_body(x_ref, g_ref, w_ref, b_ref, o_ref, xn_ref):
    @pl.when(pl.program_id(1) == 0)
    def _():
        xn_ref[...] = (_rms_scale(x_ref[...]) * g_ref[...]).astype(BF16)

    acc = jnp.dot(xn_ref[...], w_ref[...], preferred_element_type=F32)
    o_ref[...] = (acc + b_ref[...]).astype(o_ref.dtype)


def _inproj(x2, g, w, b, *, tm=1024, tn=1536):
    n, d = x2.shape
    width = w.shape[1]
    return pl.pallas_call(
        _inproj_body,
        grid=(n // tm, width // tn),
        in_specs=[
            pl.BlockSpec((tm, d), lambda i, j: (i, 0)),
            pl.BlockSpec((1, d), lambda i, j: (0, 0)),
            pl.BlockSpec((d, tn), lambda i, j: (0, j)),
            pl.BlockSpec((1, tn), lambda i, j: (0, j)),
        ],
        out_specs=pl.BlockSpec((tm, tn), lambda i, j: (i, j)),
        out_shape=jax.ShapeDtypeStruct((n, width), BF16),
        scratch_shapes=[pltpu.VMEM((tm, d), BF16)],
        compiler_params=_params("arbitrary", "arbitrary"),
        name="inproj",
    )(x2, g, w, b)


def _block_diag_pair(pair, odd, lo_mask):
    swapped = jnp.concatenate([pair[:, HEAD_DIM:], pair[:, :HEAD_DIM]], axis=1)
    zero = jnp.zeros_like(pair)
    if odd:
        lo = jnp.where(lo_mask, swapped, zero)
        hi = jnp.where(lo_mask, zero, pair)
    else:
        lo = jnp.where(lo_mask, pair, zero)
        hi = jnp.where(lo_mask, zero, swapped)
    return jnp.concatenate([lo, hi], axis=0)


def _attn_body(sinks_ref, q_ref, kvp_ref, kvc_ref, o_ref, *, tq):
    nblk = tq // WINDOW
    group = N_Q_HEADS // N_KV_HEADS
    pairs = group // 2
    rows = pairs * WINDOW
    band_w = 2 * WINDOW
    s_idx = pl.program_id(1)

    lo_mask = lax.broadcasted_iota(jnp.int32, (1, 2 * HEAD_DIM), 1) < HEAD_DIM
    qi = lax.broadcasted_iota(jnp.int32, (rows, band_w), 0) % WINDOW
    kj = lax.broadcasted_iota(jnp.int32, (rows, band_w), 1)
    in_band = (kj > qi) & (kj <= qi + WINDOW)
    first_key = jnp.where(s_idx > 0, 0, WINDOW)
    row_pair = lax.broadcasted_iota(jnp.int32, (rows, 1), 0) // WINDOW
    neg = jnp.finfo(F32).min

    for i in range(nblk):
        if i == 0:
            band = jnp.concatenate([kvp_ref[...], kvc_ref[0:WINDOW, :]], axis=0)
            mask = in_band & (kj >= first_key)
        else:
            band = kvc_ref[(i - 1) * WINDOW:(i + 1) * WINDOW, :]
            mask = in_band
        for kh in range(N_KV_HEADS):
            col = (kh // 2) * 2 * HEAD_DIM
            kk = _block_diag_pair(band[:, col:col + 2 * HEAD_DIM], kh % 2 == 1, lo_mask)
            vv = _block_diag_pair(band[:, KV_WIDTH + col:KV_WIDTH + col + 2 * HEAD_DIM], kh % 2 == 1, lo_mask)
            qbase = kh * group * HEAD_DIM
            q4 = jnp.concatenate(
                [q_ref[i * WINDOW:(i + 1) * WINDOW, qbase + 2 * HEAD_DIM * j:qbase + 2 * HEAD_DIM * (j + 1)]
                 for j in range(pairs)], axis=0)
            s = lax.dot_general(q4, kk, (((1,), (1,)), ((), ())), preferred_element_type=F32)
            s = s * (HEAD_DIM ** -0.5)
            sink_lo = jnp.zeros((rows, 1), F32)
            sink_hi = jnp.zeros((rows, 1), F32)
            for j in range(pairs):
                sink_lo = jnp.where(row_pair == j, sinks_ref[kh * group + 2 * j], sink_lo)
                sink_hi = jnp.where(row_pair == j, sinks_ref[kh * group + 2 * j + 1], sink_hi)
            halves = []
            inv = []
            for half, sink in ((0, sink_lo), (1, sink_hi)):
                sh = jnp.where(mask, s[:, half * band_w:(half + 1) * band_w], neg)
                m = jnp.maximum(jnp.max(sh, axis=1, keepdims=True), sink)
                p = jnp.exp(sh - m)
                denom = jnp.sum(p, axis=1, keepdims=True) + jnp.exp(sink - m)
                halves.append(p.astype(BF16))
                inv.append(1.0 / denom)
            pcat = jnp.concatenate(halves, axis=1)
            o = jnp.dot(pcat, vv, preferred_element_type=F32)
            o = o * jnp.where(lo_mask, inv[0], inv[1])
            for j in range(pairs):
                o_ref[i * WINDOW:(i + 1) * WINDOW, qbase + 2 * HEAD_DIM * j:qbase + 2 * HEAD_DIM * (j + 1)] = (
                    o[j * WINDOW:(j + 1) * WINDOW, :].astype(o_ref.dtype))


def _attention(z, sinks, *, batch, seq, q_col, kv_col, tq=512):
    n = z.shape[0]
    spb = seq // tq
    wpb = seq // WINDOW
    kvw = 2 * KV_WIDTH

    def prev_map(b, s, sinks_ref):
        return (jnp.maximum(b * wpb + s * (tq // WINDOW) - 1, 0), kv_col)

    return pl.pallas_call(
        functools.partial(_attn_body, tq=tq),
        grid_spec=pltpu.PrefetchScalarGridSpec(
            num_scalar_prefetch=1,
            grid=(batch, spb),
            in_specs=[
                pl.BlockSpec((tq, Q_WIDTH), lambda b, s, sinks_ref: (b * spb + s, q_col)),
                pl.BlockSpec((WINDOW, kvw), prev_map),
                pl.BlockSpec((tq, kvw), lambda b, s, sinks_ref: (b * spb + s, kv_col)),
            ],
            out_specs=pl.BlockSpec((tq, Q_WIDTH), lambda b, s, sinks_ref: (b * spb + s, 0)),
        ),
        out_shape=jax.ShapeDtypeStruct((n, Q_WIDTH), BF16),
        compiler_params=_params("arbitrary", "arbitrary"),
        name="attn",
    )(sinks, z, z, z)


def _gelu_tanh(x):
    c = (2.0 / jnp.pi) ** 0.5
    return x * (0.5 * (1.0 + jnp.tanh(c * (x + 0.044715 * (x * x * x)))))


def _lru_body(u_ref, up_ref, gate_ref, cw_ref, cb_ref, wrg_ref, brg_ref, lam_ref, y_ref,
              ext_ref, a_ref, b_ref, h_ref):
    s_idx = pl.program_id(2)
    tl, w = u_ref.shape
    pad = 8

    @pl.when(s_idx == 0)
    def _():
        ext_ref[0:pad, :] = jnp.zeros((pad, w), F32)
        h_ref[...] = jnp.zeros_like(h_ref)

    @pl.when(s_idx > 0)
    def _():
        ext_ref[0:pad, :] = up_ref[...].astype(F32)[BF16_SUBLANES - pad:BF16_SUBLANES, :]

    ext_ref[pad:pad + tl, :] = u_ref[...].astype(F32)
    first = pad - (CONV_WIDTH - 1)
    uc = cb_ref[...] + ext_ref[first:first + tl, :] * cw_ref[0:1, :]
    for tap in range(1, CONV_WIDTH):
        uc = uc + ext_ref[first + tap:first + tap + tl, :] * cw_ref[tap:tap + 1, :]

    gates = jnp.dot(uc.astype(BF16), wrg_ref[...], preferred_element_type=F32) + brg_ref[...]
    r = jax.nn.sigmoid(gates[:, :w])
    ig = jax.nn.sigmoid(gates[:, w:])
    neg_lam = -lam_ref[...]
    softplus = jnp.maximum(neg_lam, 0.0) + jnp.log1p(jnp.exp(-jnp.abs(neg_lam)))
    log_a = (-LRU_C * r) * softplus
    a = jnp.exp(log_a)
    mult = jnp.sqrt(-jnp.tanh(log_a) * (a * a + 1.0))
    row = lax.broadcasted_iota(jnp.int32, (tl, 1), 0)
    mult = jnp.where((row == 0) & (s_idx == 0), 1.0, mult)
    b = mult * (ig * uc)

    sub = row % 8
    for d in (1, 2, 4):
        keep = sub >= d
        b_prev = jnp.where(keep, pltpu.roll(b, d, 0), 0.0)
        a_prev = jnp.where(keep, pltpu.roll(a, d, 0), 1.0)
        b = b + a * b_prev
        a = a * a_prev
    a_ref[...] = a
    b_ref[...] = b

    def group_step(gi, h):
        off = pl.multiple_of(gi * 8, 8)
        hg = b_ref[pl.ds(off, 8), :] + a_ref[pl.ds(off, 8), :] * h
        b_ref[pl.ds(off, 8), :] = hg
        return hg[7:8, :]

    h_ref[...] = lax.fori_loop(0, tl // 8, group_step, h_ref[...])
    y_ref[...] = (b_ref[...] * _gelu_tanh(gate_ref[...].astype(F32))).astype(y_ref.dtype)


def _rglru(z, conv_w, conv_b, w_rg, b_rg, lam, *, batch, seq, u_col, gate_col, tl=512):
    n = z.shape[0]
    w = LRU_BLOCK_WIDTH
    spb = seq // tl
    rpb = seq // BF16_SUBLANES

    def prev_map(b, c, s):
        return (jnp.maximum(b * rpb + s * (tl // BF16_SUBLANES) - 1, 0), u_col + c)

    return pl.pallas_call(
        _lru_body,
        grid=(batch, LRU_BLOCKS, spb),
        in_specs=[
            pl.BlockSpec((tl, w), lambda b, c, s: (b * spb + s, u_col + c)),
            pl.BlockSpec((BF16_SUBLANES, w), prev_map),
            pl.BlockSpec((tl, w), lambda b, c, s: (b * spb + s, gate_col + c)),
            pl.BlockSpec((CONV_WIDTH, w), lambda b, c, s: (0, c)),
            pl.BlockSpec((1, w), lambda b, c, s: (0, c)),
            pl.BlockSpec((None, w, 2 * w), lambda b, c, s: (c, 0, 0)),
            pl.BlockSpec((None, 1, 2 * w), lambda b, c, s: (c, 0, 0)),
            pl.BlockSpec((1, w), lambda b, c, s: (0, c)),
        ],
        out_specs=pl.BlockSpec((tl, w), lambda b, c, s: (b * spb + s, c)),
        out_shape=jax.ShapeDtypeStruct((n, LRU_WIDTH), BF16),
        scratch_shapes=[
            pltpu.VMEM((tl + 8, w), F32),
            pltpu.VMEM((tl, w), F32),
            pltpu.VMEM((tl, w), F32),
            pltpu.VMEM((1, w), F32),
        ],
        compiler_params=_params("arbitrary", "arbitrary", "arbitrary"),
        name="rglru",
    )(z, z, z, conv_w, conv_b, w_rg, b_rg, lam)


def _merge_body(ya_ref, yl_ref, ga_ref, gl_ref, pa_ref, pl_ref, o_ref):
    pa = jnp.dot(ya_ref[...], pa_ref[...], preferred_element_type=F32)
    pr = jnp.dot(yl_ref[...], pl_ref[...], preferred_element_type=F32)
    merged = jax.nn.sigmoid(ga_ref[...].astype(F32)) * pa + jax.nn.sigmoid(gl_ref[...].astype(F32)) * pr
    o_ref[...] = merged.astype(o_ref.dtype)


def _merge(y_attn, y_lru, z, w_attn_proj, w_lru_proj, *, ga_col, gl_col, tm=512):
    n, d = y_attn.shape
    return pl.pallas_call(
        _merge_body,
        grid=(n // tm,),
        in_specs=[
            pl.BlockSpec((tm, d), lambda i: (i, 0)),
            pl.BlockSpec((tm, d), lambda i: (i, 0)),
            pl.BlockSpec((tm, d), lambda i: (i, ga_col)),
            pl.BlockSpec((tm, d), lambda i: (i, gl_col)),
            _resident((d, d), lambda i: (0, 0)),
            _resident((d, d), lambda i: (0, 0)),
        ],
        out_specs=pl.BlockSpec((tm, d), lambda i: (i, 0)),
        out_shape=jax.ShapeDtypeStruct((n, d), BF16),
        compiler_params=_params("arbitrary"),
        name="merge",
    )(y_attn, y_lru, z, z, w_attn_proj, w_lru_proj)


def _outproj_body(x_ref, m_ref, wo_ref, g_ref, wr_ref, br_ref, h_ref, xn_ref, idx_ref, wt_ref):
    h = x_ref[...] + jnp.dot(m_ref[...], wo_ref[...], preferred_element_type=F32)
    h_ref[...] = h
    xn = _rms_scale(h) * g_ref[...]
    xn_ref[...] = _pack_bf16_pairs(xn)
    logits = lax.dot_general(wr_ref[...], xn.astype(BF16), (((1,), (1,)), ((), ())),
                             preferred_element_type=F32) + br_ref[...]
    expert = lax.broadcasted_iota(jnp.int32, logits.shape, 0).astype(F32)
    vals = []
    for k in range(TOP_K):
        m = jnp.max(logits, axis=0, keepdims=True)
        sel = jnp.min(jnp.where(logits == m, expert, float(N_EXPERTS)), axis=0, keepdims=True)
        vals.append(m)
        idx_ref[k:k + 1, :] = sel.astype(jnp.int32)
        logits = jnp.where(expert == sel, -jnp.inf, logits)
    exps = [jnp.exp(v - vals[0]) for v in vals]
    denom = exps[0] + exps[1] + exps[2] + exps[3]
    for k in range(TOP_K):
        wt_ref[k:k + 1, :] = exps[k] / denom


def _outproj_router(x2, merged, w_out, g_ffn, w_router_t, b_router, *, tm=512):
    n, d = x2.shape
    return pl.pallas_call(
        _outproj_body,
        grid=(n // tm,),
        in_specs=[
            pl.BlockSpec((tm, d), lambda i: (i, 0)),
            pl.BlockSpec((tm, d), lambda i: (i, 0)),
            _resident((d, d), lambda i: (0, 0)),
            pl.BlockSpec((1, d), lambda i: (0, 0)),
            pl.BlockSpec((N_EXPERTS, d), lambda i: (0, 0)),
            pl.BlockSpec((N_EXPERTS, 1), lambda i: (0, 0)),
        ],
        out_specs=[
            pl.BlockSpec((tm, d), lambda i: (i, 0)),
            pl.BlockSpec((tm, d // 2), lambda i: (i, 0)),
            pl.BlockSpec((TOP_K, tm), lambda i: (0, i)),
            pl.BlockSpec((TOP_K, tm), lambda i: (0, i)),
        ],
        out_shape=[
            jax.ShapeDtypeStruct((n, d), F32),
            jax.ShapeDtypeStruct((n, d // 2), jnp.uint32),
            jax.ShapeDtypeStruct((TOP_K, n), jnp.int32),
            jax.ShapeDtypeStruct((TOP_K, n), F32),
        ],
        compiler_params=_params("arbitrary"),
        name="outproj_router",
    )(x2, merged, w_out, g_ffn, w_router_t, b_router)


def _row_copy(src, dst, sem, src_row, dst_row):
    return pltpu.make_async_copy(src.at[pl.ds(src_row, 1), :], dst.at[pl.ds(dst_row, 1), :], sem)


def _dispatch_body(dst_ref, x_ref, init_hbm, out_hbm, stage, sems):
    del init_hbm
    i = pl.program_id(0)
    n = pl.num_programs(0)
    c = dst_ref.shape[1]
    slot = i % 2
    stage[slot] = x_ref[...]
    for r in range(c):
        _row_copy(stage.at[slot], out_hbm, sems.at[slot], r // TOP_K, dst_ref[0, r]).start()

    @pl.when(i > 0)
    def _():
        for r in range(c):
            _row_copy(stage.at[1 - slot], out_hbm, sems.at[1 - slot], 0, 0).wait()

    @pl.when(i == n - 1)
    def _():
        for r in range(c):
            _row_copy(stage.at[slot], out_hbm, sems.at[slot], 0, 0).wait()


def _dispatch(rows, pos, n_sorted, *, tm=128):
    n, w = rows.shape
    dst3 = pos.reshape(n // tm, 1, tm * TOP_K)
    init = jnp.zeros((n_sorted, w), rows.dtype)
    return pl.pallas_call(
        _dispatch_body,
        grid=(n // tm,),
        in_specs=[
            pl.BlockSpec((None, 1, tm * TOP_K), lambda i: (i, 0, 0), memory_space=pltpu.SMEM),
            pl.BlockSpec((tm, w), lambda i: (i, 0)),
            pl.BlockSpec(memory_space=pl.ANY),
        ],
        out_specs=pl.BlockSpec(memory_space=pl.ANY),
        out_shape=jax.ShapeDtypeStruct((n_sorted, w), rows.dtype),
        scratch_shapes=[pltpu.VMEM((2, tm, w), rows.dtype), pltpu.SemaphoreType.DMA((2,))],
        input_output_aliases={2: 0},
        compiler_params=_params("arbitrary"),
        name="dispatch",
    )(dst3, rows, init)


def _combine_body(idx_ref, nxt_ref, wt_ref, src_hbm, o_ref, buf, sems):
    i = pl.program_id(0)
    n = pl.num_programs(0)
    tm = o_ref.shape[0]
    count = TOP_K * tm
    slot = i % 2

    @pl.when(i == 0)
    def _():
        for r in range(count):
            _row_copy(src_hbm, buf.at[0], sems.at[0], idx_ref[0, r], r).start()

    @pl.when(i + 1 < n)
    def _():
        for r in range(count):
            _row_copy(src_hbm, buf.at[1 - slot], sems.at[1 - slot], nxt_ref[0, r], r).start()

    for r in range(count):
        _row_copy(src_hbm, buf.at[slot], sems.at[slot], 0, r).wait()

    c = o_ref.shape[1] // 2
    acc_lo = jnp.zeros((tm, c), F32)
    acc_hi = jnp.zeros((tm, c), F32)
    for k in range(TOP_K):
        lo, hi = _unpack_bf16_pairs(buf[slot, k * tm:(k + 1) * tm, :])
        acc_lo = acc_lo + wt_ref[:, k:k + 1] * lo
        acc_hi = acc_hi + wt_ref[:, k:k + 1] * hi
    o_ref[:, :c] = acc_lo
    o_ref[:, c:] = acc_hi


def _combine(y_sorted, pos, weights, *, tm=128):
    n = pos.shape[0]
    w = y_sorted.shape[1]
    steps = n // tm
    idx3 = pos.reshape(steps, tm, TOP_K).transpose(0, 2, 1).reshape(steps, 1, TOP_K * tm)
    return pl.pallas_call(
        _combine_body,
        grid=(steps,),
        in_specs=[
            pl.BlockSpec((None, 1, TOP_K * tm), lambda i: (i, 0, 0), memory_space=pltpu.SMEM),
            pl.BlockSpec((None, 1, TOP_K * tm), lambda i: (jnp.minimum(i + 1, steps - 1), 0, 0),
                         memory_space=pltpu.SMEM),
            pl.BlockSpec((tm, TOP_K), lambda i: (i, 0)),
            pl.BlockSpec(memory_space=pl.ANY),
        ],
        out_specs=pl.BlockSpec((tm, 2 * w), lambda i: (i, 0)),
        out_shape=jax.ShapeDtypeStruct((n, 2 * w), F32),
        scratch_shapes=[pltpu.VMEM((2, TOP_K * tm, w), y_sorted.dtype), pltpu.SemaphoreType.DMA((2,))],
        compiler_params=_params("arbitrary"),
        name="combine",
    )(idx3, idx3, weights, y_sorted)


def _moe_mlp_body(ie_ref, nvb_ref, nu_ref, x_ref, wg_ref, wl_ref, w2_ref, bg_ref, bl_ref, b2_ref, o_ref, acc_ref):
    item = pl.program_id(0)
    chunk = pl.program_id(1)
    nvb = nvb_ref[item]

    @pl.when(chunk == 0)
    def _():
        acc_ref[...] = jnp.broadcast_to(b2_ref[...], acc_ref.shape)

    @pl.when(nvb > 0)
    def _():
        wg = wg_ref[...].astype(BF16)
        wl = wl_ref[...].astype(BF16)
        w2 = w2_ref[...].astype(BF16)
        def run_rows(first_block, n_blocks):
            rows = slice(first_block * EXPERT_BLOCK, (first_block + n_blocks) * EXPERT_BLOCK)
            lo, hi = _unpack_bf16_pairs(x_ref[rows, :])
            xj = jnp.concatenate([lo.astype(BF16), hi.astype(BF16)], axis=1)
            glu = jnp.dot(xj, wg, preferred_element_type=F32) + bg_ref[...]
            lin = jnp.dot(xj, wl, preferred_element_type=F32) + bl_ref[...]
            glu = jnp.minimum(glu, SWIGLU_LIMIT)
            lin = jnp.clip(lin, -SWIGLU_LIMIT, SWIGLU_LIMIT)
            act = glu * jax.nn.sigmoid(SWIGLU_ALPHA * glu) * (lin + 1.0)
            acc_ref[rows, :] += jnp.dot(act.astype(BF16), w2, preferred_element_type=F32)

        n_sub = x_ref.shape[0] // EXPERT_BLOCK
        for j in range(0, n_sub, 2):
            if j + 2 <= n_sub:
                pl.when(nvb >= j + 2)(functools.partial(run_rows, j, 2))
                pl.when(nvb == j + 1)(functools.partial(run_rows, j, 1))
            else:
                pl.when(nvb >= j + 1)(functools.partial(run_rows, j, 1))

    @pl.when(chunk == pl.num_programs(1) - 1)
    def _():
        o_ref[...] = _pack_bf16_pairs(acc_ref[...])


def _moe_mlp(x_sorted, w1, b1, w2, b2, item_expert, item_nvb, n_used):
    n_rows, half = x_sorted.shape
    d = 2 * half
    n_items = n_rows // EXPERT_ITEM
    f = w2.shape[1]
    nc = f // FF_CHUNK

    def live_chunk(i, c, nu):
        return jnp.where(i < nu[0], c, nc - 1)

    def rows_map(i, c, ie, nvb, nu):
        return (jnp.minimum(i, nu[0] - 1), 0)

    return pl.pallas_call(
        _moe_mlp_body,
        grid_spec=pltpu.PrefetchScalarGridSpec(
            num_scalar_prefetch=3,
            grid=(n_items, nc),
            in_specs=[
                pl.BlockSpec((EXPERT_ITEM, half), rows_map),
                pl.BlockSpec((None, d, FF_CHUNK), lambda i, c, ie, nvb, nu: (ie[i], 0, live_chunk(i, c, nu))),
                pl.BlockSpec((None, d, FF_CHUNK), lambda i, c, ie, nvb, nu: (ie[i], 0, nc + live_chunk(i, c, nu))),
                pl.BlockSpec((None, FF_CHUNK, d), lambda i, c, ie, nvb, nu: (ie[i], live_chunk(i, c, nu), 0)),
                pl.BlockSpec((None, 1, FF_CHUNK), lambda i, c, ie, nvb, nu: (ie[i], 0, live_chunk(i, c, nu))),
                pl.BlockSpec((None, 1, FF_CHUNK), lambda i, c, ie, nvb, nu: (ie[i], 0, nc + live_chunk(i, c, nu))),
                pl.BlockSpec((None, 1, d), lambda i, c, ie, nvb, nu: (ie[i], 0, 0)),
            ],
            out_specs=pl.BlockSpec((EXPERT_ITEM, half), lambda i, c, ie, nvb, nu: (i, 0)),
            scratch_shapes=[pltpu.VMEM((EXPERT_ITEM, d), F32)],
        ),
        out_shape=jax.ShapeDtypeStruct((n_rows, half), jnp.uint32),
        compiler_params=pltpu.CompilerParams(dimension_semantics=("arbitrary", "arbitrary"),
                                             vmem_limit_bytes=MOE_VMEM_LIMIT_BYTES),
        name="moe_mlp",
    )(item_expert, item_nvb, n_used, x_sorted, w1, w1, w2, b1, b1, b2)


def _final_body(h_ref, moe_ref, p_ref, gp_ref, wg_ref, wp_ref, gf_ref, o_ref, *, final_norm):
    h = h_ref[...] + moe_ref[...]
    xn = (_rms_scale(h) * gp_ref[...]).astype(BF16)
    gate = jax.nn.sigmoid(jnp.dot(xn, wg_ref[...], preferred_element_type=F32))
    ple = jnp.dot(p_ref[...].astype(BF16), wp_ref[...], preferred_element_type=F32)
    h = h + gate * ple
    o_ref[...] = _rms_scale(h) * gf_ref[...] if final_norm else h


def _final(h1, moe, p2, g_ple, w_ple_gate, w_ple, g_final, *, final_norm, tm=512):
    n, d = h1.shape
    return pl.pallas_call(
        functools.partial(_final_body, final_norm=final_norm),
        grid=(n // tm,),
        in_specs=[
            pl.BlockSpec((tm, d), lambda i: (i, 0)),
            pl.BlockSpec((tm, d), lambda i: (i, 0)),
            pl.BlockSpec((tm, PLE_DIM), lambda i: (i, 0)),
            pl.BlockSpec((1, d), lambda i: (0, 0)),
            _resident((d, d), lambda i: (0, 0)),
            _resident((PLE_DIM, d), lambda i: (0, 0)),
            pl.BlockSpec((1, d), lambda i: (0, 0)),
        ],
        out_specs=pl.BlockSpec((tm, d), lambda i: (i, 0)),
        out_shape=jax.ShapeDtypeStruct((n, d), F32),
        compiler_params=_params("arbitrary"),
        name="final",
    )(h1, moe, p2, g_ple, w_ple_gate, w_ple, g_final)


def _routing_tables(expert_ids, n_tokens):
    nk = n_tokens * TOP_K
    sub = EXPERT_ITEM // EXPERT_BLOCK
    e_flat = expert_ids.T.reshape(nk)
    onehot = (e_flat[:, None] == jnp.arange(N_EXPERTS, dtype=jnp.int32)[None, :]).astype(jnp.int32)
    csum = jnp.cumsum(onehot, axis=0)
    rank = jnp.sum(onehot * csum, axis=1) - 1
    counts = csum[-1]
    padded = (counts + EXPERT_ITEM - 1) // EXPERT_ITEM * EXPERT_ITEM
    pend = jnp.cumsum(padded)
    pstart = pend - padded
    dest = pstart[e_flat] + rank
    n_items = nk // EXPERT_ITEM + N_EXPERTS
    n_used = (pend[-1] // EXPERT_ITEM).astype(jnp.int32)
    item_start = jnp.arange(n_items, dtype=jnp.int32) * EXPERT_ITEM
    item_expert = jnp.minimum(jnp.searchsorted(pend, item_start, side='right'), N_EXPERTS - 1).astype(jnp.int32)
    used = jnp.arange(n_items) < n_used
    item_expert = jnp.where(used, item_expert, item_expert[n_used - 1])
    rows_left = counts[item_expert] - (item_start - pstart[item_expert])
    item_nvb = jnp.where(used, jnp.clip((rows_left + EXPERT_BLOCK - 1) // EXPERT_BLOCK, 0, sub), 0).astype(jnp.int32)
    pos = dest.reshape(n_tokens, TOP_K).astype(jnp.int32)
    return pos, item_expert, item_nvb, n_used.reshape(1), n_items * EXPERT_ITEM


def _layer(h, p_l, norm_mix_g, w_in, b_in, conv_w, conv_b, w_rg_a, b_rg_a, w_rg_x, b_rg_x, lru_lambda,
           attn_sinks, w_attn_proj, w_lru_proj, w_out, norm_ffn_g, w_router, b_router, w_mlp1, b_mlp1,
           w_mlp2, b_mlp2, norm_ple_g, w_ple, w_ple_gate, norm_final_g, *, batch, seq, final_norm):
    n = h.shape[0]
    row = lambda v: v.reshape(1, -1)

    kv0 = Q_WIDTH
    rest0 = Q_WIDTH + 2 * KV_WIDTH
    reorder = lambda a: jnp.concatenate([a[..., :kv0], a[..., rest0:], a[..., kv0:rest0]], axis=-1)
    z = _inproj(h, row(norm_mix_g), reorder(w_in).astype(BF16), row(reorder(b_in)))

    kv_col = (Q_WIDTH + 2 * LRU_WIDTH + 2 * D_MODEL) // (2 * KV_WIDTH)
    y_attn = _attention(z, attn_sinks, batch=batch, seq=seq, q_col=0, kv_col=kv_col)

    w_rg = jnp.concatenate([w_rg_a, w_rg_x], axis=-1).astype(BF16)
    b_rg = jnp.concatenate([b_rg_a, b_rg_x], axis=-1).reshape(LRU_BLOCKS, 1, 2 * LRU_BLOCK_WIDTH)
    y_lru = _rglru(z, conv_w, row(conv_b), w_rg, b_rg, row(lru_lambda), batch=batch, seq=seq,
                   u_col=Q_WIDTH // LRU_BLOCK_WIDTH, gate_col=(Q_WIDTH + LRU_WIDTH) // LRU_BLOCK_WIDTH)

    merged = _merge(y_attn, y_lru, z, w_attn_proj.astype(BF16), w_lru_proj.astype(BF16),
                    ga_col=(Q_WIDTH + 2 * LRU_WIDTH) // D_MODEL, gl_col=(Q_WIDTH + 2 * LRU_WIDTH + D_MODEL) // D_MODEL)
    h1, xn2, expert_ids, expert_w = _outproj_router(
        h, merged, w_out.astype(BF16), row(norm_ffn_g), w_router.T.astype(BF16), b_router.reshape(N_EXPERTS, 1))

    pos, item_expert, item_nvb, n_used, n_sorted = _routing_tables(expert_ids, n)
    x_sorted = _dispatch(xn2, pos, n_sorted)
    y_sorted = _moe_mlp(x_sorted, w_mlp1, b_mlp1[:, None, :], w_mlp2, b_mlp2[:, None, :], item_expert, item_nvb, n_used)
    moe = _combine(y_sorted, pos, expert_w.T)

    return _final(h1, moe, p_l, row(norm_ple_g), w_ple_gate.astype(BF16), w_ple.astype(BF16), row(norm_final_g),
                  final_norm=final_norm)


def kernel(x, p, norm_mix_g, w_in, b_in, conv_w, conv_b, w_rg_a, b_rg_a, w_rg_x, b_rg_x, lru_lambda, attn_sinks, w_attn_proj, w_lru_proj, w_out, norm_ffn_g, w_router, b_router, w_mlp1, b_mlp1, w_mlp2, b_mlp2, norm_ple_g, w_ple, w_ple_gate, norm_final_g):
    batch, seq, d = x.shape
    depth = p.shape[0]
    h = x.reshape(batch * seq, d)
    for l in range(depth):
        h = _layer(h, p[l].reshape(batch * seq, -1), norm_mix_g[l], w_in[l], b_in[l], conv_w[l], conv_b[l],
                   w_rg_a[l], b_rg_a[l], w_rg_x[l], b_rg_x[l], lru_lambda[l], attn_sinks[l], w_attn_proj[l],
                   w_lru_proj[l], w_out[l], norm_ffn_g[l], w_router[l], b_router[l], w_mlp1[l], b_mlp1[l],
                   w_mlp2[l], b_mlp2[l], norm_ple_g[l], w_ple[l], w_ple_gate[l], norm_final_g,
                   batch=batch, seq=seq, final_norm=(l == depth - 1))
    return h.reshape(batch, seq, d)
```

```python
import functools

import jax
import jax.numpy as jnp
from jax import lax
from jax.experimental import pallas as pl
from jax.experimental.pallas import tpu as pltpu

D_MODEL = 2048
N_Q_HEADS = 32
N_KV_HEADS = 4
HEAD_DIM = 64
Q_WIDTH = N_Q_HEADS * HEAD_DIM
KV_WIDTH = N_KV_HEADS * HEAD_DIM
WINDOW = 128
LRU_WIDTH = D_MODEL
LRU_BLOCKS = 8
LRU_BLOCK_WIDTH = LRU_WIDTH // LRU_BLOCKS
CONV_WIDTH = 4
LRU_C = 8.0
IN_WIDTH = Q_WIDTH + 2 * KV_WIDTH + 2 * LRU_WIDTH + 2 * D_MODEL
N_EXPERTS = 32
TOP_K = 4
D_EXPERT = D_MODEL
SWIGLU_LIMIT = 7.0
SWIGLU_ALPHA = 1.702
EXPERT_BLOCK = 256
ITEM_BLOCKS = 9
FF_CHUNK = 512
PLE_DIM = 256
RMS_EPS = 1e-6

VMEM_LIMIT_BYTES = 56 * 1024 * 1024
MOE_VMEM_LIMIT_BYTES = 60 * 1024 * 1024
BF16_SUBLANES = 16

F32 = jnp.float32
BF16 = jnp.bfloat16


def _params(*sem):
    return pltpu.CompilerParams(dimension_semantics=sem, vmem_limit_bytes=VMEM_LIMIT_BYTES)


def _resident(shape, index_map):
    return pl.BlockSpec(shape, index_map, pipeline_mode=pl.Buffered(1))


def _rms_scale(x):
    var = jnp.mean(x * x, axis=-1, keepdims=True)
    return x * lax.rsqrt(var + RMS_EPS)


def _pack_bf16_pairs(x):
    c = x.shape[1] // 2
    xb = x.astype(BF16).astype(F32)
    lo = lax.bitcast_convert_type(xb[:, :c], jnp.uint32) >> 16
    hi = lax.bitcast_convert_type(xb[:, c:], jnp.uint32) & jnp.uint32(0xFFFF0000)
    return hi | lo


def _unpack_bf16_pairs(w):
    lo = lax.bitcast_convert_type(w << 16, F32)
    hi = lax.bitcast_convert_type(w & jnp.uint32(0xFFFF0000), F32)
    return lo, hi


def _inproj_body(x_ref, g_ref, w_ref, b_ref, o_ref, xn_ref):
    @pl.when(pl.program_id(1) == 0)
    def _():
        xn_ref[...] = (_rms_scale(x_ref[...]) * g_ref[...]).astype(BF16)

    acc = jnp.dot(xn_ref[...], w_ref[...], preferred_element_type=F32)
    o_ref[...] = (acc + b_ref[...]).astype(o_ref.dtype)


def _inproj(x2, g, w, b, *, tm=1024, tn=1536):
    n, d = x2.shape
    width = w.shape[1]
    return pl.pallas_call(
        _inproj_body,
        grid=(n // tm, width // tn),
        in_specs=[
            pl.BlockSpec((tm, d), lambda i, j: (i, 0)),
            pl.BlockSpec((1, d), lambda i, j: (0, 0)),
            pl.BlockSpec((d, tn), lambda i, j: (0, j)),
            pl.BlockSpec((1, tn), lambda i, j: (0, j)),
        ],
        out_specs=pl.BlockSpec((tm, tn), lambda i, j: (i, j)),
        out_shape=jax.ShapeDtypeStruct((n, width), BF16),
        scratch_shapes=[pltpu.VMEM((tm, d), BF16)],
        compiler_params=_params("arbitrary", "arbitrary"),
        name="inproj",
    )(x2, g, w, b)


def _block_diag_pair(pair, odd, lo_mask):
    swapped = jnp.concatenate([pair[:, HEAD_DIM:], pair[:, :HEAD_DIM]], axis=1)
    zero = jnp.zeros_like(pair)
    if odd:
        lo = jnp.where(lo_mask, swapped, zero)
        hi = jnp.where(lo_mask, zero, pair)
    else:
        lo = jnp.where(lo_mask, pair, zero)
        hi = jnp.where(lo_mask, zero, swapped)
    return jnp.concatenate([lo, hi], axis=0)


def _attn_body(sinks_ref, q_ref, kvp_ref, kvc_ref, o_ref, *, tq):
    nblk = tq // WINDOW
    group = N_Q_HEADS // N_KV_HEADS
    pairs = group // 2
    rows = pairs * WINDOW
    band_w = 2 * WINDOW
    s_idx = pl.program_id(1)

    lo_mask = lax.broadcasted_iota(jnp.int32, (1, 2 * HEAD_DIM), 1) < HEAD_DIM
    qi = lax.broadcasted_iota(jnp.int32, (rows, band_w), 0) % WINDOW
    kj = lax.broadcasted_iota(jnp.int32, (rows, band_w), 1)
    in_band = (kj > qi) & (kj <= qi + WINDOW)
    first_key = jnp.where(s_idx > 0, 0, WINDOW)
    row_pair = lax.broadcasted_iota(jnp.int32, (rows, 1), 0) // WINDOW
    neg = jnp.finfo(F32).min

    for i in range(nblk):
        if i == 0:
            band = jnp.concatenate([kvp_ref[...], kvc_ref[0:WINDOW, :]], axis=0)
            mask = in_band & (kj >= first_key)
        else:
            band = kvc_ref[(i - 1) * WINDOW:(i + 1) * WINDOW, :]
            mask = in_band
        for kh in range(N_KV_HEADS):
            col = (kh // 2) * 2 * HEAD_DIM
            kk = _block_diag_pair(band[:, col:col + 2 * HEAD_DIM], kh % 2 == 1, lo_mask)
            vv = _block_diag_pair(band[:, KV_WIDTH + col:KV_WIDTH + col + 2 * HEAD_DIM], kh % 2 == 1, lo_mask)
            qbase = kh * group * HEAD_DIM
            q4 = jnp.concatenate(
                [q_ref[i * WINDOW:(i + 1) * WINDOW, qbase + 2 * HEAD_DIM * j:qbase + 2 * HEAD_DIM * (j + 1)]
                 for j in range(pairs)], axis=0)
            s = lax.dot_general(q4, kk, (((1,), (1,)), ((), ())), preferred_element_type=F32)
            s = s * (HEAD_DIM ** -0.5)
            sink_lo = jnp.zeros((rows, 1), F32)
            sink_hi = jnp.zeros((rows, 1), F32)
            for j in range(pairs):
                sink_lo = jnp.where(row_pair == j, sinks_ref[kh * group + 2 * j], sink_lo)
                sink_hi = jnp.where(row_pair == j, sinks_ref[kh * group + 2 * j + 1], sink_hi)
            halves = []
            inv = []
            for half, sink in ((0, sink_lo), (1, sink_hi)):
                sh = jnp.where(mask, s[:, half * band_w:(half + 1) * band_w], neg)
                m = jnp.maximum(jnp.max(sh, axis=1, keepdims=True), sink)
                p = jnp.exp(sh - m)
                denom = jnp.sum(p, axis=1, keepdims=True) + jnp.exp(sink - m)
                halves.append(p.astype(BF16))
                inv.append(1.0 / denom)
            pcat = jnp.concatenate(halves, axis=1)
            o = jnp.dot(pcat, vv, preferred_element_type=F32)
            o = o * jnp.where(lo_mask, inv[0], inv[1])
            for j in range(pairs):
                o_ref[i * WINDOW:(i + 1) * WINDOW, qbase + 2 * HEAD_DIM * j:qbase + 2 * HEAD_DIM * (j + 1)] = (
                    o[j * WINDOW:(j + 1) * WINDOW, :].astype(o_ref.dtype))


def _attention(z, sinks, *, batch, seq, q_col, kv_col, tq=512):
    n = z.shape[0]
    spb = seq // tq
    wpb = seq // WINDOW
    kvw = 2 * KV_WIDTH

    def prev_map(b, s, sinks_ref):
        return (jnp.maximum(b * wpb + s * (tq // WINDOW) - 1, 0), kv_col)

    return pl.pallas_call(
        functools.partial(_attn_body, tq=tq),
        grid_spec=pltpu.PrefetchScalarGridSpec(
            num_scalar_prefetch=1,
            grid=(batch, spb),
            in_specs=[
                pl.BlockSpec((tq, Q_WIDTH), lambda b, s, sinks_ref: (b * spb + s, q_col)),
                pl.BlockSpec((WINDOW, kvw), prev_map),
                pl.BlockSpec((tq, kvw), lambda b, s, sinks_ref: (b * spb + s, kv_col)),
            ],
            out_specs=pl.BlockSpec((tq, Q_WIDTH), lambda b, s, sinks_ref: (b * spb + s, 0)),
        ),
        out_shape=jax.ShapeDtypeStruct((n, Q_WIDTH), BF16),
        compiler_params=_params("arbitrary", "arbitrary"),
        name="attn",
    )(sinks, z, z, z)


def _gelu_tanh(x):
    c = (2.0 / jnp.pi) ** 0.5
    return x * (0.5 * (1.0 + jnp.tanh(c * (x + 0.044715 * (x * x * x)))))


def _lru_body(u_ref, up_ref, gate_ref, cw_ref, cb_ref, wrg_ref, brg_ref, lam_ref, y_ref,
              ext_ref, a_ref, b_ref, h_ref):
    s_idx = pl.program_id(2)
    tl, w = u_ref.shape
    pad = 8

    @pl.when(s_idx == 0)
    def _():
        ext_ref[0:pad, :] = jnp.zeros((pad, w), F32)
        h_ref[...] = jnp.zeros_like(h_ref)

    @pl.when(s_idx > 0)
    def _():
        ext_ref[0:pad, :] = up_ref[...].astype(F32)[BF16_SUBLANES - pad:BF16_SUBLANES, :]

    ext_ref[pad:pad + tl, :] = u_ref[...].astype(F32)
    first = pad - (CONV_WIDTH - 1)
    uc = cb_ref[...] + ext_ref[first:first + tl, :] * cw_ref[0:1, :]
    for tap in range(1, CONV_WIDTH):
        uc = uc + ext_ref[first + tap:first + tap + tl, :] * cw_ref[tap:tap + 1, :]

    gates = jnp.dot(uc.astype(BF16), wrg_ref[...], preferred_element_type=F32) + brg_ref[...]
    r = jax.nn.sigmoid(gates[:, :w])
    ig = jax.nn.sigmoid(gates[:, w:])
    neg_lam = -lam_ref[...]
    softplus = jnp.maximum(neg_lam, 0.0) + jnp.log1p(jnp.exp(-jnp.abs(neg_lam)))
    log_a = (-LRU_C * r) * softplus
    a = jnp.exp(log_a)
    mult = jnp.sqrt(-jnp.tanh(log_a) * (a * a + 1.0))
    row = lax.broadcasted_iota(jnp.int32, (tl, 1), 0)
    mult = jnp.where((row == 0) & (s_idx == 0), 1.0, mult)
    b = mult * (ig * uc)

    sub = row % 8
    for d in (1, 2, 4):
        keep = sub >= d
        b_prev = jnp.where(keep, pltpu.roll(b, d, 0), 0.0)
        a_prev = jnp.where(keep, pltpu.roll(a, d, 0), 1.0)
        b = b + a * b_prev
        a = a * a_prev
    a_ref[...] = a
    b_ref[...] = b

    def group_step(gi, h):
        off = pl.multiple_of(gi * 8, 8)
        hg = b_ref[pl.ds(off, 8), :] + a_ref[pl.ds(off, 8), :] * h
        b_ref[pl.ds(off, 8), :] = hg
        return hg[7:8, :]

    h_ref[...] = lax.fori_loop(0, tl // 8, group_step, h_ref[...])
    y_ref[...] = (b_ref[...] * _gelu_tanh(gate_ref[...].astype(F32))).astype(y_ref.dtype)


def _rglru(z, conv_w, conv_b, w_rg, b_rg, lam, *, batch, seq, u_col, gate_col, tl=512):
    n = z.shape[0]
    w = LRU_BLOCK_WIDTH
    spb = seq // tl
    rpb = seq // BF16_SUBLANES

    def prev_map(b, c, s):
        return (jnp.maximum(b * rpb + s * (tl // BF16_SUBLANES) - 1, 0), u_col + c)

    return pl.pallas_call(
        _lru_body,
        grid=(batch, LRU_BLOCKS, spb),
        in_specs=[
            pl.BlockSpec((tl, w), lambda b, c, s: (b * spb + s, u_col + c)),
            pl.BlockSpec((BF16_SUBLANES, w), prev_map),
            pl.BlockSpec((tl, w), lambda b, c, s: (b * spb + s, gate_col + c)),
            pl.BlockSpec((CONV_WIDTH, w), lambda b, c, s: (0, c)),
            pl.BlockSpec((1, w), lambda b, c, s: (0, c)),
            pl.BlockSpec((None, w, 2 * w), lambda b, c, s: (c, 0, 0)),
            pl.BlockSpec((None, 1, 2 * w), lambda b, c, s: (c, 0, 0)),
            pl.BlockSpec((1, w), lambda b, c, s: (0, c)),
        ],
        out_specs=pl.BlockSpec((tl, w), lambda b, c, s: (b * spb + s, c)),
        out_shape=jax.ShapeDtypeStruct((n, LRU_WIDTH), BF16),
        scratch_shapes=[
            pltpu.VMEM((tl + 8, w), F32),
            pltpu.VMEM((tl, w), F32),
            pltpu.VMEM((tl, w), F32),
            pltpu.VMEM((1, w), F32),
        ],
        compiler_params=_params("arbitrary", "arbitrary", "arbitrary"),
        name="rglru",
    )(z, z, z, conv_w, conv_b, w_rg, b_rg, lam)


def _merge_body(ya_ref, yl_ref, ga_ref, gl_ref, pa_ref, pl_ref, o_ref):
    pa = jnp.dot(ya_ref[...], pa_ref[...], preferred_element_type=F32)
    pr = jnp.dot(yl_ref[...], pl_ref[...], preferred_element_type=F32)
    merged = jax.nn.sigmoid(ga_ref[...].astype(F32)) * pa + jax.nn.sigmoid(gl_ref[...].astype(F32)) * pr
    o_ref[...] = merged.astype(o_ref.dtype)


def _merge(y_attn, y_lru, z, w_attn_proj, w_lru_proj, *, ga_col, gl_col, tm=512):
    n, d = y_attn.shape
    return pl.pallas_call(
        _merge_body,
        grid=(n // tm,),
        in_specs=[
            pl.BlockSpec((tm, d), lambda i: (i, 0)),
            pl.BlockSpec((tm, d), lambda i: (i, 0)),
            pl.BlockSpec((tm, d), lambda i: (i, ga_col)),
            pl.BlockSpec((tm, d), lambda i: (i, gl_col)),
            _resident((d, d), lambda i: (0, 0)),
            _resident((d, d), lambda i: (0, 0)),
        ],
        out_specs=pl.BlockSpec((tm, d), lambda i: (i, 0)),
        out_shape=jax.ShapeDtypeStruct((n, d), BF16),
        compiler_params=_params("arbitrary"),
        name="merge",
    )(y_attn, y_lru, z, z, w_attn_proj, w_lru_proj)


def _outproj_body(x_ref, m_ref, wo_ref, g_ref, wr_ref, br_ref, h_ref, xn_ref, idx_ref, wt_ref):
    h = x_ref[...] + jnp.dot(m_ref[...], wo_ref[...], preferred_element_type=F32)
    h_ref[...] = h
    xn = _rms_scale(h) * g_ref[...]
    xn_ref[...] = _pack_bf16_pairs(xn)
    logits = lax.dot_general(wr_ref[...], xn.astype(BF16), (((1,), (1,)), ((), ())),
                             preferred_element_type=F32) + br_ref[...]
    expert = lax.broadcasted_iota(jnp.int32, logits.shape, 0).astype(F32)
    vals = []
    for k in range(TOP_K):
        m = jnp.max(logits, axis=0, keepdims=True)
        sel = jnp.min(jnp.where(logits == m, expert, float(N_EXPERTS)), axis=0, keepdims=True)
        vals.append(m)
        idx_ref[k:k + 1, :] = sel.astype(jnp.int32)
        logits = jnp.where(expert == sel, -jnp.inf, logits)
    exps = [jnp.exp(v - vals[0]) for v in vals]
    denom = exps[0] + exps[1] + exps[2] + exps[3]
    for k in range(TOP_K):
        wt_ref[k:k + 1, :] = exps[k] / denom


def _outproj_router(x2, merged, w_out, g_ffn, w_router_t, b_router, *, tm=512):
    n, d = x2.shape
    return pl.pallas_call(
        _outproj_body,
        grid=(n // tm,),
        in_specs=[
            pl.BlockSpec((tm, d), lambda i: (i, 0)),
            pl.BlockSpec((tm, d), lambda i: (i, 0)),
            _resident((d, d), lambda i: (0, 0)),
            pl.BlockSpec((1, d), lambda i: (0, 0)),
            pl.BlockSpec((N_EXPERTS, d), lambda i: (0, 0)),
            pl.BlockSpec((N_EXPERTS, 1), lambda i: (0, 0)),
        ],
        out_specs=[
            pl.BlockSpec((tm, d), lambda i: (i, 0)),
            pl.BlockSpec((tm, d // 2), lambda i: (i, 0)),
            pl.BlockSpec((TOP_K, tm), lambda i: (0, i)),
            pl.BlockSpec((TOP_K, tm), lambda i: (0, i)),
        ],
        out_shape=[
            jax.ShapeDtypeStruct((n, d), F32),
            jax.ShapeDtypeStruct((n, d // 2), jnp.uint32),
            jax.ShapeDtypeStruct((TOP_K, n), jnp.int32),
            jax.ShapeDtypeStruct((TOP_K, n), F32),
        ],
        compiler_params=_params("arbitrary"),
        name="outproj_router",
    )(x2, merged, w_out, g_ffn, w_router_t, b_router)


def _row_copy(src, dst, sem, src_row, dst_row):
    return pltpu.make_async_copy(src.at[pl.ds(src_row, 1), :], dst.at[pl.ds(dst_row, 1), :], sem)


def _dispatch_body(dst_ref, x_ref, init_hbm, out_hbm, stage, sems):
    del init_hbm
    i = pl.program_id(0)
    n = pl.num_programs(0)
    c = dst_ref.shape[1]
    slot = i % 2
    stage[slot] = x_ref[...]
    for r in range(c):
        _row_copy(stage.at[slot], out_hbm, sems.at[slot], r // TOP_K, dst_ref[0, r]).start()

    @pl.when(i > 0)
    def _():
        for r in range(c):
            _row_copy(stage.at[1 - slot], out_hbm, sems.at[1 - slot], 0, 0).wait()

    @pl.when(i == n - 1)
    def _():
        for r in range(c):
            _row_copy(stage.at[slot], out_hbm, sems.at[slot], 0, 0).wait()


def _dispatch(rows, pos, n_sorted, *, tm=128):
    n, w = rows.shape
    dst3 = pos.reshape(n // tm, 1, tm * TOP_K)
    init = jnp.zeros((n_sorted, w), rows.dtype)
    return pl.pallas_call(
        _dispatch_body,
        grid=(n // tm,),
        in_specs=[
            pl.BlockSpec((None, 1, tm * TOP_K), lambda i: (i, 0, 0), memory_space=pltpu.SMEM),
            pl.BlockSpec((tm, w), lambda i: (i, 0)),
            pl.BlockSpec(memory_space=pl.ANY),
        ],
        out_specs=pl.BlockSpec(memory_space=pl.ANY),
        out_shape=jax.ShapeDtypeStruct((n_sorted, w), rows.dtype),
        scratch_shapes=[pltpu.VMEM((2, tm, w), rows.dtype), pltpu.SemaphoreType.DMA((2,))],
        input_output_aliases={2: 0},
        compiler_params=_params("arbitrary"),
        name="dispatch",
    )(dst3, rows, init)


def _combine_body(idx_ref, nxt_ref, wt_ref, src_hbm, o_ref, buf, sems):
    i = pl.program_id(0)
    n = pl.num_programs(0)
    tm = o_ref.shape[0]
    count = TOP_K * tm
    slot = i % 2

    @pl.when(i == 0)
    def _():
        for r in range(count):
            _row_copy(src_hbm, buf.at[0], sems.at[0], idx_ref[0, r], r).start()

    @pl.when(i + 1 < n)
    def _():
        for r in range(count):
            _row_copy(src_hbm, buf.at[1 - slot], sems.at[1 - slot], nxt_ref[0, r], r).start()

    for r in range(count):
        _row_copy(src_hbm, buf.at[slot], sems.at[slot], 0, r).wait()

    c = o_ref.shape[1] // 2
    acc_lo = jnp.zeros((tm, c), F32)
    acc_hi = jnp.zeros((tm, c), F32)
    for k in range(TOP_K):
        lo, hi = _unpack_bf16_pairs(buf[slot, k * tm:(k + 1) * tm, :])
        acc_lo = acc_lo + wt_ref[:, k:k + 1] * lo
        acc_hi = acc_hi + wt_ref[:, k:k + 1] * hi
    o_ref[:, :c] = acc_lo
    o_ref[:, c:] = acc_hi


def _combine(y_sorted, pos, weights, *, tm=128):
    n = pos.shape[0]
    w = y_sorted.shape[1]
    steps = n // tm
    idx3 = pos.reshape(steps, tm, TOP_K).transpose(0, 2, 1).reshape(steps, 1, TOP_K * tm)
    return pl.pallas_call(
        _combine_body,
        grid=(steps,),
        in_specs=[
            pl.BlockSpec((None, 1, TOP_K * tm), lambda i: (i, 0, 0), memory_space=pltpu.SMEM),
            pl.BlockSpec((None, 1, TOP_K * tm), lambda i: (jnp.minimum(i + 1, steps - 1), 0, 0),
                         memory_space=pltpu.SMEM),
            pl.BlockSpec((tm, TOP_K), lambda i: (i, 0)),
            pl.BlockSpec(memory_space=pl.ANY),
        ],
        out_specs=pl.BlockSpec((tm, 2 * w), lambda i: (i, 0)),
        out_shape=jax.ShapeDtypeStruct((n, 2 * w), F32),
        scratch_shapes=[pltpu.VMEM((2, TOP_K * tm, w), y_sorted.dtype), pltpu.SemaphoreType.DMA((2,))],
        compiler_params=_params("arbitrary"),
        name="combine",
    )(idx3, idx3, weights, y_sorted)


def _moe_mlp_body(ie_ref, ib_ref, nvb_ref, nu_ref, x_hbm, wg_ref, wl_ref, w2_ref, bg_ref, bl_ref, b2_ref, y_hbm,
                  acc_ref, xbuf, obuf, xsem, osem, par_ref):
    item = pl.program_id(0)
    chunk = pl.program_id(1)
    n_items = pl.num_programs(0)
    n_chunks = pl.num_programs(1)
    nvb = nvb_ref[item]
    blk0 = ib_ref[item]
    last_chunk = chunk == n_chunks - 1

    def x_copy(block, slot):
        rows = pl.ds(pl.multiple_of(block * EXPERT_BLOCK, EXPERT_BLOCK), EXPERT_BLOCK)
        return pltpu.make_async_copy(x_hbm.at[rows, :], xbuf.at[slot], xsem.at[slot])

    def y_copy(block, slot):
        rows = pl.ds(pl.multiple_of(block * EXPERT_BLOCK, EXPERT_BLOCK), EXPERT_BLOCK)
        return pltpu.make_async_copy(obuf.at[slot], y_hbm.at[rows, :], osem.at[slot])

    @pl.when((item == 0) & (chunk == 0))
    def _():
        par_ref[0] = 0
        x_copy(blk0, 0).start(priority=1)

    @pl.when(nvb > 0)
    def _():
        wg = wg_ref[...].astype(BF16)
        wl = wl_ref[...].astype(BF16)
        w2 = w2_ref[...].astype(BF16)
        same_item = chunk + 1 < n_chunks
        nxt = jnp.minimum(jnp.where(same_item, item, item + 1), n_items - 1)
        has_next = same_item | ((item + 1 < n_items) & (nvb_ref[nxt] > 0))
        nxt_blk0 = ib_ref[nxt]

        def block_step(j, carry):
            slot = par_ref[0]
            x_copy(blk0 + j, slot).wait()
            more = j + 1 < nvb

            @pl.when(more)
            def _():
                x_copy(blk0 + j + 1, 1 - slot).start(priority=1)

            @pl.when(jnp.logical_not(more) & has_next)
            def _():
                x_copy(nxt_blk0, 1 - slot).start(priority=1)

            par_ref[0] = 1 - slot
            rows = pl.ds(pl.multiple_of(j * EXPERT_BLOCK, EXPERT_BLOCK), EXPERT_BLOCK)

            @pl.when(chunk == 0)
            def _():
                acc_ref[rows, :] = jnp.broadcast_to(b2_ref[...], (EXPERT_BLOCK, acc_ref.shape[1]))

            lo, hi = _unpack_bf16_pairs(xbuf[slot])
            xj = jnp.concatenate([lo.astype(BF16), hi.astype(BF16)], axis=1)
            glu = jnp.dot(xj, wg, preferred_element_type=F32) + bg_ref[...]
            lin = jnp.dot(xj, wl, preferred_element_type=F32) + bl_ref[...]
            glu = jnp.minimum(glu, SWIGLU_LIMIT)
            lin = jnp.clip(lin, -SWIGLU_LIMIT, SWIGLU_LIMIT)
            act = glu * jax.nn.sigmoid(SWIGLU_ALPHA * glu) * (lin + 1.0)
            acc_ref[rows, :] += jnp.dot(act.astype(BF16), w2, preferred_element_type=F32)

            @pl.when(last_chunk)
            def _():
                oslot = j % 2

                @pl.when(j >= 2)
                def _():
                    y_copy(blk0 + j - 2, oslot).wait()

                obuf[oslot] = _pack_bf16_pairs(acc_ref[rows, :])
                y_copy(blk0 + j, oslot).start(priority=1)

            return carry

        lax.fori_loop(0, nvb, block_step, 0)

        @pl.when(last_chunk)
        def _():
            y_copy(blk0 + nvb - 1, (nvb - 1) % 2).wait()

            @pl.when(nvb >= 2)
            def _():
                y_copy(blk0 + nvb - 2, nvb % 2).wait()

    @pl.when((item == n_items - 1) & last_chunk)
    def _():
        obuf[0] = jnp.zeros(obuf.shape[1:], obuf.dtype)

        def zero_step(b, carry):
            y_copy(b, 0).start(priority=1)
            y_copy(b, 0).wait()
            return carry

        lax.fori_loop(nu_ref[1], y_hbm.shape[0] // EXPERT_BLOCK, zero_step, 0)


def _moe_mlp(x_sorted, w1, b1, w2, b2, item_expert, item_blk0, item_nvb, n_used):
    n_rows, half = x_sorted.shape
    d = 2 * half
    n_items = item_expert.shape[0]
    f = w2.shape[1]
    nc = f // FF_CHUNK

    def live_chunk(i, c, nu):
        return jnp.where(i < nu[0], c, nc - 1)

    return pl.pallas_call(
        _moe_mlp_body,
        grid_spec=pltpu.PrefetchScalarGridSpec(
            num_scalar_prefetch=4,
            grid=(n_items, nc),
            in_specs=[
                pl.BlockSpec(memory_space=pl.ANY),
                pl.BlockSpec((None, d, FF_CHUNK), lambda i, c, ie, ib, nvb, nu: (ie[i], 0, live_chunk(i, c, nu))),
                pl.BlockSpec((None, d, FF_CHUNK), lambda i, c, ie, ib, nvb, nu: (ie[i], 0, nc + live_chunk(i, c, nu))),
                pl.BlockSpec((None, FF_CHUNK, d), lambda i, c, ie, ib, nvb, nu: (ie[i], live_chunk(i, c, nu), 0)),
                pl.BlockSpec((None, 1, FF_CHUNK), lambda i, c, ie, ib, nvb, nu: (ie[i], 0, live_chunk(i, c, nu))),
                pl.BlockSpec((None, 1, FF_CHUNK), lambda i, c, ie, ib, nvb, nu: (ie[i], 0, nc + live_chunk(i, c, nu))),
                pl.BlockSpec((None, 1, d), lambda i, c, ie, ib, nvb, nu: (ie[i], 0, 0)),
            ],
            out_specs=pl.BlockSpec(memory_space=pl.ANY),
            scratch_shapes=[
                pltpu.VMEM((ITEM_BLOCKS * EXPERT_BLOCK, d), F32),
                pltpu.VMEM((2, EXPERT_BLOCK, half), x_sorted.dtype),
                pltpu.VMEM((2, EXPERT_BLOCK, half), x_sorted.dtype),
                pltpu.SemaphoreType.DMA((2,)),
                pltpu.SemaphoreType.DMA((2,)),
                pltpu.SMEM((1,), jnp.int32),
            ],
        ),
        out_shape=jax.ShapeDtypeStruct((n_rows, half), x_sorted.dtype),
        compiler_params=pltpu.CompilerParams(dimension_semantics=("arbitrary", "arbitrary"),
                                             vmem_limit_bytes=MOE_VMEM_LIMIT_BYTES),
        name="moe_mlp",
    )(item_expert, item_blk0, item_nvb, n_used, x_sorted, w1, w1, w2, b1, b1, b2)


def _final_body(h_ref, moe_ref, p_ref, gp_ref, wg_ref, wp_ref, gf_ref, o_ref, *, final_norm):
    h = h_ref[...] + moe_ref[...]
    xn = (_rms_scale(h) * gp_ref[...]).astype(BF16)
    gate = jax.nn.sigmoid(jnp.dot(xn, wg_ref[...], preferred_element_type=F32))
    ple = jnp.dot(p_ref[...].astype(BF16), wp_ref[...], preferred_element_type=F32)
    h = h + gate * ple
    o_ref[...] = _rms_scale(h) * gf_ref[...] if final_norm else h


def _final(h1, moe, p2, g_ple, w_ple_gate, w_ple, g_final, *, final_norm, tm=512):
    n, d = h1.shape
    return pl.pallas_call(
        functools.partial(_final_body, final_norm=final_norm),
        grid=(n // tm,),
        in_specs=[
            pl.BlockSpec((tm, d), lambda i: (i, 0)),
            pl.BlockSpec((tm, d), lambda i: (i, 0)),
            pl.BlockSpec((tm, PLE_DIM), lambda i: (i, 0)),
            pl.BlockSpec((1, d), lambda i: (0, 0)),
            _resident((d, d), lambda i: (0, 0)),
            _resident((PLE_DIM, d), lambda i: (0, 0)),
            pl.BlockSpec((1, d), lambda i: (0, 0)),
        ],
        out_specs=pl.BlockSpec((tm, d), lambda i: (i, 0)),
        out_shape=jax.ShapeDtypeStruct((n, d), F32),
        compiler_params=_params("arbitrary"),
        name="final",
    )(h1, moe, p2, g_ple, w_ple_gate, w_ple, g_final)


def _routing_tables(expert_ids, n_tokens):
    nk = n_tokens * TOP_K
    e_flat = expert_ids.T.reshape(nk)
    onehot = (e_flat[:, None] == jnp.arange(N_EXPERTS, dtype=jnp.int32)[None, :]).astype(jnp.int32)
    csum = jnp.cumsum(onehot, axis=0)
    rank = jnp.sum(onehot * csum, axis=1) - 1
    counts = csum[-1]
    nblk = (counts + EXPERT_BLOCK - 1) // EXPERT_BLOCK
    blk_end = jnp.cumsum(nblk)
    blk_start = blk_end - nblk
    dest = blk_start[e_flat] * EXPERT_BLOCK + rank
    n_blocks = nk // EXPERT_BLOCK + N_EXPERTS
    nitem = (nblk + ITEM_BLOCKS - 1) // ITEM_BLOCKS
    item_end = jnp.cumsum(nitem)
    item_start = item_end - nitem
    n_items = N_EXPERTS + n_blocks // ITEM_BLOCKS
    idx = jnp.arange(n_items, dtype=jnp.int32)
    used = idx < item_end[-1]
    e_of = jnp.minimum(jnp.searchsorted(item_end, idx, side='right'), N_EXPERTS - 1).astype(jnp.int32)
    e_of = jnp.where(used, e_of, e_of[item_end[-1] - 1])
    local = (idx - item_start[e_of]) * ITEM_BLOCKS
    item_blk0 = jnp.where(used, blk_start[e_of] + local, 0).astype(jnp.int32)
    item_nvb = jnp.where(used, jnp.clip(nblk[e_of] - local, 0, ITEM_BLOCKS), 0).astype(jnp.int32)
    n_used = jnp.stack([item_end[-1], blk_end[-1]]).astype(jnp.int32)
    pos = dest.reshape(n_tokens, TOP_K).astype(jnp.int32)
    return pos, e_of, item_blk0, item_nvb, n_used, n_blocks * EXPERT_BLOCK


def _layer(h, p_l, norm_mix_g, w_in, b_in, conv_w, conv_b, w_rg_a, b_rg_a, w_rg_x, b_rg_x, lru_lambda,
           attn_sinks, w_attn_proj, w_lru_proj, w_out, norm_ffn_g, w_router, b_router, w_mlp1, b_mlp1,
           w_mlp2, b_mlp2, norm_ple_g, w_ple, w_ple_gate, norm_final_g, *, batch, seq, final_norm):
    n = h.shape[0]
    row = lambda v: v.reshape(1, -1)

    kv0 = Q_WIDTH
    rest0 = Q_WIDTH + 2 * KV_WIDTH
    reorder = lambda a: jnp.concatenate([a[..., :kv0], a[..., rest0:], a[..., kv0:rest0]], axis=-1)
    z = _inproj(h, row(norm_mix_g), reorder(w_in).astype(BF16), row(reorder(b_in)))

    kv_col = (Q_WIDTH + 2 * LRU_WIDTH + 2 * D_MODEL) // (2 * KV_WIDTH)
    y_attn = _attention(z, attn_sinks, batch=batch, seq=seq, q_col=0, kv_col=kv_col)

    w_rg = jnp.concatenate([w_rg_a, w_rg_x], axis=-1).astype(BF16)
    b_rg = jnp.concatenate([b_rg_a, b_rg_x], axis=-1).reshape(LRU_BLOCKS, 1, 2 * LRU_BLOCK_WIDTH)
    y_lru = _rglru(z, conv_w, row(conv_b), w_rg, b_rg, row(lru_lambda), batch=batch, seq=seq,
                   u_col=Q_WIDTH // LRU_BLOCK_WIDTH, gate_col=(Q_WIDTH + LRU_WIDTH) // LRU_BLOCK_WIDTH)

    merged = _merge(y_attn, y_lru, z, w_attn_proj.astype(BF16), w_lru_proj.astype(BF16),
                    ga_col=(Q_WIDTH + 2 * LRU_WIDTH) // D_MODEL, gl_col=(Q_WIDTH + 2 * LRU_WIDTH + D_MODEL) // D_MODEL)
    h1, xn2, expert_ids, expert_w = _outproj_router(
        h, merged, w_out.astype(BF16), row(norm_ffn_g), w_router.T.astype(BF16), b_router.reshape(N_EXPERTS, 1))

    pos, item_expert, item_blk0, item_nvb, n_used, n_sorted = _routing_tables(expert_ids, n)
    x_sorted = _dispatch(xn2, pos, n_sorted)
    y_sorted = _moe_mlp(x_sorted, w_mlp1, b_mlp1[:, None, :], w_mlp2, b_mlp2[:, None, :],
                        item_expert, item_blk0, item_nvb, n_used)
    moe = _combine(y_sorted, pos, expert_w.T)

    return _final(h1, moe, p_l, row(norm_ple_g), w_ple_gate.astype(BF16), w_ple.astype(BF16), row(norm_final_g),
                  final_norm=final_norm)


def kernel(x, p, norm_mix_g, w_in, b_in, conv_w, conv_b, w_rg_a, b_rg_a, w_rg_x, b_rg_x, lru_lambda, attn_sinks, w_attn_proj, w_lru_proj, w_out, norm_ffn_g, w_router, b_router, w_mlp1, b_mlp1, w_mlp2, b_mlp2, norm_ple_g, w_ple, w_ple_gate, norm_final_g):
    batch, seq, d = x.shape
    depth = p.shape[0]
    h = x.reshape(batch * seq, d)
    for l in range(depth):
        h = _layer(h, p[l].reshape(batch * seq, -1), norm_mix_g[l], w_in[l], b_in[l], conv_w[l], conv_b[l],
                   w_rg_a[l], b_rg_a[l], w_rg_x[l], b_rg_x[l], lru_lambda[l], attn_sinks[l], w_attn_proj[l],
                   w_lru_proj[l], w_out[l], norm_ffn_g[l], w_router[l], b_router[l], w_mlp1[l], b_mlp1[l],
                   w_mlp2[l], b_mlp2[l], norm_ple_g[l], w_ple[l], w_ple_gate[l], norm_final_g,
                   batch=batch, seq=seq, final_norm=(l == depth - 1))
    return h.reshape(batch, seq, d)
```

```python
import functools

import jax
import jax.numpy as jnp
from jax import lax
from jax.experimental import pallas as pl
from jax.experimental.pallas import tpu as pltpu

D_MODEL = 2048
N_Q_HEADS = 32
N_KV_HEADS = 4
HEAD_DIM = 64
Q_WIDTH = N_Q_HEADS * HEAD_DIM
KV_WIDTH = N_KV_HEADS * HEAD_DIM
WINDOW = 128
LRU_WIDTH = D_MODEL
LRU_BLOCKS = 8
LRU_BLOCK_WIDTH = LRU_WIDTH // LRU_BLOCKS
CONV_WIDTH = 4
LRU_C = 8.0
IN_WIDTH = Q_WIDTH + 2 * KV_WIDTH + 2 * LRU_WIDTH + 2 * D_MODEL
N_EXPERTS = 32
TOP_K = 4
D_EXPERT = D_MODEL
SWIGLU_LIMIT = 7.0
SWIGLU_ALPHA = 1.702
EXPERT_BLOCK = 256
EXPERT_ITEM = 1280
FF_CHUNK = 256
PLE_DIM = 256
RMS_EPS = 1e-6

VMEM_LIMIT_BYTES = 56 * 1024 * 1024
MOE_VMEM_LIMIT_BYTES = 60 * 1024 * 1024
BF16_SUBLANES = 16

F32 = jnp.float32
BF16 = jnp.bfloat16


def _params(*sem):
    return pltpu.CompilerParams(dimension_semantics=sem, vmem_limit_bytes=VMEM_LIMIT_BYTES)


def _resident(shape, index_map):
    return pl.BlockSpec(shape, index_map, pipeline_mode=pl.Buffered(1))


def _rms_scale(x):
    var = jnp.mean(x * x, axis=-1, keepdims=True)
    return x * lax.rsqrt(var + RMS_EPS)


def _pack_bf16_pairs(x):
    c = x.shape[1] // 2
    xb = x.astype(BF16).astype(F32)
    lo = lax.bitcast_convert_type(xb[:, :c], jnp.uint32) >> 16
    hi = lax.bitcast_convert_type(xb[:, c:], jnp.uint32) & jnp.uint32(0xFFFF0000)
    return hi | lo


def _unpack_bf16_pairs(w):
    lo = lax.bitcast_convert_type(w << 16, F32)
    hi = lax.bitcast_convert_type(w & jnp.uint32(0xFFFF0000), F32)
    return lo, hi


def _inproj_body(x_ref, g_ref, w_ref, b_ref, o_ref, xn_ref):
    @pl.when(pl.program_id(1) == 0)
    def _():
        xn_ref[...] = (_rms_scale(x_ref[...]) * g_ref[...]).astype(BF16)

    acc = jnp.dot(xn_ref[...], w_ref[...], preferred_element_type=F32)
    o_ref[...] = (acc + b_ref[...]).astype(o_ref.dtype)


def _inproj(x2, g, w, b, *, tm=1024, tn=1536):
    n, d = x2.shape
    width = w.shape[1]
    return pl.pallas_call(
        _inproj_body,
        grid=(n // tm, width // tn),
        in_specs=[
            pl.BlockSpec((tm, d), lambda i, j: (i, 0)),
            pl.BlockSpec((1, d), lambda i, j: (0, 0)),
            pl.BlockSpec((d, tn), lambda i, j: (0, j)),
            pl.BlockSpec((1, tn), lambda i, j: (0, j)),
        ],
        out_specs=pl.BlockSpec((tm, tn), lambda i, j: (i, j)),
        out_shape=jax.ShapeDtypeStruct((n, width), BF16),
        scratch_shapes=[pltpu.VMEM((tm, d), BF16)],
        compiler_params=_params("arbitrary", "arbitrary"),
        name="inproj",
    )(x2, g, w, b)


def _block_diag_pair(pair, odd, lo_mask):
    swapped = jnp.concatenate([pair[:, HEAD_DIM:], pair[:, :HEAD_DIM]], axis=1)
    zero = jnp.zeros_like(pair)
    if odd:
        lo = jnp.where(lo_mask, swapped, zero)
        hi = jnp.where(lo_mask, zero, pair)
    else:
        lo = jnp.where(lo_mask, pair, zero)
        hi = jnp.where(lo_mask, zero, swapped)
    return jnp.concatenate([lo, hi], axis=0)


def _attn_body(sinks_ref, q_ref, kvp_ref, kvc_ref, o_ref, *, tq):
    nblk = tq // WINDOW
    group = N_Q_HEADS // N_KV_HEADS
    pairs = group // 2
    rows = pairs * WINDOW
    band_w = 2 * WINDOW
    s_idx = pl.program_id(1)

    lo_mask = lax.broadcasted_iota(jnp.int32, (1, 2 * HEAD_DIM), 1) < HEAD_DIM
    qi = lax.broadcasted_iota(jnp.int32, (rows, band_w), 0) % WINDOW
    kj = lax.broadcasted_iota(jnp.int32, (rows, band_w), 1)
    in_band = (kj > qi) & (kj <= qi + WINDOW)
    first_key = jnp.where(s_idx > 0, 0, WINDOW)
    row_pair = lax.broadcasted_iota(jnp.int32, (rows, 1), 0) // WINDOW
    neg = jnp.finfo(F32).min

    for i in range(nblk):
        if i == 0:
            band = jnp.concatenate([kvp_ref[...], kvc_ref[0:WINDOW, :]], axis=0)
            mask = in_band & (kj >= first_key)
        else:
            band = kvc_ref[(i - 1) * WINDOW:(i + 1) * WINDOW, :]
            mask = in_band
        for kh in range(N_KV_HEADS):
            col = (kh // 2) * 2 * HEAD_DIM
            kk = _block_diag_pair(band[:, col:col + 2 * HEAD_DIM], kh % 2 == 1, lo_mask)
            vv = _block_diag_pair(band[:, KV_WIDTH + col:KV_WIDTH + col + 2 * HEAD_DIM], kh % 2 == 1, lo_mask)
            qbase = kh * group * HEAD_DIM
            q4 = jnp.concatenate(
                [q_ref[i * WINDOW:(i + 1) * WINDOW, qbase + 2 * HEAD_DIM * j:qbase + 2 * HEAD_DIM * (j + 1)]
                 for j in range(pairs)], axis=0)
            s = lax.dot_general(q4, kk, (((1,), (1,)), ((), ())), preferred_element_type=F32)
            s = s * (HEAD_DIM ** -0.5)
            sink_lo = jnp.zeros((rows, 1), F32)
            sink_hi = jnp.zeros((rows, 1), F32)
            for j in range(pairs):
                sink_lo = jnp.where(row_pair == j, sinks_ref[kh * group + 2 * j], sink_lo)
                sink_hi = jnp.where(row_pair == j, sinks_ref[kh * group + 2 * j + 1], sink_hi)
            halves = []
            inv = []
            for half, sink in ((0, sink_lo), (1, sink_hi)):
                sh = jnp.where(mask, s[:, half * band_w:(half + 1) * band_w], neg)
                m = jnp.maximum(jnp.max(sh, axis=1, keepdims=True), sink)
                p = jnp.exp(sh - m)
                denom = jnp.sum(p, axis=1, keepdims=True) + jnp.exp(sink - m)
                halves.append(p.astype(BF16))
                inv.append(1.0 / denom)
            pcat = jnp.concatenate(halves, axis=1)
            o = jnp.dot(pcat, vv, preferred_element_type=F32)
            o = o * jnp.where(lo_mask, inv[0], inv[1])
            for j in range(pairs):
                o_ref[i * WINDOW:(i + 1) * WINDOW, qbase + 2 * HEAD_DIM * j:qbase + 2 * HEAD_DIM * (j + 1)] = (
                    o[j * WINDOW:(j + 1) * WINDOW, :].astype(o_ref.dtype))


def _attention(z, sinks, *, batch, seq, q_col, kv_col, tq=512):
    n = z.shape[0]
    spb = seq // tq
    wpb = seq // WINDOW
    kvw = 2 * KV_WIDTH

    def prev_map(b, s, sinks_ref):
        return (jnp.maximum(b * wpb + s * (tq // WINDOW) - 1, 0), kv_col)

    return pl.pallas_call(
        functools.partial(_attn_body, tq=tq),
        grid_spec=pltpu.PrefetchScalarGridSpec(
            num_scalar_prefetch=1,
            grid=(batch, spb),
            in_specs=[
                pl.BlockSpec((tq, Q_WIDTH), lambda b, s, sinks_ref: (b * spb + s, q_col)),
                pl.BlockSpec((WINDOW, kvw), prev_map),
                pl.BlockSpec((tq, kvw), lambda b, s, sinks_ref: (b * spb + s, kv_col)),
            ],
            out_specs=pl.BlockSpec((tq, Q_WIDTH), lambda b, s, sinks_ref: (b * spb + s, 0)),
        ),
        out_shape=jax.ShapeDtypeStruct((n, Q_WIDTH), BF16),
        compiler_params=_params("arbitrary", "arbitrary"),
        name="attn",
    )(sinks, z, z, z)


def _gelu_tanh(x):
    c = (2.0 / jnp.pi) ** 0.5
    return x * (0.5 * (1.0 + jnp.tanh(c * (x + 0.044715 * (x * x * x)))))


def _lru_body(u_ref, up_ref, gate_ref, cw_ref, cb_ref, wrg_ref, brg_ref, lam_ref, y_ref,
              ext_ref, a_ref, b_ref, h_ref):
    s_idx = pl.program_id(2)
    tl, w = u_ref.shape
    pad = 8

    @pl.when(s_idx == 0)
    def _():
        ext_ref[0:pad, :] = jnp.zeros((pad, w), F32)
        h_ref[...] = jnp.zeros_like(h_ref)

    @pl.when(s_idx > 0)
    def _():
        ext_ref[0:pad, :] = up_ref[...].astype(F32)[BF16_SUBLANES - pad:BF16_SUBLANES, :]

    ext_ref[pad:pad + tl, :] = u_ref[...].astype(F32)
    first = pad - (CONV_WIDTH - 1)
    uc = cb_ref[...] + ext_ref[first:first + tl, :] * cw_ref[0:1, :]
    for tap in range(1, CONV_WIDTH):
        uc = uc + ext_ref[first + tap:first + tap + tl, :] * cw_ref[tap:tap + 1, :]

    gates = jnp.dot(uc.astype(BF16), wrg_ref[...], preferred_element_type=F32) + brg_ref[...]
    r = jax.nn.sigmoid(gates[:, :w])
    ig = jax.nn.sigmoid(gates[:, w:])
    neg_lam = -lam_ref[...]
    softplus = jnp.maximum(neg_lam, 0.0) + jnp.log1p(jnp.exp(-jnp.abs(neg_lam)))
    log_a = (-LRU_C * r) * softplus
    a = jnp.exp(log_a)
    m2 = -jnp.tanh(log_a) * (a * a + 1.0)
    mult = jnp.where(m2 > 0.0, m2 * lax.rsqrt(m2), 0.0)
    row = lax.broadcasted_iota(jnp.int32, (tl, 1), 0)
    mult = jnp.where((row == 0) & (s_idx == 0), 1.0, mult)
    b = mult * (ig * uc)

    sub = row % 8
    for d in (1, 2, 4):
        keep = sub >= d
        b_prev = jnp.where(keep, pltpu.roll(b, d, 0), 0.0)
        a_prev = jnp.where(keep, pltpu.roll(a, d, 0), 1.0)
        b = b + a * b_prev
        a = a * a_prev
    a_ref[...] = a
    b_ref[...] = b

    def group_step(gi, h):
        off = pl.multiple_of(gi * 8, 8)
        hg = b_ref[pl.ds(off, 8), :] + a_ref[pl.ds(off, 8), :] * h
        b_ref[pl.ds(off, 8), :] = hg
        return hg[7:8, :]

    h_ref[...] = lax.fori_loop(0, tl // 8, group_step, h_ref[...])
    y_ref[...] = (b_ref[...] * _gelu_tanh(gate_ref[...].astype(F32))).astype(y_ref.dtype)


def _rglru(z, conv_w, conv_b, w_rg, b_rg, lam, *, batch, seq, u_col, gate_col, tl=512):
    n = z.shape[0]
    w = LRU_BLOCK_WIDTH
    spb = seq // tl
    rpb = seq // BF16_SUBLANES

    def prev_map(b, c, s):
        return (jnp.maximum(b * rpb + s * (tl // BF16_SUBLANES) - 1, 0), u_col + c)

    return pl.pallas_call(
        _lru_body,
        grid=(batch, LRU_BLOCKS, spb),
        in_specs=[
            pl.BlockSpec((tl, w), lambda b, c, s: (b * spb + s, u_col + c)),
            pl.BlockSpec((BF16_SUBLANES, w), prev_map),
            pl.BlockSpec((tl, w), lambda b, c, s: (b * spb + s, gate_col + c)),
            pl.BlockSpec((CONV_WIDTH, w), lambda b, c, s: (0, c)),
            pl.BlockSpec((1, w), lambda b, c, s: (0, c)),
            pl.BlockSpec((None, w, 2 * w), lambda b, c, s: (c, 0, 0)),
            pl.BlockSpec((None, 1, 2 * w), lambda b, c, s: (c, 0, 0)),
            pl.BlockSpec((1, w), lambda b, c, s: (0, c)),
        ],
        out_specs=pl.BlockSpec((tl, w), lambda b, c, s: (b * spb + s, c)),
        out_shape=jax.ShapeDtypeStruct((n, LRU_WIDTH), BF16),
        scratch_shapes=[
            pltpu.VMEM((tl + 8, w), F32),
            pltpu.VMEM((tl, w), F32),
            pltpu.VMEM((tl, w), F32),
            pltpu.VMEM((1, w), F32),
        ],
        compiler_params=_params("arbitrary", "arbitrary", "arbitrary"),
        name="rglru",
    )(z, z, z, conv_w, conv_b, w_rg, b_rg, lam)


def _merge_body(ya_ref, yl_ref, ga_ref, gl_ref, pa_ref, pl_ref, o_ref):
    pa = jnp.dot(ya_ref[...], pa_ref[...], preferred_element_type=F32)
    pr = jnp.dot(yl_ref[...], pl_ref[...], preferred_element_type=F32)
    merged = jax.nn.sigmoid(ga_ref[...].astype(F32)) * pa + jax.nn.sigmoid(gl_ref[...].astype(F32)) * pr
    o_ref[...] = merged.astype(o_ref.dtype)


def _merge(y_attn, y_lru, z, w_attn_proj, w_lru_proj, *, ga_col, gl_col, tm=512):
    n, d = y_attn.shape
    return pl.pallas_call(
        _merge_body,
        grid=(n // tm,),
        in_specs=[
            pl.BlockSpec((tm, d), lambda i: (i, 0)),
            pl.BlockSpec((tm, d), lambda i: (i, 0)),
            pl.BlockSpec((tm, d), lambda i: (i, ga_col)),
            pl.BlockSpec((tm, d), lambda i: (i, gl_col)),
            _resident((d, d), lambda i: (0, 0)),
            _resident((d, d), lambda i: (0, 0)),
        ],
        out_specs=pl.BlockSpec((tm, d), lambda i: (i, 0)),
        out_shape=jax.ShapeDtypeStruct((n, d), BF16),
        compiler_params=_params("arbitrary"),
        name="merge",
    )(y_attn, y_lru, z, z, w_attn_proj, w_lru_proj)


def _outproj_body(x_ref, m_ref, wo_ref, g_ref, wr_ref, br_ref, h_ref, xn_ref, idx_ref, wt_ref):
    h = x_ref[...] + jnp.dot(m_ref[...], wo_ref[...], preferred_element_type=F32)
    h_ref[...] = h
    xn = _rms_scale(h) * g_ref[...]
    xn_ref[...] = _pack_bf16_pairs(xn)
    logits = lax.dot_general(wr_ref[...], xn.astype(BF16), (((1,), (1,)), ((), ())),
                             preferred_element_type=F32) + br_ref[...]
    expert = lax.broadcasted_iota(jnp.int32, logits.shape, 0).astype(F32)
    vals = []
    for k in range(TOP_K):
        m = jnp.max(logits, axis=0, keepdims=True)
        sel = jnp.min(jnp.where(logits == m, expert, float(N_EXPERTS)), axis=0, keepdims=True)
        vals.append(m)
        idx_ref[k:k + 1, :] = sel.astype(jnp.int32)
        logits = jnp.where(expert == sel, -jnp.inf, logits)
    exps = [jnp.exp(v - vals[0]) for v in vals]
    denom = exps[0] + exps[1] + exps[2] + exps[3]
    for k in range(TOP_K):
        wt_ref[k:k + 1, :] = exps[k] / denom


def _outproj_router(x2, merged, w_out, g_ffn, w_router_t, b_router, *, tm=512):
    n, d = x2.shape
    return pl.pallas_call(
        _outproj_body,
        grid=(n // tm,),
        in_specs=[
            pl.BlockSpec((tm, d), lambda i: (i, 0)),
            pl.BlockSpec((tm, d), lambda i: (i, 0)),
            _resident((d, d), lambda i: (0, 0)),
            pl.BlockSpec((1, d), lambda i: (0, 0)),
            pl.BlockSpec((N_EXPERTS, d), lambda i: (0, 0)),
            pl.BlockSpec((N_EXPERTS, 1), lambda i: (0, 0)),
        ],
        out_specs=[
            pl.BlockSpec((tm, d), lambda i: (i, 0)),
            pl.BlockSpec((tm, d // 2), lambda i: (i, 0)),
            pl.BlockSpec((TOP_K, tm), lambda i: (0, i)),
            pl.BlockSpec((TOP_K, tm), lambda i: (0, i)),
        ],
        out_shape=[
            jax.ShapeDtypeStruct((n, d), F32),
            jax.ShapeDtypeStruct((n, d // 2), jnp.uint32),
            jax.ShapeDtypeStruct((TOP_K, n), jnp.int32),
            jax.ShapeDtypeStruct((TOP_K, n), F32),
        ],
        compiler_params=_params("arbitrary"),
        name="outproj_router",
    )(x2, merged, w_out, g_ffn, w_router_t, b_router)


def _row_copy(src, dst, sem, src_row, dst_row):
    return pltpu.make_async_copy(src.at[pl.ds(src_row, 1), :], dst.at[pl.ds(dst_row, 1), :], sem)


def _dispatch_body(dst_ref, x_ref, init_hbm, out_hbm, stage, sems):
    del init_hbm
    i = pl.program_id(0)
    n = pl.num_programs(0)
    c = dst_ref.shape[1]
    slot = i % 2
    stage[slot] = x_ref[...]
    for r in range(c):
        _row_copy(stage.at[slot], out_hbm, sems.at[slot], r // TOP_K, dst_ref[0, r]).start()

    @pl.when(i > 0)
    def _():
        for r in range(c):
            _row_copy(stage.at[1 - slot], out_hbm, sems.at[1 - slot], 0, 0).wait()

    @pl.when(i == n - 1)
    def _():
        for r in range(c):
            _row_copy(stage.at[slot], out_hbm, sems.at[slot], 0, 0).wait()


def _dispatch(rows, pos, n_sorted, *, tm=128):
    n, w = rows.shape
    dst3 = pos.reshape(n // tm, 1, tm * TOP_K)
    init = jnp.zeros((n_sorted, w), rows.dtype)
    return pl.pallas_call(
        _dispatch_body,
        grid=(n // tm,),
        in_specs=[
            pl.BlockSpec((None, 1, tm * TOP_K), lambda i: (i, 0, 0), memory_space=pltpu.SMEM),
            pl.BlockSpec((tm, w), lambda i: (i, 0)),
            pl.BlockSpec(memory_space=pl.ANY),
        ],
        out_specs=pl.BlockSpec(memory_space=pl.ANY),
        out_shape=jax.ShapeDtypeStruct((n_sorted, w), rows.dtype),
        scratch_shapes=[pltpu.VMEM((2, tm, w), rows.dtype), pltpu.SemaphoreType.DMA((2,))],
        input_output_aliases={2: 0},
        compiler_params=_params("arbitrary"),
        name="dispatch",
    )(dst3, rows, init)


def _combine_body(idx_ref, nxt_ref, wt_ref, src_hbm, o_ref, buf, sems):
    i = pl.program_id(0)
    n = pl.num_programs(0)
    tm = o_ref.shape[0]
    count = TOP_K * tm
    slot = i % 2

    @pl.when(i == 0)
    def _():
        for r in range(count):
            _row_copy(src_hbm, buf.at[0], sems.at[0], idx_ref[0, r], r).start()

    @pl.when(i + 1 < n)
    def _():
        for r in range(count):
            _row_copy(src_hbm, buf.at[1 - slot], sems.at[1 - slot], nxt_ref[0, r], r).start()

    for r in range(count):
        _row_copy(src_hbm, buf.at[slot], sems.at[slot], 0, r).wait()

    c = o_ref.shape[1] // 2
    acc_lo = jnp.zeros((tm, c), F32)
    acc_hi = jnp.zeros((tm, c), F32)
    for k in range(TOP_K):
        lo, hi = _unpack_bf16_pairs(buf[slot, k * tm:(k + 1) * tm, :])
        acc_lo = acc_lo + wt_ref[:, k:k + 1] * lo
        acc_hi = acc_hi + wt_ref[:, k:k + 1] * hi
    o_ref[:, :c] = acc_lo
    o_ref[:, c:] = acc_hi


def _combine(y_sorted, pos, weights, *, tm=128):
    n = pos.shape[0]
    w = y_sorted.shape[1]
    steps = n // tm
    idx3 = pos.reshape(steps, tm, TOP_K).transpose(0, 2, 1).reshape(steps, 1, TOP_K * tm)
    return pl.pallas_call(
        _combine_body,
        grid=(steps,),
        in_specs=[
            pl.BlockSpec((None, 1, TOP_K * tm), lambda i: (i, 0, 0), memory_space=pltpu.SMEM),
            pl.BlockSpec((None, 1, TOP_K * tm), lambda i: (jnp.minimum(i + 1, steps - 1), 0, 0),
                         memory_space=pltpu.SMEM),
            pl.BlockSpec((tm, TOP_K), lambda i: (i, 0)),
            pl.BlockSpec(memory_space=pl.ANY),
        ],
        out_specs=pl.BlockSpec((tm, 2 * w), lambda i: (i, 0)),
        out_shape=jax.ShapeDtypeStruct((n, 2 * w), F32),
        scratch_shapes=[pltpu.VMEM((2, TOP_K * tm, w), y_sorted.dtype), pltpu.SemaphoreType.DMA((2,))],
        compiler_params=_params("arbitrary"),
        name="combine",
    )(idx3, idx3, weights, y_sorted)


def _moe_mlp_body(ie_ref, nvb_ref, nu_ref, x_ref, wg_ref, wl_ref, w2_ref, bg_ref, bl_ref, b2_ref, o_ref, acc_ref):
    item = pl.program_id(0)
    chunk = pl.program_id(1)
    nvb = nvb_ref[item]

    @pl.when(chunk == 0)
    def _():
        acc_ref[...] = jnp.broadcast_to(b2_ref[...], acc_ref.shape)

    @pl.when(nvb > 0)
    def _():
        wg = wg_ref[...].astype(BF16)
        wl = wl_ref[...].astype(BF16)
        w2 = w2_ref[...].astype(BF16)

        def run_rows(first_block, n_blocks):
            rows = slice(first_block * EXPERT_BLOCK, (first_block + n_blocks) * EXPERT_BLOCK)
            lo, hi = _unpack_bf16_pairs(x_ref[rows, :])
            xj = jnp.concatenate([lo.astype(BF16), hi.astype(BF16)], axis=1)
            glu = jnp.dot(xj, wg, preferred_element_type=F32) + bg_ref[...]
            lin = jnp.dot(xj, wl, preferred_element_type=F32) + bl_ref[...]
            glu = jnp.minimum(glu, SWIGLU_LIMIT)
            lin = jnp.clip(lin, -SWIGLU_LIMIT, SWIGLU_LIMIT)
            act = glu * jax.nn.sigmoid(SWIGLU_ALPHA * glu) * (lin + 1.0)
            acc_ref[rows, :] += jnp.dot(act.astype(BF16), w2, preferred_element_type=F32)

        n_sub = x_ref.shape[0] // EXPERT_BLOCK
        for j in range(0, n_sub, 2):
            if j + 2 <= n_sub:
                pl.when(nvb >= j + 2)(functools.partial(run_rows, j, 2))
                pl.when(nvb == j + 1)(functools.partial(run_rows, j, 1))
            else:
                pl.when(nvb >= j + 1)(functools.partial(run_rows, j, 1))

    @pl.when(chunk == pl.num_programs(1) - 1)
    def _():
        o_ref[...] = _pack_bf16_pairs(acc_ref[...])


def _moe_mlp(x_sorted, w1, b1, w2, b2, item_expert, item_nvb, n_used):
    n_rows, half = x_sorted.shape
    d = 2 * half
    n_items = n_rows // EXPERT_ITEM
    f = w2.shape[1]
    nc = f // FF_CHUNK

    def live_chunk(i, c, nu):
        return jnp.where(i < nu[0], c, nc - 1)

    def rows_map(i, c, ie, nvb, nu):
        return (jnp.minimum(i, nu[0] - 1), 0)

    return pl.pallas_call(
        _moe_mlp_body,
        grid_spec=pltpu.PrefetchScalarGridSpec(
            num_scalar_prefetch=3,
            grid=(n_items, nc),
            in_specs=[
                pl.BlockSpec((EXPERT_ITEM, half), rows_map),
                pl.BlockSpec((None, d, FF_CHUNK), lambda i, c, ie, nvb, nu: (ie[i], 0, live_chunk(i, c, nu))),
                pl.BlockSpec((None, d, FF_CHUNK), lambda i, c, ie, nvb, nu: (ie[i], 0, nc + live_chunk(i, c, nu))),
                pl.BlockSpec((None, FF_CHUNK, d), lambda i, c, ie, nvb, nu: (ie[i], live_chunk(i, c, nu), 0)),
                pl.BlockSpec((None, 1, FF_CHUNK), lambda i, c, ie, nvb, nu: (ie[i], 0, live_chunk(i, c, nu))),
                pl.BlockSpec((None, 1, FF_CHUNK), lambda i, c, ie, nvb, nu: (ie[i], 0, nc + live_chunk(i, c, nu))),
                pl.BlockSpec((None, 1, d), lambda i, c, ie, nvb, nu: (ie[i], 0, 0)),
            ],
            out_specs=pl.BlockSpec((EXPERT_ITEM, half), lambda i, c, ie, nvb, nu: (i, 0)),
            scratch_shapes=[pltpu.VMEM((EXPERT_ITEM, d), F32)],
        ),
        out_shape=jax.ShapeDtypeStruct((n_rows, half), jnp.uint32),
        compiler_params=pltpu.CompilerParams(dimension_semantics=("arbitrary", "arbitrary"),
                                             vmem_limit_bytes=MOE_VMEM_LIMIT_BYTES),
        name="moe_mlp",
    )(item_expert, item_nvb, n_used, x_sorted, w1, w1, w2, b1, b1, b2)


def _final_body(h_ref, moe_ref, p_ref, gp_ref, wg_ref, wp_ref, gf_ref, o_ref, *, final_norm):
    h = h_ref[...] + moe_ref[...]
    xn = (_rms_scale(h) * gp_ref[...]).astype(BF16)
    gate = jax.nn.sigmoid(jnp.dot(xn, wg_ref[...], preferred_element_type=F32))
    ple = jnp.dot(p_ref[...].astype(BF16), wp_ref[...], preferred_element_type=F32)
    h = h + gate * ple
    o_ref[...] = _rms_scale(h) * gf_ref[...] if final_norm else h


def _final(h1, moe, p2, g_ple, w_ple_gate, w_ple, g_final, *, final_norm, tm=512):
    n, d = h1.shape
    return pl.pallas_call(
        functools.partial(_final_body, final_norm=final_norm),
        grid=(n // tm,),
        in_specs=[
            pl.BlockSpec((tm, d), lambda i: (i, 0)),
            pl.BlockSpec((tm, d), lambda i: (i, 0)),
            pl.BlockSpec((tm, PLE_DIM), lambda i: (i, 0)),
            pl.BlockSpec((1, d), lambda i: (0, 0)),
            _resident((d, d), lambda i: (0, 0)),
            _resident((PLE_DIM, d), lambda i: (0, 0)),
            pl.BlockSpec((1, d), lambda i: (0, 0)),
        ],
        out_specs=pl.BlockSpec((tm, d), lambda i: (i, 0)),
        out_shape=jax.ShapeDtypeStruct((n, d), F32),
        compiler_params=_params("arbitrary"),
        name="final",
    )(h1, moe, p2, g_ple, w_ple_gate, w_ple, g_final)


def _routing_tables(expert_ids, n_tokens):
    nk = n_tokens * TOP_K
    sub = EXPERT_ITEM // EXPERT_BLOCK
    e_flat = expert_ids.T.reshape(nk)
    onehot = (e_flat[:, None] == jnp.arange(N_EXPERTS, dtype=jnp.int32)[None, :]).astype(jnp.int32)
    csum = jnp.cumsum(onehot, axis=0)
    rank = jnp.sum(onehot * csum, axis=1) - 1
    counts = csum[-1]
    padded = (counts + EXPERT_ITEM - 1) // EXPERT_ITEM * EXPERT_ITEM
    pend = jnp.cumsum(padded)
    pstart = pend - padded
    dest = pstart[e_flat] + rank
    n_items = nk // EXPERT_ITEM + N_EXPERTS
    n_used = (pend[-1] // EXPERT_ITEM).astype(jnp.int32)
    item_start = jnp.arange(n_items, dtype=jnp.int32) * EXPERT_ITEM
    item_expert = jnp.minimum(jnp.searchsorted(pend, item_start, side='right'), N_EXPERTS - 1).astype(jnp.int32)
    used = jnp.arange(n_items) < n_used
    item_expert = jnp.where(used, item_expert, item_expert[n_used - 1])
    rows_left = counts[item_expert] - (item_start - pstart[item_expert])
    item_nvb = jnp.where(used, jnp.clip((rows_left + EXPERT_BLOCK - 1) // EXPERT_BLOCK, 0, sub), 0).astype(jnp.int32)
    pos = dest.reshape(n_tokens, TOP_K).astype(jnp.int32)
    return pos, item_expert, item_nvb, n_used.reshape(1), n_items * EXPERT_ITEM


def _layer(h, p_l, norm_mix_g, w_in, b_in, conv_w, conv_b, w_rg_a, b_rg_a, w_rg_x, b_rg_x, lru_lambda,
           attn_sinks, w_attn_proj, w_lru_proj, w_out, norm_ffn_g, w_router, b_router, w_mlp1, b_mlp1,
           w_mlp2, b_mlp2, norm_ple_g, w_ple, w_ple_gate, norm_final_g, *, batch, seq, final_norm):
    n = h.shape[0]
    row = lambda v: v.reshape(1, -1)

    kv0 = Q_WIDTH
    rest0 = Q_WIDTH + 2 * KV_WIDTH
    reorder = lambda a: jnp.concatenate([a[..., :kv0], a[..., rest0:], a[..., kv0:rest0]], axis=-1)
    z = _inproj(h, row(norm_mix_g), reorder(w_in).astype(BF16), row(reorder(b_in)))

    kv_col = (Q_WIDTH + 2 * LRU_WIDTH + 2 * D_MODEL) // (2 * KV_WIDTH)
    y_attn = _attention(z, attn_sinks, batch=batch, seq=seq, q_col=0, kv_col=kv_col)

    w_rg = jnp.concatenate([w_rg_a, w_rg_x], axis=-1).astype(BF16)
    b_rg = jnp.concatenate([b_rg_a, b_rg_x], axis=-1).reshape(LRU_BLOCKS, 1, 2 * LRU_BLOCK_WIDTH)
    y_lru = _rglru(z, conv_w, row(conv_b), w_rg, b_rg, row(lru_lambda), batch=batch, seq=seq,
                   u_col=Q_WIDTH // LRU_BLOCK_WIDTH, gate_col=(Q_WIDTH + LRU_WIDTH) // LRU_BLOCK_WIDTH)

    merged = _merge(y_attn, y_lru, z, w_attn_proj.astype(BF16), w_lru_proj.astype(BF16),
                    ga_col=(Q_WIDTH + 2 * LRU_WIDTH) // D_MODEL, gl_col=(Q_WIDTH + 2 * LRU_WIDTH + D_MODEL) // D_MODEL)
    h1, xn2, expert_ids, expert_w = _outproj_router(
        h, merged, w_out.astype(BF16), row(norm_ffn_g), w_router.T.astype(BF16), b_router.reshape(N_EXPERTS, 1))

    pos, item_expert, item_nvb, n_used, n_sorted = _routing_tables(expert_ids, n)
    x_sorted = _dispatch(xn2, pos, n_sorted)
    y_sorted = _moe_mlp(x_sorted, w_mlp1, b_mlp1[:, None, :], w_mlp2, b_mlp2[:, None, :], item_expert, item_nvb, n_used)
    moe = _combine(y_sorted, pos, expert_w.T)

    return _final(h1, moe, p_l, row(norm_ple_g), w_ple_gate.astype(BF16), w_ple.astype(BF16), row(norm_final_g),
                  final_norm=final_norm)


def kernel(x, p, norm_mix_g, w_in, b_in, conv_w, conv_b, w_rg_a, b_rg_a, w_rg_x, b_rg_x, lru_lambda, attn_sinks, w_attn_proj, w_lru_proj, w_out, norm_ffn_g, w_router, b_router, w_mlp1, b_mlp1, w_mlp2, b_mlp2, norm_ple_g, w_ple, w_ple_gate, norm_final_g):
    batch, seq, d = x.shape
    depth = p.shape[0]
    h = x.reshape(batch * seq, d)
    for l in range(depth):
        h = _layer(h, p[l].reshape(batch * seq, -1), norm_mix_g[l], w_in[l], b_in[l], conv_w[l], conv_b[l],
                   w_rg_a[l], b_rg_a[l], w_rg_x[l], b_rg_x[l], lru_lambda[l], attn_sinks[l], w_attn_proj[l],
                   w_lru_proj[l], w_out[l], norm_ffn_g[l], w_router[l], b_router[l], w_mlp1[l], b_mlp1[l],
                   w_mlp2[l], b_mlp2[l], norm_ple_g[l], w_ple[l], w_ple_gate[l], norm_final_g,
                   batch=batch, seq=seq, final_norm=(l == depth - 1))
    return h.reshape(batch, seq, d)
```

```python
import functools

import jax
import jax.numpy as jnp
from jax import lax
from jax.experimental import pallas as pl
from jax.experimental.pallas import tpu as pltpu

D_MODEL = 2048
N_Q_HEADS = 32
N_KV_HEADS = 4
HEAD_DIM = 64
Q_WIDTH = N_Q_HEADS * HEAD_DIM
KV_WIDTH = N_KV_HEADS * HEAD_DIM
WINDOW = 128
LRU_WIDTH = D_MODEL
LRU_BLOCKS = 8
LRU_BLOCK_WIDTH = LRU_WIDTH // LRU_BLOCKS
CONV_WIDTH = 4
LRU_C = 8.0
IN_WIDTH = Q_WIDTH + 2 * KV_WIDTH + 2 * LRU_WIDTH + 2 * D_MODEL
N_EXPERTS = 32
TOP_K = 4
D_EXPERT = D_MODEL
SWIGLU_LIMIT = 7.0
SWIGLU_ALPHA = 1.702
EXPERT_BLOCK = 256
EXPERT_ITEM = 1280
FF_CHUNK = 512
PLE_DIM = 256
RMS_EPS = 1e-6

VMEM_LIMIT_BYTES = 56 * 1024 * 1024
MOE_VMEM_LIMIT_BYTES = 60 * 1024 * 1024
BF16_SUBLANES = 16

F32 = jnp.float32
BF16 = jnp.bfloat16


def _params(*sem):
    return pltpu.CompilerParams(dimension_semantics=sem, vmem_limit_bytes=VMEM_LIMIT_BYTES)


def _resident(shape, index_map):
    return pl.BlockSpec(shape, index_map, pipeline_mode=pl.Buffered(1))


def _rms_scale(x):
    var = jnp.mean(x * x, axis=-1, keepdims=True)
    return x * lax.rsqrt(var + RMS_EPS)


def _pack_bf16_pairs(x):
    c = x.shape[1] // 2
    xb = x.astype(BF16).astype(F32)
    lo = lax.bitcast_convert_type(xb[:, :c], jnp.uint32) >> 16
    hi = lax.bitcast_convert_type(xb[:, c:], jnp.uint32) & jnp.uint32(0xFFFF0000)
    return hi | lo


def _unpack_bf16_pairs(w):
    lo = lax.bitcast_convert_type(w << 16, F32)
    hi = lax.bitcast_convert_type(w & jnp.uint32(0xFFFF0000), F32)
    return lo, hi


def _inproj_body(x_ref, g_ref, w_ref, b_ref, o_ref, xn_ref):
    @pl.when(pl.program_id(1) == 0)
    def _():
        xn_ref[...] = (_rms_scale(x_ref[...]) * g_ref[...]).astype(BF16)

    acc = jnp.dot(xn_ref[...], w_ref[...], preferred_element_type=F32)
    o_ref[...] = (acc + b_ref[...]).astype(o_ref.dtype)


def _inproj(x2, g, w, b, *, tm=1024, tn=1536):
    n, d = x2.shape
    width = w.shape[1]
    return pl.pallas_call(
        _inproj_body,
        grid=(n // tm, width // tn),
        in_specs=[
            pl.BlockSpec((tm, d), lambda i, j: (i, 0)),
            pl.BlockSpec((1, d), lambda i, j: (0, 0)),
            pl.BlockSpec((d, tn), lambda i, j: (0, j)),
            pl.BlockSpec((1, tn), lambda i, j: (0, j)),
        ],
        out_specs=pl.BlockSpec((tm, tn), lambda i, j: (i, j)),
        out_shape=jax.ShapeDtypeStruct((n, width), BF16),
        scratch_shapes=[pltpu.VMEM((tm, d), BF16)],
        compiler_params=_params("arbitrary", "arbitrary"),
        name="inproj",
    )(x2, g, w, b)


def _block_diag_pair(pair, odd, lo_mask):
    swapped = jnp.concatenate([pair[:, HEAD_DIM:], pair[:, :HEAD_DIM]], axis=1)
    zero = jnp.zeros_like(pair)
    if odd:
        lo = jnp.where(lo_mask, swapped, zero)
        hi = jnp.where(lo_mask, zero, pair)
    else:
        lo = jnp.where(lo_mask, pair, zero)
        hi = jnp.where(lo_mask, zero, swapped)
    return jnp.concatenate([lo, hi], axis=0)


def _attn_body(sinks_ref, q_ref, kvp_ref, kvc_ref, o_ref, *, tq):
    nblk = tq // WINDOW
    group = N_Q_HEADS // N_KV_HEADS
    pairs = group // 2
    rows = pairs * WINDOW
    band_w = 2 * WINDOW
    s_idx = pl.program_id(1)

    lo_mask = lax.broadcasted_iota(jnp.int32, (1, 2 * HEAD_DIM), 1) < HEAD_DIM
    qi = lax.broadcasted_iota(jnp.int32, (rows, band_w), 0) % WINDOW
    kj = lax.broadcasted_iota(jnp.int32, (rows, band_w), 1)
    in_band = (kj > qi) & (kj <= qi + WINDOW)
    first_key = jnp.where(s_idx > 0, 0, WINDOW)
    row_pair = lax.broadcasted_iota(jnp.int32, (rows, 1), 0) // WINDOW
    neg = jnp.finfo(F32).min

    for i in range(nblk):
        if i == 0:
            band = jnp.concatenate([kvp_ref[...], kvc_ref[0:WINDOW, :]], axis=0)
            mask = in_band & (kj >= first_key)
        else:
            band = kvc_ref[(i - 1) * WINDOW:(i + 1) * WINDOW, :]
            mask = in_band
        for kh in range(N_KV_HEADS):
            col = (kh // 2) * 2 * HEAD_DIM
            kk = _block_diag_pair(band[:, col:col + 2 * HEAD_DIM], kh % 2 == 1, lo_mask)
            vv = _block_diag_pair(band[:, KV_WIDTH + col:KV_WIDTH + col + 2 * HEAD_DIM], kh % 2 == 1, lo_mask)
            qbase = kh * group * HEAD_DIM
            q4 = jnp.concatenate(
                [q_ref[i * WINDOW:(i + 1) * WINDOW, qbase + 2 * HEAD_DIM * j:qbase + 2 * HEAD_DIM * (j + 1)]
                 for j in range(pairs)], axis=0)
            s = lax.dot_general(q4, kk, (((1,), (1,)), ((), ())), preferred_element_type=F32)
            s = s * (HEAD_DIM ** -0.5)
            sink_lo = jnp.zeros((rows, 1), F32)
            sink_hi = jnp.zeros((rows, 1), F32)
            for j in range(pairs):
                sink_lo = jnp.where(row_pair == j, sinks_ref[kh * group + 2 * j], sink_lo)
                sink_hi = jnp.where(row_pair == j, sinks_ref[kh * group + 2 * j + 1], sink_hi)
            halves = []
            inv = []
            for half, sink in ((0, sink_lo), (1, sink_hi)):
                sh = jnp.where(mask, s[:, half * band_w:(half + 1) * band_w], neg)
                m = jnp.maximum(jnp.max(sh, axis=1, keepdims=True), sink)
                p = jnp.exp(sh - m)
                denom = jnp.sum(p, axis=1, keepdims=True) + jnp.exp(sink - m)
                halves.append(p.astype(BF16))
                inv.append(1.0 / denom)
            pcat = jnp.concatenate(halves, axis=1)
            o = jnp.dot(pcat, vv, preferred_element_type=F32)
            o = o * jnp.where(lo_mask, inv[0], inv[1])
            for j in range(pairs):
                o_ref[i * WINDOW:(i + 1) * WINDOW, qbase + 2 * HEAD_DIM * j:qbase + 2 * HEAD_DIM * (j + 1)] = (
                    o[j * WINDOW:(j + 1) * WINDOW, :].astype(o_ref.dtype))


def _attention(z, sinks, *, batch, seq, q_col, kv_col, tq=512):
    n = z.shape[0]
    spb = seq // tq
    wpb = seq // WINDOW
    kvw = 2 * KV_WIDTH

    def prev_map(b, s, sinks_ref):
        return (jnp.maximum(b * wpb + s * (tq // WINDOW) - 1, 0), kv_col)

    return pl.pallas_call(
        functools.partial(_attn_body, tq=tq),
        grid_spec=pltpu.PrefetchScalarGridSpec(
            num_scalar_prefetch=1,
            grid=(batch, spb),
            in_specs=[
                pl.BlockSpec((tq, Q_WIDTH), lambda b, s, sinks_ref: (b * spb + s, q_col)),
                pl.BlockSpec((WINDOW, kvw), prev_map),
                pl.BlockSpec((tq, kvw), lambda b, s, sinks_ref: (b * spb + s, kv_col)),
            ],
            out_specs=pl.BlockSpec((tq, Q_WIDTH), lambda b, s, sinks_ref: (b * spb + s, 0)),
        ),
        out_shape=jax.ShapeDtypeStruct((n, Q_WIDTH), BF16),
        compiler_params=_params("arbitrary", "arbitrary"),
        name="attn",
    )(sinks, z, z, z)


def _gelu_tanh(x):
    c = (2.0 / jnp.pi) ** 0.5
    return x * (0.5 * (1.0 + jnp.tanh(c * (x + 0.044715 * (x * x * x)))))


def _lru_body(u_ref, up_ref, gate_ref, cw_ref, cb_ref, wrg_ref, brg_ref, lam_ref, y_ref,
              ext_ref, a_ref, b_ref, h_ref):
    s_idx = pl.program_id(2)
    tl, w = u_ref.shape
    pad = 8

    @pl.when(s_idx == 0)
    def _():
        ext_ref[0:pad, :] = jnp.zeros((pad, w), F32)
        h_ref[...] = jnp.zeros_like(h_ref)

    @pl.when(s_idx > 0)
    def _():
        ext_ref[0:pad, :] = up_ref[...].astype(F32)[BF16_SUBLANES - pad:BF16_SUBLANES, :]

    ext_ref[pad:pad + tl, :] = u_ref[...].astype(F32)
    first = pad - (CONV_WIDTH - 1)
    uc = cb_ref[...] + ext_ref[first:first + tl, :] * cw_ref[0:1, :]
    for tap in range(1, CONV_WIDTH):
        uc = uc + ext_ref[first + tap:first + tap + tl, :] * cw_ref[tap:tap + 1, :]

    gates = jnp.dot(uc.astype(BF16), wrg_ref[...], preferred_element_type=F32) + brg_ref[...]
    r = jax.nn.sigmoid(gates[:, :w])
    ig = jax.nn.sigmoid(gates[:, w:])
    neg_lam = -lam_ref[...]
    softplus = jnp.maximum(neg_lam, 0.0) + jnp.log1p(jnp.exp(-jnp.abs(neg_lam)))
    log_a = (-LRU_C * r) * softplus
    a = jnp.exp(log_a)
    m2 = -jnp.tanh(log_a) * (a * a + 1.0)
    mult = jnp.where(m2 > 0.0, m2 * lax.rsqrt(m2), 0.0)
    row = lax.broadcasted_iota(jnp.int32, (tl, 1), 0)
    mult = jnp.where((row == 0) & (s_idx == 0), 1.0, mult)
    b = mult * (ig * uc)

    sub = row % 8
    for d in (1, 2, 4):
        keep = sub >= d
        b_prev = jnp.where(keep, pltpu.roll(b, d, 0), 0.0)
        a_prev = jnp.where(keep, pltpu.roll(a, d, 0), 1.0)
        b = b + a * b_prev
        a = a * a_prev
    a_ref[...] = a
    b_ref[...] = b

    def group_step(gi, h):
        off = pl.multiple_of(gi * 8, 8)
        hg = b_ref[pl.ds(off, 8), :] + a_ref[pl.ds(off, 8), :] * h
        b_ref[pl.ds(off, 8), :] = hg
        return hg[7:8, :]

    h_ref[...] = lax.fori_loop(0, tl // 8, group_step, h_ref[...])
    y_ref[...] = (b_ref[...] * _gelu_tanh(gate_ref[...].astype(F32))).astype(y_ref.dtype)


def _rglru(z, conv_w, conv_b, w_rg, b_rg, lam, *, batch, seq, u_col, gate_col, tl=512):
    n = z.shape[0]
    w = LRU_BLOCK_WIDTH
    spb = seq // tl
    rpb = seq // BF16_SUBLANES

    def prev_map(b, c, s):
        return (jnp.maximum(b * rpb + s * (tl // BF16_SUBLANES) - 1, 0), u_col + c)

    return pl.pallas_call(
        _lru_body,
        grid=(batch, LRU_BLOCKS, spb),
        in_specs=[
            pl.BlockSpec((tl, w), lambda b, c, s: (b * spb + s, u_col + c)),
            pl.BlockSpec((BF16_SUBLANES, w), prev_map),
            pl.BlockSpec((tl, w), lambda b, c, s: (b * spb + s, gate_col + c)),
            pl.BlockSpec((CONV_WIDTH, w), lambda b, c, s: (0, c)),
            pl.BlockSpec((1, w), lambda b, c, s: (0, c)),
            pl.BlockSpec((None, w, 2 * w), lambda b, c, s: (c, 0, 0)),
            pl.BlockSpec((None, 1, 2 * w), lambda b, c, s: (c, 0, 0)),
            pl.BlockSpec((1, w), lambda b, c, s: (0, c)),
        ],
        out_specs=pl.BlockSpec((tl, w), lambda b, c, s: (b * spb + s, c)),
        out_shape=jax.ShapeDtypeStruct((n, LRU_WIDTH), BF16),
        scratch_shapes=[
            pltpu.VMEM((tl + 8, w), F32),
            pltpu.VMEM((tl, w), F32),
            pltpu.VMEM((tl, w), F32),
            pltpu.VMEM((1, w), F32),
        ],
        compiler_params=_params("arbitrary", "arbitrary", "arbitrary"),
        name="rglru",
    )(z, z, z, conv_w, conv_b, w_rg, b_rg, lam)


def _merge_body(ya_ref, yl_ref, ga_ref, gl_ref, pa_ref, pl_ref, o_ref):
    pa = jnp.dot(ya_ref[...], pa_ref[...], preferred_element_type=F32)
    pr = jnp.dot(yl_ref[...], pl_ref[...], preferred_element_type=F32)
    merged = jax.nn.sigmoid(ga_ref[...].astype(F32)) * pa + jax.nn.sigmoid(gl_ref[...].astype(F32)) * pr
    o_ref[...] = merged.astype(o_ref.dtype)


def _merge(y_attn, y_lru, z, w_attn_proj, w_lru_proj, *, ga_col, gl_col, tm=512):
    n, d = y_attn.shape
    return pl.pallas_call(
        _merge_body,
        grid=(n // tm,),
        in_specs=[
            pl.BlockSpec((tm, d), lambda i: (i, 0)),
            pl.BlockSpec((tm, d), lambda i: (i, 0)),
            pl.BlockSpec((tm, d), lambda i: (i, ga_col)),
            pl.BlockSpec((tm, d), lambda i: (i, gl_col)),
            _resident((d, d), lambda i: (0, 0)),
            _resident((d, d), lambda i: (0, 0)),
        ],
        out_specs=pl.BlockSpec((tm, d), lambda i: (i, 0)),
        out_shape=jax.ShapeDtypeStruct((n, d), BF16),
        compiler_params=_params("arbitrary"),
        name="merge",
    )(y_attn, y_lru, z, z, w_attn_proj, w_lru_proj)


def _outproj_body(x_ref, m_ref, wo_ref, g_ref, wr_ref, br_ref, h_ref, xn_ref, idx_ref, wt_ref):
    h = x_ref[...] + jnp.dot(m_ref[...], wo_ref[...], preferred_element_type=F32)
    h_ref[...] = h
    xn = _rms_scale(h) * g_ref[...]
    xn_ref[...] = _pack_bf16_pairs(xn)
    logits = lax.dot_general(wr_ref[...], xn.astype(BF16), (((1,), (1,)), ((), ())),
                             preferred_element_type=F32) + br_ref[...]
    expert = lax.broadcasted_iota(jnp.int32, logits.shape, 0).astype(F32)
    vals = []
    for k in range(TOP_K):
        m = jnp.max(logits, axis=0, keepdims=True)
        sel = jnp.min(jnp.where(logits == m, expert, float(N_EXPERTS)), axis=0, keepdims=True)
        vals.append(m)
        idx_ref[k:k + 1, :] = sel.astype(jnp.int32)
        logits = jnp.where(expert == sel, -jnp.inf, logits)
    exps = [jnp.exp(v - vals[0]) for v in vals]
    denom = exps[0] + exps[1] + exps[2] + exps[3]
    for k in range(TOP_K):
        wt_ref[k:k + 1, :] = exps[k] / denom


def _outproj_router(x2, merged, w_out, g_ffn, w_router_t, b_router, *, tm=512):
    n, d = x2.shape
    return pl.pallas_call(
        _outproj_body,
        grid=(n // tm,),
        in_specs=[
            pl.BlockSpec((tm, d), lambda i: (i, 0)),
            pl.BlockSpec((tm, d), lambda i: (i, 0)),
            _resident((d, d), lambda i: (0, 0)),
            pl.BlockSpec((1, d), lambda i: (0, 0)),
            pl.BlockSpec((N_EXPERTS, d), lambda i: (0, 0)),
            pl.BlockSpec((N_EXPERTS, 1), lambda i: (0, 0)),
        ],
        out_specs=[
            pl.BlockSpec((tm, d), lambda i: (i, 0)),
            pl.BlockSpec((tm, d // 2), lambda i: (i, 0)),
            pl.BlockSpec((TOP_K, tm), lambda i: (0, i)),
            pl.BlockSpec((TOP_K, tm), lambda i: (0, i)),
        ],
        out_shape=[
            jax.ShapeDtypeStruct((n, d), F32),
            jax.ShapeDtypeStruct((n, d // 2), jnp.uint32),
            jax.ShapeDtypeStruct((TOP_K, n), jnp.int32),
            jax.ShapeDtypeStruct((TOP_K, n), F32),
        ],
        compiler_params=_params("arbitrary"),
        name="outproj_router",
    )(x2, merged, w_out, g_ffn, w_router_t, b_router)


def _row_copy(src, dst, sem, src_row, dst_row):
    return pltpu.make_async_copy(src.at[pl.ds(src_row, 1), :], dst.at[pl.ds(dst_row, 1), :], sem)


def _dispatch_body(dst_ref, x_ref, init_hbm, out_hbm, stage, sems):
    del init_hbm
    i = pl.program_id(0)
    n = pl.num_programs(0)
    c = dst_ref.shape[1]
    slot = i % 2
    stage[slot] = x_ref[...]
    for r in range(c):
        _row_copy(stage.at[slot], out_hbm, sems.at[slot], r // TOP_K, dst_ref[0, r]).start()

    @pl.when(i > 0)
    def _():
        for r in range(c):
            _row_copy(stage.at[1 - slot], out_hbm, sems.at[1 - slot], 0, 0).wait()

    @pl.when(i == n - 1)
    def _():
        for r in range(c):
            _row_copy(stage.at[slot], out_hbm, sems.at[slot], 0, 0).wait()


def _dispatch(rows, pos, n_sorted, *, tm=128):
    n, w = rows.shape
    dst3 = pos.reshape(n // tm, 1, tm * TOP_K)
    init = jnp.zeros((n_sorted, w), rows.dtype)
    return pl.pallas_call(
        _dispatch_body,
        grid=(n // tm,),
        in_specs=[
            pl.BlockSpec((None, 1, tm * TOP_K), lambda i: (i, 0, 0), memory_space=pltpu.SMEM),
            pl.BlockSpec((tm, w), lambda i: (i, 0)),
            pl.BlockSpec(memory_space=pl.ANY),
        ],
        out_specs=pl.BlockSpec(memory_space=pl.ANY),
        out_shape=jax.ShapeDtypeStruct((n_sorted, w), rows.dtype),
        scratch_shapes=[pltpu.VMEM((2, tm, w), rows.dtype), pltpu.SemaphoreType.DMA((2,))],
        input_output_aliases={2: 0},
        compiler_params=_params("arbitrary"),
        name="dispatch",
    )(dst3, rows, init)


def _combine_body(idx_ref, nxt_ref, wt_ref, src_hbm, o_ref, buf, sems):
    i = pl.program_id(0)
    n = pl.num_programs(0)
    tm = o_ref.shape[0]
    count = TOP_K * tm
    slot = i % 2

    @pl.when(i == 0)
    def _():
        for r in range(count):
            _row_copy(src_hbm, buf.at[0], sems.at[0], idx_ref[0, r], r).start()

    @pl.when(i + 1 < n)
    def _():
        for r in range(count):
            _row_copy(src_hbm, buf.at[1 - slot], sems.at[1 - slot], nxt_ref[0, r], r).start()

    for r in range(count):
        _row_copy(src_hbm, buf.at[slot], sems.at[slot], 0, r).wait()

    c = o_ref.shape[1] // 2
    acc_lo = jnp.zeros((tm, c), F32)
    acc_hi = jnp.zeros((tm, c), F32)
    for k in range(TOP_K):
        lo, hi = _unpack_bf16_pairs(buf[slot, k * tm:(k + 1) * tm, :])
        acc_lo = acc_lo + wt_ref[:, k:k + 1] * lo
        acc_hi = acc_hi + wt_ref[:, k:k + 1] * hi
    o_ref[:, :c] = acc_lo
    o_ref[:, c:] = acc_hi


def _combine(y_sorted, pos, weights, *, tm=128):
    n = pos.shape[0]
    w = y_sorted.shape[1]
    steps = n // tm
    idx3 = pos.reshape(steps, tm, TOP_K).transpose(0, 2, 1).reshape(steps, 1, TOP_K * tm)
    return pl.pallas_call(
        _combine_body,
        grid=(steps,),
        in_specs=[
            pl.BlockSpec((None, 1, TOP_K * tm), lambda i: (i, 0, 0), memory_space=pltpu.SMEM),
            pl.BlockSpec((None, 1, TOP_K * tm), lambda i: (jnp.minimum(i + 1, steps - 1), 0, 0),
                         memory_space=pltpu.SMEM),
            pl.BlockSpec((tm, TOP_K), lambda i: (i, 0)),
            pl.BlockSpec(memory_space=pl.ANY),
        ],
        out_specs=pl.BlockSpec((tm, 2 * w), lambda i: (i, 0)),
        out_shape=jax.ShapeDtypeStruct((n, 2 * w), F32),
        scratch_shapes=[pltpu.VMEM((2, TOP_K * tm, w), y_sorted.dtype), pltpu.SemaphoreType.DMA((2,))],
        compiler_params=_params("arbitrary"),
        name="combine",
    )(idx3, idx3, weights, y_sorted)


def _moe_mlp_body(ie_ref, nvb_ref, nu_ref, x_ref, w1_hbm, w2_hbm, bg_ref, bl_ref, b2_ref, o_ref,
                  acc_ref, sg, sl, s2, wg_ref, wl_ref, w2_ref, sems):
    item = pl.program_id(0)
    chunk = pl.program_id(1)
    n_items = pl.num_programs(0)
    n_chunks = pl.num_programs(1)
    nvb = nvb_ref[item]
    f = w2_hbm.shape[1]
    fc = sg.shape[1]

    def weight_copies(it, ch):
        e = ie_ref[it]
        c0 = pl.multiple_of(ch * fc, fc)
        return (pltpu.make_async_copy(w1_hbm.at[e, :, pl.ds(c0, fc)], sg, sems.at[0]),
                pltpu.make_async_copy(w1_hbm.at[e, :, pl.ds(f + c0, fc)], sl, sems.at[1]),
                pltpu.make_async_copy(w2_hbm.at[e, pl.ds(c0, fc), :], s2, sems.at[2]))

    @pl.when(chunk == 0)
    def _():
        acc_ref[...] = jnp.broadcast_to(b2_ref[...], acc_ref.shape)

    @pl.when((item == 0) & (chunk == 0))
    def _():
        for cp in weight_copies(item, chunk):
            cp.start()

    @pl.when(nvb > 0)
    def _():
        for cp in weight_copies(item, chunk):
            cp.wait()
        wg_ref[...] = sg[...].astype(BF16)
        wl_ref[...] = sl[...].astype(BF16)
        w2_ref[...] = s2[...].astype(BF16)

        same_item = chunk + 1 < n_chunks
        nxt_item = jnp.minimum(jnp.where(same_item, item, item + 1), n_items - 1)
        nxt_chunk = jnp.where(same_item, chunk + 1, 0)

        @pl.when(same_item | ((item + 1 < n_items) & (nvb_ref[nxt_item] > 0)))
        def _():
            for cp in weight_copies(nxt_item, nxt_chunk):
                cp.start()

        def run_rows(first_block, n_blocks):
            rows = slice(first_block * EXPERT_BLOCK, (first_block + n_blocks) * EXPERT_BLOCK)
            lo, hi = _unpack_bf16_pairs(x_ref[rows, :])
            xj = jnp.concatenate([lo.astype(BF16), hi.astype(BF16)], axis=1)
            glu = jnp.dot(xj, wg_ref[...], preferred_element_type=F32) + bg_ref[...]
            lin = jnp.dot(xj, wl_ref[...], preferred_element_type=F32) + bl_ref[...]
            glu = jnp.minimum(glu, SWIGLU_LIMIT)
            lin = jnp.clip(lin, -SWIGLU_LIMIT, SWIGLU_LIMIT)
            act = glu * jax.nn.sigmoid(SWIGLU_ALPHA * glu) * (lin + 1.0)
            acc_ref[rows, :] += jnp.dot(act.astype(BF16), w2_ref[...], preferred_element_type=F32)

        n_sub = x_ref.shape[0] // EXPERT_BLOCK
        for j in range(0, n_sub, 2):
            if j + 2 <= n_sub:
                pl.when(nvb >= j + 2)(functools.partial(run_rows, j, 2))
                pl.when(nvb == j + 1)(functools.partial(run_rows, j, 1))
            else:
                pl.when(nvb >= j + 1)(functools.partial(run_rows, j, 1))

    @pl.when(chunk == n_chunks - 1)
    def _():
        o_ref[...] = _pack_bf16_pairs(acc_ref[...])


def _moe_mlp(x_sorted, w1, b1, w2, b2, item_expert, item_nvb, n_used):
    n_rows, half = x_sorted.shape
    d = 2 * half
    n_items = n_rows // EXPERT_ITEM
    f = w2.shape[1]
    nc = f // FF_CHUNK

    def live_chunk(i, c, nu):
        return jnp.where(i < nu[0], c, nc - 1)

    def rows_map(i, c, ie, nvb, nu):
        return (jnp.minimum(i, nu[0] - 1), 0)

    return pl.pallas_call(
        _moe_mlp_body,
        grid_spec=pltpu.PrefetchScalarGridSpec(
            num_scalar_prefetch=3,
            grid=(n_items, nc),
            in_specs=[
                pl.BlockSpec((EXPERT_ITEM, half), rows_map),
                pl.BlockSpec(memory_space=pl.ANY),
                pl.BlockSpec(memory_space=pl.ANY),
                pl.BlockSpec((None, 1, FF_CHUNK), lambda i, c, ie, nvb, nu: (ie[i], 0, live_chunk(i, c, nu))),
                pl.BlockSpec((None, 1, FF_CHUNK), lambda i, c, ie, nvb, nu: (ie[i], 0, nc + live_chunk(i, c, nu))),
                pl.BlockSpec((None, 1, d), lambda i, c, ie, nvb, nu: (ie[i], 0, 0)),
            ],
            out_specs=pl.BlockSpec((EXPERT_ITEM, half), lambda i, c, ie, nvb, nu: (i, 0)),
            scratch_shapes=[
                pltpu.VMEM((EXPERT_ITEM, d), F32),
                pltpu.VMEM((d, FF_CHUNK), F32),
                pltpu.VMEM((d, FF_CHUNK), F32),
                pltpu.VMEM((FF_CHUNK, d), F32),
                pltpu.VMEM((d, FF_CHUNK), BF16),
                pltpu.VMEM((d, FF_CHUNK), BF16),
                pltpu.VMEM((FF_CHUNK, d), BF16),
                pltpu.SemaphoreType.DMA((3,)),
            ],
        ),
        out_shape=jax.ShapeDtypeStruct((n_rows, half), jnp.uint32),
        compiler_params=pltpu.CompilerParams(dimension_semantics=("arbitrary", "arbitrary"),
                                             vmem_limit_bytes=MOE_VMEM_LIMIT_BYTES),
        name="moe_mlp",
    )(item_expert, item_nvb, n_used, x_sorted, w1, w2, b1, b1, b2)


def _final_body(h_ref, moe_ref, p_ref, gp_ref, wg_ref, wp_ref, gf_ref, o_ref, *, final_norm):
    h = h_ref[...] + moe_ref[...]
    xn = (_rms_scale(h) * gp_ref[...]).astype(BF16)
    gate = jax.nn.sigmoid(jnp.dot(xn, wg_ref[...], preferred_element_type=F32))
    ple = jnp.dot(p_ref[...].astype(BF16), wp_ref[...], preferred_element_type=F32)
    h = h + gate * ple
    o_ref[...] = _rms_scale(h) * gf_ref[...] if final_norm else h


def _final(h1, moe, p2, g_ple, w_ple_gate, w_ple, g_final, *, final_norm, tm=512):
    n, d = h1.shape
    return pl.pallas_call(
        functools.partial(_final_body, final_norm=final_norm),
        grid=(n // tm,),
        in_specs=[
            pl.BlockSpec((tm, d), lambda i: (i, 0)),
            pl.BlockSpec((tm, d), lambda i: (i, 0)),
            pl.BlockSpec((tm, PLE_DIM), lambda i: (i, 0)),
            pl.BlockSpec((1, d), lambda i: (0, 0)),
            _resident((d, d), lambda i: (0, 0)),
            _resident((PLE_DIM, d), lambda i: (0, 0)),
            pl.BlockSpec((1, d), lambda i: (0, 0)),
        ],
        out_specs=pl.BlockSpec((tm, d), lambda i: (i, 0)),
        out_shape=jax.ShapeDtypeStruct((n, d), F32),
        compiler_params=_params("arbitrary"),
        name="final",
    )(h1, moe, p2, g_ple, w_ple_gate, w_ple, g_final)


def _routing_tables(expert_ids, n_tokens):
    nk = n_tokens * TOP_K
    sub = EXPERT_ITEM // EXPERT_BLOCK
    e_flat = expert_ids.T.reshape(nk)
    onehot = (e_flat[:, None] == jnp.arange(N_EXPERTS, dtype=jnp.int32)[None, :]).astype(jnp.int32)
    csum = jnp.cumsum(onehot, axis=0)
    rank = jnp.sum(onehot * csum, axis=1) - 1
    counts = csum[-1]
    padded = (counts + EXPERT_ITEM - 1) // EXPERT_ITEM * EXPERT_ITEM
    pend = jnp.cumsum(padded)
    pstart = pend - padded
    dest = pstart[e_flat] + rank
    n_items = nk // EXPERT_ITEM + N_EXPERTS
    n_used = (pend[-1] // EXPERT_ITEM).astype(jnp.int32)
    item_start = jnp.arange(n_items, dtype=jnp.int32) * EXPERT_ITEM
    item_expert = jnp.minimum(jnp.searchsorted(pend, item_start, side='right'), N_EXPERTS - 1).astype(jnp.int32)
    used = jnp.arange(n_items) < n_used
    item_expert = jnp.where(used, item_expert, item_expert[n_used - 1])
    rows_left = counts[item_expert] - (item_start - pstart[item_expert])
    item_nvb = jnp.where(used, jnp.clip((rows_left + EXPERT_BLOCK - 1) // EXPERT_BLOCK, 0, sub), 0).astype(jnp.int32)
    pos = dest.reshape(n_tokens, TOP_K).astype(jnp.int32)
    return pos, item_expert, item_nvb, n_used.reshape(1), n_items * EXPERT_ITEM


def _layer(h, p_l, norm_mix_g, w_in, b_in, conv_w, conv_b, w_rg_a, b_rg_a, w_rg_x, b_rg_x, lru_lambda,
           attn_sinks, w_attn_proj, w_lru_proj, w_out, norm_ffn_g, w_router, b_router, w_mlp1, b_mlp1,
           w_mlp2, b_mlp2, norm_ple_g, w_ple, w_ple_gate, norm_final_g, *, batch, seq, final_norm):
    n = h.shape[0]
    row = lambda v: v.reshape(1, -1)

    kv0 = Q_WIDTH
    rest0 = Q_WIDTH + 2 * KV_WIDTH
    reorder = lambda a: jnp.concatenate([a[..., :kv0], a[..., rest0:], a[..., kv0:rest0]], axis=-1)
    z = _inproj(h, row(norm_mix_g), reorder(w_in).astype(BF16), row(reorder(b_in)))

    kv_col = (Q_WIDTH + 2 * LRU_WIDTH + 2 * D_MODEL) // (2 * KV_WIDTH)
    y_attn = _attention(z, attn_sinks, batch=batch, seq=seq, q_col=0, kv_col=kv_col)

    w_rg = jnp.concatenate([w_rg_a, w_rg_x], axis=-1).astype(BF16)
    b_rg = jnp.concatenate([b_rg_a, b_rg_x], axis=-1).reshape(LRU_BLOCKS, 1, 2 * LRU_BLOCK_WIDTH)
    y_lru = _rglru(z, conv_w, row(conv_b), w_rg, b_rg, row(lru_lambda), batch=batch, seq=seq,
                   u_col=Q_WIDTH // LRU_BLOCK_WIDTH, gate_col=(Q_WIDTH + LRU_WIDTH) // LRU_BLOCK_WIDTH)

    merged = _merge(y_attn, y_lru, z, w_attn_proj.astype(BF16), w_lru_proj.astype(BF16),
                    ga_col=(Q_WIDTH + 2 * LRU_WIDTH) // D_MODEL, gl_col=(Q_WIDTH + 2 * LRU_WIDTH + D_MODEL) // D_MODEL)
    h1, xn2, expert_ids, expert_w = _outproj_router(
        h, merged, w_out.astype(BF16), row(norm_ffn_g), w_router.T.astype(BF16), b_router.reshape(N_EXPERTS, 1))

    pos, item_expert, item_nvb, n_used, n_sorted = _routing_tables(expert_ids, n)
    x_sorted = _dispatch(xn2, pos, n_sorted)
    y_sorted = _moe_mlp(x_sorted, w_mlp1, b_mlp1[:, None, :], w_mlp2, b_mlp2[:, None, :], item_expert, item_nvb, n_used)
    moe = _combine(y_sorted, pos, expert_w.T)

    return _final(h1, moe, p_l, row(norm_ple_g), w_ple_gate.astype(BF16), w_ple.astype(BF16), row(norm_final_g),
                  final_norm=final_norm)


def kernel(x, p, norm_mix_g, w_in, b_in, conv_w, conv_b, w_rg_a, b_rg_a, w_rg_x, b_rg_x, lru_lambda, attn_sinks, w_attn_proj, w_lru_proj, w_out, norm_ffn_g, w_router, b_router, w_mlp1, b_mlp1, w_mlp2, b_mlp2, norm_ple_g, w_ple, w_ple_gate, norm_final_g):
    batch, seq, d = x.shape
    depth = p.shape[0]
    h = x.reshape(batch * seq, d)
    for l in range(depth):
        h = _layer(h, p[l].reshape(batch * seq, -1), norm_mix_g[l], w_in[l], b_in[l], conv_w[l], conv_b[l],
                   w_rg_a[l], b_rg_a[l], w_rg_x[l], b_rg_x[l], lru_lambda[l], attn_sinks[l], w_attn_proj[l],
                   w_lru_proj[l], w_out[l], norm_ffn_g[l], w_router[l], b_router[l], w_mlp1[l], b_mlp1[l],
                   w_mlp2[l], b_mlp2[l], norm_ple_g[l], w_ple[l], w_ple_gate[l], norm_final_g,
                   batch=batch, seq=seq, final_norm=(l == depth - 1))
    return h.reshape(batch, seq, d)
```

```python
import functools

import jax
import jax.numpy as jnp
from jax import lax
from jax.experimental import pallas as pl
from jax.experimental.pallas import tpu as pltpu

D_MODEL = 2048
N_Q_HEADS = 32
N_KV_HEADS = 4
HEAD_DIM = 64
Q_WIDTH = N_Q_HEADS * HEAD_DIM
KV_WIDTH = N_KV_HEADS * HEAD_DIM
WINDOW = 128
LRU_WIDTH = D_MODEL
LRU_BLOCKS = 8
LRU_BLOCK_WIDTH = LRU_WIDTH // LRU_BLOCKS
CONV_WIDTH = 4
LRU_C = 8.0
IN_WIDTH = Q_WIDTH + 2 * KV_WIDTH + 2 * LRU_WIDTH + 2 * D_MODEL
N_EXPERTS = 32
TOP_K = 4
D_EXPERT = D_MODEL
SWIGLU_LIMIT = 7.0
SWIGLU_ALPHA = 1.702
EXPERT_BLOCK = 256
EXPERT_ITEM = 1280
FF_CHUNK = 512
PLE_DIM = 256
RMS_EPS = 1e-6
LOG2_E = 1.4426950408889634

VMEM_LIMIT_BYTES = 56 * 1024 * 1024
MOE_VMEM_LIMIT_BYTES = 60 * 1024 * 1024
BF16_SUBLANES = 16

F32 = jnp.float32
BF16 = jnp.bfloat16


def _params(*sem):
    return pltpu.CompilerParams(dimension_semantics=sem, vmem_limit_bytes=VMEM_LIMIT_BYTES)


def _resident(shape, index_map):
    return pl.BlockSpec(shape, index_map, pipeline_mode=pl.Buffered(1))


def _rms_scale(x):
    var = jnp.mean(x * x, axis=-1, keepdims=True)
    return x * lax.rsqrt(var + RMS_EPS)


def _pack_bf16_pairs(x):
    c = x.shape[1] // 2
    xb = x.astype(BF16).astype(F32)
    lo = lax.bitcast_convert_type(xb[:, :c], jnp.uint32) >> 16
    hi = lax.bitcast_convert_type(xb[:, c:], jnp.uint32) & jnp.uint32(0xFFFF0000)
    return hi | lo


def _unpack_bf16_pairs(w):
    lo = lax.bitcast_convert_type(w << 16, F32)
    hi = lax.bitcast_convert_type(w & jnp.uint32(0xFFFF0000), F32)
    return lo, hi


def _inproj_body(x_ref, g_ref, w_ref, b_ref, o_ref, xn_ref):
    @pl.when(pl.program_id(1) == 0)
    def _():
        xn_ref[...] = (_rms_scale(x_ref[...]) * g_ref[...]).astype(BF16)

    acc = jnp.dot(xn_ref[...], w_ref[...], preferred_element_type=F32)
    o_ref[...] = (acc + b_ref[...]).astype(o_ref.dtype)


def _inproj(x2, g, w, b, *, tm=1024, tn=1536):
    n, d = x2.shape
    width = w.shape[1]
    return pl.pallas_call(
        _inproj_body,
        grid=(n // tm, width // tn),
        in_specs=[
            pl.BlockSpec((tm, d), lambda i, j: (i, 0)),
            pl.BlockSpec((1, d), lambda i, j: (0, 0)),
            pl.BlockSpec((d, tn), lambda i, j: (0, j)),
            pl.BlockSpec((1, tn), lambda i, j: (0, j)),
        ],
        out_specs=pl.BlockSpec((tm, tn), lambda i, j: (i, j)),
        out_shape=jax.ShapeDtypeStruct((n, width), BF16),
        scratch_shapes=[pltpu.VMEM((tm, d), BF16)],
        compiler_params=_params("arbitrary", "arbitrary"),
        name="inproj",
    )(x2, g, w, b)


def _block_diag_pair(pair, odd, lo_mask):
    swapped = jnp.concatenate([pair[:, HEAD_DIM:], pair[:, :HEAD_DIM]], axis=1)
    zero = jnp.zeros_like(pair)
    if odd:
        lo = jnp.where(lo_mask, swapped, zero)
        hi = jnp.where(lo_mask, zero, pair)
    else:
        lo = jnp.where(lo_mask, pair, zero)
        hi = jnp.where(lo_mask, zero, swapped)
    return jnp.concatenate([lo, hi], axis=0)


def _attn_body(sinks_ref, q_ref, kvp_ref, kvc_ref, o_ref, *, tq):
    nblk = tq // WINDOW
    group = N_Q_HEADS // N_KV_HEADS
    pairs = group // 2
    rows = pairs * WINDOW
    band_w = 2 * WINDOW
    s_idx = pl.program_id(1)

    lo_mask = lax.broadcasted_iota(jnp.int32, (1, 2 * HEAD_DIM), 1) < HEAD_DIM
    qi = lax.broadcasted_iota(jnp.int32, (rows, band_w), 0) % WINDOW
    kj = lax.broadcasted_iota(jnp.int32, (rows, band_w), 1)
    in_band = (kj > qi) & (kj <= qi + WINDOW)
    first_key = jnp.where(s_idx > 0, 0, WINDOW)
    row_pair = lax.broadcasted_iota(jnp.int32, (rows, 1), 0) // WINDOW
    neg = jnp.finfo(F32).min

    for i in range(nblk):
        if i == 0:
            band = jnp.concatenate([kvp_ref[...], kvc_ref[0:WINDOW, :]], axis=0)
            mask = in_band & (kj >= first_key)
        else:
            band = kvc_ref[(i - 1) * WINDOW:(i + 1) * WINDOW, :]
            mask = in_band
        for kh in range(N_KV_HEADS):
            col = (kh // 2) * 2 * HEAD_DIM
            kk = _block_diag_pair(band[:, col:col + 2 * HEAD_DIM], kh % 2 == 1, lo_mask)
            vv = _block_diag_pair(band[:, KV_WIDTH + col:KV_WIDTH + col + 2 * HEAD_DIM], kh % 2 == 1, lo_mask)
            qbase = kh * group * HEAD_DIM
            q4 = jnp.concatenate(
                [q_ref[i * WINDOW:(i + 1) * WINDOW, qbase + 2 * HEAD_DIM * j:qbase + 2 * HEAD_DIM * (j + 1)]
                 for j in range(pairs)], axis=0)
            s = lax.dot_general(q4, kk, (((1,), (1,)), ((), ())), preferred_element_type=F32)
            s = s * (HEAD_DIM ** -0.5 * LOG2_E)
            sink_lo = jnp.zeros((rows, 1), F32)
            sink_hi = jnp.zeros((rows, 1), F32)
            for j in range(pairs):
                sink_lo = jnp.where(row_pair == j, sinks_ref[kh * group + 2 * j] * LOG2_E, sink_lo)
                sink_hi = jnp.where(row_pair == j, sinks_ref[kh * group + 2 * j + 1] * LOG2_E, sink_hi)
            halves = []
            inv = []
            for half, sink in ((0, sink_lo), (1, sink_hi)):
                sh = jnp.where(mask, s[:, half * band_w:(half + 1) * band_w], neg)
                m = jnp.maximum(jnp.max(sh, axis=1, keepdims=True), sink)
                p = jnp.exp2(sh - m)
                denom = jnp.sum(p, axis=1, keepdims=True) + jnp.exp2(sink - m)
                halves.append(p.astype(BF16))
                inv.append(1.0 / denom)
            pcat = jnp.concatenate(halves, axis=1)
            o = jnp.dot(pcat, vv, preferred_element_type=F32)
            o = o * jnp.where(lo_mask, inv[0], inv[1])
            for j in range(pairs):
                o_ref[i * WINDOW:(i + 1) * WINDOW, qbase + 2 * HEAD_DIM * j:qbase + 2 * HEAD_DIM * (j + 1)] = (
                    o[j * WINDOW:(j + 1) * WINDOW, :].astype(o_ref.dtype))


def _attention(z, sinks, *, batch, seq, q_col, kv_col, tq=512):
    n = z.shape[0]
    spb = seq // tq
    wpb = seq // WINDOW
    kvw = 2 * KV_WIDTH

    def prev_map(b, s, sinks_ref):
        return (jnp.maximum(b * wpb + s * (tq // WINDOW) - 1, 0), kv_col)

    return pl.pallas_call(
        functools.partial(_attn_body, tq=tq),
        grid_spec=pltpu.PrefetchScalarGridSpec(
            num_scalar_prefetch=1,
            grid=(batch, spb),
            in_specs=[
                pl.BlockSpec((tq, Q_WIDTH), lambda b, s, sinks_ref: (b * spb + s, q_col)),
                pl.BlockSpec((WINDOW, kvw), prev_map),
                pl.BlockSpec((tq, kvw), lambda b, s, sinks_ref: (b * spb + s, kv_col)),
            ],
            out_specs=pl.BlockSpec((tq, Q_WIDTH), lambda b, s, sinks_ref: (b * spb + s, 0)),
        ),
        out_shape=jax.ShapeDtypeStruct((n, Q_WIDTH), BF16),
        compiler_params=_params("arbitrary", "arbitrary"),
        name="attn",
    )(sinks, z, z, z)


def _gelu_tanh(x):
    c = (2.0 / jnp.pi) ** 0.5
    return x * (0.5 * (1.0 + jnp.tanh(c * (x + 0.044715 * (x * x * x)))))


def _lru_body(u_ref, up_ref, gate_ref, cw_ref, cb_ref, wrg_ref, brg_ref, lam_ref, y_ref,
              ext_ref, a_ref, b_ref, h_ref):
    s_idx = pl.program_id(2)
    tl, w = u_ref.shape
    pad = 8

    @pl.when(s_idx == 0)
    def _():
        ext_ref[0:pad, :] = jnp.zeros((pad, w), F32)
        h_ref[...] = jnp.zeros_like(h_ref)

    @pl.when(s_idx > 0)
    def _():
        ext_ref[0:pad, :] = up_ref[...].astype(F32)[BF16_SUBLANES - pad:BF16_SUBLANES, :]

    ext_ref[pad:pad + tl, :] = u_ref[...].astype(F32)
    first = pad - (CONV_WIDTH - 1)
    uc = cb_ref[...] + ext_ref[first:first + tl, :] * cw_ref[0:1, :]
    for tap in range(1, CONV_WIDTH):
        uc = uc + ext_ref[first + tap:first + tap + tl, :] * cw_ref[tap:tap + 1, :]

    gates = jnp.dot(uc.astype(BF16), wrg_ref[...], preferred_element_type=F32) + brg_ref[...]
    r = jax.nn.sigmoid(gates[:, :w])
    ig = jax.nn.sigmoid(gates[:, w:])
    neg_lam = -lam_ref[...]
    softplus = jnp.maximum(neg_lam, 0.0) + jnp.log1p(jnp.exp(-jnp.abs(neg_lam)))
    log_a = (-LRU_C * r) * softplus
    a = jnp.exp(log_a)
    m2 = -jnp.tanh(log_a) * (a * a + 1.0)
    mult = jnp.where(m2 > 0.0, m2 * lax.rsqrt(m2), 0.0)
    row = lax.broadcasted_iota(jnp.int32, (tl, 1), 0)
    mult = jnp.where((row == 0) & (s_idx == 0), 1.0, mult)
    b = mult * (ig * uc)

    sub = row % 8
    for d in (1, 2, 4):
        keep = sub >= d
        b_prev = jnp.where(keep, pltpu.roll(b, d, 0), 0.0)
        a_prev = jnp.where(keep, pltpu.roll(a, d, 0), 1.0)
        b = b + a * b_prev
        a = a * a_prev
    a_ref[...] = a
    b_ref[...] = b

    def group_step(gi, h):
        off = pl.multiple_of(gi * 8, 8)
        hg = b_ref[pl.ds(off, 8), :] + a_ref[pl.ds(off, 8), :] * h
        b_ref[pl.ds(off, 8), :] = hg
        return hg[7:8, :]

    h_ref[...] = lax.fori_loop(0, tl // 8, group_step, h_ref[...])
    y_ref[...] = (b_ref[...] * _gelu_tanh(gate_ref[...].astype(F32))).astype(y_ref.dtype)


def _rglru(z, conv_w, conv_b, w_rg, b_rg, lam, *, batch, seq, u_col, gate_col, tl=512):
    n = z.shape[0]
    w = LRU_BLOCK_WIDTH
    spb = seq // tl
    rpb = seq // BF16_SUBLANES

    def prev_map(b, c, s):
        return (jnp.maximum(b * rpb + s * (tl // BF16_SUBLANES) - 1, 0), u_col + c)

    return pl.pallas_call(
        _lru_body,
        grid=(batch, LRU_BLOCKS, spb),
        in_specs=[
            pl.BlockSpec((tl, w), lambda b, c, s: (b * spb + s, u_col + c)),
            pl.BlockSpec((BF16_SUBLANES, w), prev_map),
            pl.BlockSpec((tl, w), lambda b, c, s: (b * spb + s, gate_col + c)),
            pl.BlockSpec((CONV_WIDTH, w), lambda b, c, s: (0, c)),
            pl.BlockSpec((1, w), lambda b, c, s: (0, c)),
            pl.BlockSpec((None, w, 2 * w), lambda b, c, s: (c, 0, 0)),
            pl.BlockSpec((None, 1, 2 * w), lambda b, c, s: (c, 0, 0)),
            pl.BlockSpec((1, w), lambda b, c, s: (0, c)),
        ],
        out_specs=pl.BlockSpec((tl, w), lambda b, c, s: (b * spb + s, c)),
        out_shape=jax.ShapeDtypeStruct((n, LRU_WIDTH), BF16),
        scratch_shapes=[
            pltpu.VMEM((tl + 8, w), F32),
            pltpu.VMEM((tl, w), F32),
            pltpu.VMEM((tl, w), F32),
            pltpu.VMEM((1, w), F32),
        ],
        compiler_params=_params("arbitrary", "arbitrary", "arbitrary"),
        name="rglru",
    )(z, z, z, conv_w, conv_b, w_rg, b_rg, lam)


def _merge_body(ya_ref, yl_ref, ga_ref, gl_ref, pa_ref, pl_ref, o_ref):
    pa = jnp.dot(ya_ref[...], pa_ref[...], preferred_element_type=F32)
    pr = jnp.dot(yl_ref[...], pl_ref[...], preferred_element_type=F32)
    merged = jax.nn.sigmoid(ga_ref[...].astype(F32)) * pa + jax.nn.sigmoid(gl_ref[...].astype(F32)) * pr
    o_ref[...] = merged.astype(o_ref.dtype)


def _merge(y_attn, y_lru, z, w_attn_proj, w_lru_proj, *, ga_col, gl_col, tm=512):
    n, d = y_attn.shape
    return pl.pallas_call(
        _merge_body,
        grid=(n // tm,),
        in_specs=[
            pl.BlockSpec((tm, d), lambda i: (i, 0)),
            pl.BlockSpec((tm, d), lambda i: (i, 0)),
            pl.BlockSpec((tm, d), lambda i: (i, ga_col)),
            pl.BlockSpec((tm, d), lambda i: (i, gl_col)),
            _resident((d, d), lambda i: (0, 0)),
            _resident((d, d), lambda i: (0, 0)),
        ],
        out_specs=pl.BlockSpec((tm, d), lambda i: (i, 0)),
        out_shape=jax.ShapeDtypeStruct((n, d), BF16),
        compiler_params=_params("arbitrary"),
        name="merge",
    )(y_attn, y_lru, z, z, w_attn_proj, w_lru_proj)


def _outproj_body(x_ref, m_ref, wo_ref, g_ref, wr_ref, br_ref, tri_ref, h_ref, xn_ref, idx_ref, wt_ref, rank_ref,
                  cnt_ref):
    @pl.when(pl.program_id(0) == 0)
    def _():
        cnt_ref[...] = jnp.zeros_like(cnt_ref)

    h = x_ref[...] + jnp.dot(m_ref[...], wo_ref[...], preferred_element_type=F32)
    h_ref[...] = h
    xn = _rms_scale(h) * g_ref[...]
    xn_ref[...] = _pack_bf16_pairs(xn)
    logits = lax.dot_general(wr_ref[...], xn.astype(BF16), (((1,), (1,)), ((), ())),
                             preferred_element_type=F32) + br_ref[...]
    expert = lax.broadcasted_iota(jnp.int32, logits.shape, 0).astype(F32)
    vals, sels = [], []
    for k in range(TOP_K):
        m = jnp.max(logits, axis=0, keepdims=True)
        sel = jnp.min(jnp.where(logits == m, expert, float(N_EXPERTS)), axis=0, keepdims=True)
        vals.append(m)
        sels.append(sel)
        idx_ref[k:k + 1, :] = sel.astype(jnp.int32)
        logits = jnp.where(expert == sel, -jnp.inf, logits)
    exps = [jnp.exp(v - vals[0]) for v in vals]
    denom = exps[0] + exps[1] + exps[2] + exps[3]
    for k in range(TOP_K):
        wt_ref[k:k + 1, :] = exps[k] / denom

    chosen = jnp.zeros(logits.shape, F32)
    for sel in sels:
        chosen = chosen + jnp.where(expert == sel, 1.0, 0.0)
    before = jnp.dot(chosen.astype(BF16), tri_ref[...], preferred_element_type=F32) + cnt_ref[:, 0:1]
    for k in range(TOP_K):
        rank = jnp.sum(jnp.where(expert == sels[k], before, 0.0), axis=0, keepdims=True)
        rank_ref[k:k + 1, :] = rank.astype(jnp.int32)
    cnt_ref[...] = cnt_ref[...] + jnp.sum(chosen, axis=1, keepdims=True)


def _outproj_router(x2, merged, w_out, g_ffn, w_router_t, b_router, *, tm=512):
    n, d = x2.shape
    lane = 128
    earlier = (jnp.arange(tm)[:, None] < jnp.arange(tm)[None, :]).astype(BF16)
    return pl.pallas_call(
        _outproj_body,
        grid=(n // tm,),
        in_specs=[
            pl.BlockSpec((tm, d), lambda i: (i, 0)),
            pl.BlockSpec((tm, d), lambda i: (i, 0)),
            _resident((d, d), lambda i: (0, 0)),
            pl.BlockSpec((1, d), lambda i: (0, 0)),
            pl.BlockSpec((N_EXPERTS, d), lambda i: (0, 0)),
            pl.BlockSpec((N_EXPERTS, 1), lambda i: (0, 0)),
            pl.BlockSpec((tm, tm), lambda i: (0, 0)),
        ],
        out_specs=[
            pl.BlockSpec((tm, d), lambda i: (i, 0)),
            pl.BlockSpec((tm, d // 2), lambda i: (i, 0)),
            pl.BlockSpec((TOP_K, tm), lambda i: (0, i)),
            pl.BlockSpec((TOP_K, tm), lambda i: (0, i)),
            pl.BlockSpec((TOP_K, tm), lambda i: (0, i)),
            pl.BlockSpec((N_EXPERTS, lane), lambda i: (0, 0)),
        ],
        out_shape=[
            jax.ShapeDtypeStruct((n, d), F32),
            jax.ShapeDtypeStruct((n, d // 2), jnp.uint32),
            jax.ShapeDtypeStruct((TOP_K, n), jnp.int32),
            jax.ShapeDtypeStruct((TOP_K, n), F32),
            jax.ShapeDtypeStruct((TOP_K, n), jnp.int32),
            jax.ShapeDtypeStruct((N_EXPERTS, lane), F32),
        ],
        compiler_params=_params("arbitrary"),
        name="outproj_router",
    )(x2, merged, w_out, g_ffn, w_router_t, b_router, earlier)


def _row_copy(src, dst, sem, src_row, dst_row):
    return pltpu.make_async_copy(src.at[pl.ds(src_row, 1), :], dst.at[pl.ds(dst_row, 1), :], sem)


def _dispatch_body(dst_ref, x_ref, init_hbm, out_hbm, stage, sems):
    del init_hbm
    i = pl.program_id(0)
    n = pl.num_programs(0)
    c = dst_ref.shape[1]
    slot = i % 2
    stage[slot] = x_ref[...]
    for r in range(c):
        _row_copy(stage.at[slot], out_hbm, sems.at[slot], r // TOP_K, dst_ref[0, r]).start()

    @pl.when(i > 0)
    def _():
        for r in range(c):
            _row_copy(stage.at[1 - slot], out_hbm, sems.at[1 - slot], 0, 0).wait()

    @pl.when(i == n - 1)
    def _():
        for r in range(c):
            _row_copy(stage.at[slot], out_hbm, sems.at[slot], 0, 0).wait()


def _dispatch(rows, pos, n_sorted, *, tm=128):
    n, w = rows.shape
    dst3 = pos.reshape(n // tm, 1, tm * TOP_K)
    init = jnp.zeros((n_sorted, w), rows.dtype)
    return pl.pallas_call(
        _dispatch_body,
        grid=(n // tm,),
        in_specs=[
            pl.BlockSpec((None, 1, tm * TOP_K), lambda i: (i, 0, 0), memory_space=pltpu.SMEM),
            pl.BlockSpec((tm, w), lambda i: (i, 0)),
            pl.BlockSpec(memory_space=pl.ANY),
        ],
        out_specs=pl.BlockSpec(memory_space=pl.ANY),
        out_shape=jax.ShapeDtypeStruct((n_sorted, w), rows.dtype),
        scratch_shapes=[pltpu.VMEM((2, tm, w), rows.dtype), pltpu.SemaphoreType.DMA((2,))],
        input_output_aliases={2: 0},
        compiler_params=_params("arbitrary"),
        name="dispatch",
    )(dst3, rows, init)


def _combine_body(idx_ref, nxt_ref, wt_ref, src_hbm, o_ref, buf, sems):
    i = pl.program_id(0)
    n = pl.num_programs(0)
    tm = o_ref.shape[0]
    count = TOP_K * tm
    slot = i % 2

    @pl.when(i == 0)
    def _():
        for r in range(count):
            _row_copy(src_hbm, buf.at[0], sems.at[0], idx_ref[0, r], r).start()

    @pl.when(i + 1 < n)
    def _():
        for r in range(count):
            _row_copy(src_hbm, buf.at[1 - slot], sems.at[1 - slot], nxt_ref[0, r], r).start()

    for r in range(count):
        _row_copy(src_hbm, buf.at[slot], sems.at[slot], 0, r).wait()

    c = o_ref.shape[1] // 2
    acc_lo = jnp.zeros((tm, c), F32)
    acc_hi = jnp.zeros((tm, c), F32)
    for k in range(TOP_K):
        lo, hi = _unpack_bf16_pairs(buf[slot, k * tm:(k + 1) * tm, :])
        acc_lo = acc_lo + wt_ref[:, k:k + 1] * lo
        acc_hi = acc_hi + wt_ref[:, k:k + 1] * hi
    o_ref[:, :c] = acc_lo
    o_ref[:, c:] = acc_hi


def _combine(y_sorted, pos, weights, *, tm=128):
    n = pos.shape[0]
    w = y_sorted.shape[1]
    steps = n // tm
    idx3 = pos.reshape(steps, tm, TOP_K).transpose(0, 2, 1).reshape(steps, 1, TOP_K * tm)
    return pl.pallas_call(
        _combine_body,
        grid=(steps,),
        in_specs=[
            pl.BlockSpec((None, 1, TOP_K * tm), lambda i: (i, 0, 0), memory_space=pltpu.SMEM),
            pl.BlockSpec((None, 1, TOP_K * tm), lambda i: (jnp.minimum(i + 1, steps - 1), 0, 0),
                         memory_space=pltpu.SMEM),
            pl.BlockSpec((tm, TOP_K), lambda i: (i, 0)),
            pl.BlockSpec(memory_space=pl.ANY),
        ],
        out_specs=pl.BlockSpec((tm, 2 * w), lambda i: (i, 0)),
        out_shape=jax.ShapeDtypeStruct((n, 2 * w), F32),
        scratch_shapes=[pltpu.VMEM((2, TOP_K * tm, w), y_sorted.dtype), pltpu.SemaphoreType.DMA((2,))],
        compiler_params=_params("arbitrary"),
        name="combine",
    )(idx3, idx3, weights, y_sorted)


def _moe_mlp_body(ie_ref, nvb_ref, nu_ref, x_ref, w1_hbm, w2_hbm, bg_ref, bl_ref, b2_ref, o_ref,
                  acc_ref, sg, sl, s2, wg_ref, wl_ref, w2_ref, sems):
    item = pl.program_id(0)
    chunk = pl.program_id(1)
    n_items = pl.num_programs(0)
    n_chunks = pl.num_programs(1)
    nvb = nvb_ref[item]
    f = w2_hbm.shape[1]
    fc = sg.shape[1]

    def weight_copies(it, ch):
        e = ie_ref[it]
        c0 = pl.multiple_of(ch * fc, fc)
        return (pltpu.make_async_copy(w1_hbm.at[e, :, pl.ds(c0, fc)], sg, sems.at[0]),
                pltpu.make_async_copy(w1_hbm.at[e, :, pl.ds(f + c0, fc)], sl, sems.at[1]),
                pltpu.make_async_copy(w2_hbm.at[e, pl.ds(c0, fc), :], s2, sems.at[2]))

    @pl.when(chunk == 0)
    def _():
        acc_ref[...] = jnp.broadcast_to(b2_ref[...], acc_ref.shape)

    @pl.when((item == 0) & (chunk == 0))
    def _():
        for cp in weight_copies(item, chunk):
            cp.start()

    @pl.when(nvb > 0)
    def _():
        for cp in weight_copies(item, chunk):
            cp.wait()
        wg_ref[...] = sg[...].astype(BF16)
        wl_ref[...] = sl[...].astype(BF16)
        w2_ref[...] = s2[...].astype(BF16)

        same_item = chunk + 1 < n_chunks
        nxt_item = jnp.minimum(jnp.where(same_item, item, item + 1), n_items - 1)
        nxt_chunk = jnp.where(same_item, chunk + 1, 0)

        @pl.when(same_item | ((item + 1 < n_items) & (nvb_ref[nxt_item] > 0)))
        def _():
            for cp in weight_copies(nxt_item, nxt_chunk):
                cp.start()

        def run_rows(first_block, n_blocks):
            rows = slice(first_block * EXPERT_BLOCK, (first_block + n_blocks) * EXPERT_BLOCK)
            lo, hi = _unpack_bf16_pairs(x_ref[rows, :])
            xj = jnp.concatenate([lo.astype(BF16), hi.astype(BF16)], axis=1)
            glu = jnp.dot(xj, wg_ref[...], preferred_element_type=F32) + bg_ref[...]
            lin = jnp.dot(xj, wl_ref[...], preferred_element_type=F32) + bl_ref[...]
            glu = jnp.minimum(glu, SWIGLU_LIMIT)
            lin = jnp.clip(lin, -SWIGLU_LIMIT, SWIGLU_LIMIT)
            act = glu * jax.nn.sigmoid(SWIGLU_ALPHA * glu) * (lin + 1.0)
            acc_ref[rows, :] += jnp.dot(act.astype(BF16), w2_ref[...], preferred_element_type=F32)

        n_sub = x_ref.shape[0] // EXPERT_BLOCK
        for j in range(0, n_sub, 2):
            if j + 2 <= n_sub:
                pl.when(nvb >= j + 2)(functools.partial(run_rows, j, 2))
                pl.when(nvb == j + 1)(functools.partial(run_rows, j, 1))
            else:
                pl.when(nvb >= j + 1)(functools.partial(run_rows, j, 1))

    @pl.when(chunk == n_chunks - 1)
    def _():
        o_ref[...] = _pack_bf16_pairs(acc_ref[...])


def _moe_mlp(x_sorted, w1, b1, w2, b2, item_expert, item_nvb, n_used):
    n_rows, half = x_sorted.shape
    d = 2 * half
    n_items = n_rows // EXPERT_ITEM
    f = w2.shape[1]
    nc = f // FF_CHUNK

    def live_chunk(i, c, nu):
        return jnp.where(i < nu[0], c, nc - 1)

    def rows_map(i, c, ie, nvb, nu):
        return (jnp.minimum(i, nu[0] - 1), 0)

    return pl.pallas_call(
        _moe_mlp_body,
        grid_spec=pltpu.PrefetchScalarGridSpec(
            num_scalar_prefetch=3,
            grid=(n_items, nc),
            in_specs=[
                pl.BlockSpec((EXPERT_ITEM, half), rows_map),
                pl.BlockSpec(memory_space=pl.ANY),
                pl.BlockSpec(memory_space=pl.ANY),
                pl.BlockSpec((None, 1, FF_CHUNK), lambda i, c, ie, nvb, nu: (ie[i], 0, live_chunk(i, c, nu))),
                pl.BlockSpec((None, 1, FF_CHUNK), lambda i, c, ie, nvb, nu: (ie[i], 0, nc + live_chunk(i, c, nu))),
                pl.BlockSpec((None, 1, d), lambda i, c, ie, nvb, nu: (ie[i], 0, 0)),
            ],
            out_specs=pl.BlockSpec((EXPERT_ITEM, half), lambda i, c, ie, nvb, nu: (i, 0)),
            scratch_shapes=[
                pltpu.VMEM((EXPERT_ITEM, d), F32),
                pltpu.VMEM((d, FF_CHUNK), F32),
                pltpu.VMEM((d, FF_CHUNK), F32),
                pltpu.VMEM((FF_CHUNK, d), F32),
                pltpu.VMEM((d, FF_CHUNK), BF16),
                pltpu.VMEM((d, FF_CHUNK), BF16),
                pltpu.VMEM((FF_CHUNK, d), BF16),
                pltpu.SemaphoreType.DMA((3,)),
            ],
        ),
        out_shape=jax.ShapeDtypeStruct((n_rows, half), jnp.uint32),
        compiler_params=pltpu.CompilerParams(dimension_semantics=("arbitrary", "arbitrary"),
                                             vmem_limit_bytes=MOE_VMEM_LIMIT_BYTES),
        name="moe_mlp",
    )(item_expert, item_nvb, n_used, x_sorted, w1, w2, b1, b1, b2)


def _final_body(h_ref, moe_ref, p_ref, gp_ref, wg_ref, wp_ref, gf_ref, o_ref, *, final_norm):
    h = h_ref[...] + moe_ref[...]
    xn = (_rms_scale(h) * gp_ref[...]).astype(BF16)
    gate = jax.nn.sigmoid(jnp.dot(xn, wg_ref[...], preferred_element_type=F32))
    ple = jnp.dot(p_ref[...].astype(BF16), wp_ref[...], preferred_element_type=F32)
    h = h + gate * ple
    o_ref[...] = _rms_scale(h) * gf_ref[...] if final_norm else h


def _final(h1, moe, p2, g_ple, w_ple_gate, w_ple, g_final, *, final_norm, tm=512):
    n, d = h1.shape
    return pl.pallas_call(
        functools.partial(_final_body, final_norm=final_norm),
        grid=(n // tm,),
        in_specs=[
            pl.BlockSpec((tm, d), lambda i: (i, 0)),
            pl.BlockSpec((tm, d), lambda i: (i, 0)),
            pl.BlockSpec((tm, PLE_DIM), lambda i: (i, 0)),
            pl.BlockSpec((1, d), lambda i: (0, 0)),
            _resident((d, d), lambda i: (0, 0)),
            _resident((PLE_DIM, d), lambda i: (0, 0)),
            pl.BlockSpec((1, d), lambda i: (0, 0)),
        ],
        out_specs=pl.BlockSpec((tm, d), lambda i: (i, 0)),
        out_shape=jax.ShapeDtypeStruct((n, d), F32),
        compiler_params=_params("arbitrary"),
        name="final",
    )(h1, moe, p2, g_ple, w_ple_gate, w_ple, g_final)


def _routing_tables(expert_ids, rank, counts, n_tokens):
    nk = n_tokens * TOP_K
    sub = EXPERT_ITEM // EXPERT_BLOCK
    padded = (counts + EXPERT_ITEM - 1) // EXPERT_ITEM * EXPERT_ITEM
    pend = jnp.cumsum(padded)
    pstart = pend - padded
    dest = (pstart[expert_ids] + rank).T
    n_items = nk // EXPERT_ITEM + N_EXPERTS
    n_used = (pend[-1] // EXPERT_ITEM).astype(jnp.int32)
    item_start = jnp.arange(n_items, dtype=jnp.int32) * EXPERT_ITEM
    item_expert = jnp.minimum(jnp.searchsorted(pend, item_start, side='right'), N_EXPERTS - 1).astype(jnp.int32)
    used = jnp.arange(n_items) < n_used
    item_expert = jnp.where(used, item_expert, item_expert[n_used - 1])
    rows_left = counts[item_expert] - (item_start - pstart[item_expert])
    item_nvb = jnp.where(used, jnp.clip((rows_left + EXPERT_BLOCK - 1) // EXPERT_BLOCK, 0, sub), 0).astype(jnp.int32)
    pos = dest.reshape(n_tokens, TOP_K).astype(jnp.int32)
    return pos, item_expert, item_nvb, n_used.reshape(1), n_items * EXPERT_ITEM


def _layer(h, p_l, norm_mix_g, w_in, b_in, conv_w, conv_b, w_rg_a, b_rg_a, w_rg_x, b_rg_x, lru_lambda,
           attn_sinks, w_attn_proj, w_lru_proj, w_out, norm_ffn_g, w_router, b_router, w_mlp1, b_mlp1,
           w_mlp2, b_mlp2, norm_ple_g, w_ple, w_ple_gate, norm_final_g, *, batch, seq, final_norm):
    n = h.shape[0]
    row = lambda v: v.reshape(1, -1)

    kv0 = Q_WIDTH
    rest0 = Q_WIDTH + 2 * KV_WIDTH
    reorder = lambda a: jnp.concatenate([a[..., :kv0], a[..., rest0:], a[..., kv0:rest0]], axis=-1)
    z = _inproj(h, row(norm_mix_g), reorder(w_in).astype(BF16), row(reorder(b_in)))

    kv_col = (Q_WIDTH + 2 * LRU_WIDTH + 2 * D_MODEL) // (2 * KV_WIDTH)
    y_attn = _attention(z, attn_sinks, batch=batch, seq=seq, q_col=0, kv_col=kv_col)

    w_rg = jnp.concatenate([w_rg_a, w_rg_x], axis=-1).astype(BF16)
    b_rg = jnp.concatenate([b_rg_a, b_rg_x], axis=-1).reshape(LRU_BLOCKS, 1, 2 * LRU_BLOCK_WIDTH)
    y_lru = _rglru(z, conv_w, row(conv_b), w_rg, b_rg, row(lru_lambda), batch=batch, seq=seq,
                   u_col=Q_WIDTH // LRU_BLOCK_WIDTH, gate_col=(Q_WIDTH + LRU_WIDTH) // LRU_BLOCK_WIDTH)

    merged = _merge(y_attn, y_lru, z, w_attn_proj.astype(BF16), w_lru_proj.astype(BF16),
                    ga_col=(Q_WIDTH + 2 * LRU_WIDTH) // D_MODEL, gl_col=(Q_WIDTH + 2 * LRU_WIDTH + D_MODEL) // D_MODEL)
    h1, xn2, expert_ids, expert_w, rank, counts = _outproj_router(
        h, merged, w_out.astype(BF16), row(norm_ffn_g), w_router.T.astype(BF16), b_router.reshape(N_EXPERTS, 1))

    pos, item_expert, item_nvb, n_used, n_sorted = _routing_tables(
        expert_ids, rank, counts[:, 0].astype(jnp.int32), n)
    x_sorted = _dispatch(xn2, pos, n_sorted)
    y_sorted = _moe_mlp(x_sorted, w_mlp1, b_mlp1[:, None, :], w_mlp2, b_mlp2[:, None, :], item_expert, item_nvb, n_used)
    moe = _combine(y_sorted, pos, expert_w.T)

    return _final(h1, moe, p_l, row(norm_ple_g), w_ple_gate.astype(BF16), w_ple.astype(BF16), row(norm_final_g),
                  final_norm=final_norm)


def kernel(x, p, norm_mix_g, w_in, b_in, conv_w, conv_b, w_rg_a, b_rg_a, w_rg_x, b_rg_x, lru_lambda, attn_sinks, w_attn_proj, w_lru_proj, w_out, norm_ffn_g, w_router, b_router, w_mlp1, b_mlp1, w_mlp2, b_mlp2, norm_ple_g, w_ple, w_ple_gate, norm_final_g):
    batch, seq, d = x.shape
    depth = p.shape[0]
    h = x.reshape(batch * seq, d)
    for l in range(depth):
        h = _layer(h, p[l].reshape(batch * seq, -1), norm_mix_g[l], w_in[l], b_in[l], conv_w[l], conv_b[l],
                   w_rg_a[l], b_rg_a[l], w_rg_x[l], b_rg_x[l], lru_lambda[l], attn_sinks[l], w_attn_proj[l],
                   w_lru_proj[l], w_out[l], norm_ffn_g[l], w_router[l], b_router[l], w_mlp1[l], b_mlp1[l],
                   w_mlp2[l], b_mlp2[l], norm_ple_g[l], w_ple[l], w_ple_gate[l], norm_final_g,
                   batch=batch, seq=seq, final_norm=(l == depth - 1))
    return h.reshape(batch, seq, d)
```

```python
import functools

import jax
import jax.numpy as jnp
from jax import lax
from jax.experimental import pallas as pl
from jax.experimental.pallas import tpu as pltpu

D_MODEL = 2048
N_Q_HEADS = 32
N_KV_HEADS = 4
HEAD_DIM = 64
Q_WIDTH = N_Q_HEADS * HEAD_DIM
KV_WIDTH = N_KV_HEADS * HEAD_DIM
WINDOW = 128
LRU_WIDTH = D_MODEL
LRU_BLOCKS = 8
LRU_BLOCK_WIDTH = LRU_WIDTH // LRU_BLOCKS
CONV_WIDTH = 4
LRU_C = 8.0
IN_WIDTH = Q_WIDTH + 2 * KV_WIDTH + 2 * LRU_WIDTH + 2 * D_MODEL
N_EXPERTS = 32
TOP_K = 4
D_EXPERT = D_MODEL
SWIGLU_LIMIT = 7.0
SWIGLU_ALPHA = 1.702
EXPERT_BLOCK = 256
EXPERT_ITEM = 1280
FF_CHUNK = 512
PLE_DIM = 256
RMS_EPS = 1e-6
LOG2_E = 1.4426950408889634

VMEM_LIMIT_BYTES = 56 * 1024 * 1024
MOE_VMEM_LIMIT_BYTES = 60 * 1024 * 1024
BF16_SUBLANES = 16

F32 = jnp.float32
BF16 = jnp.bfloat16


def _params(*sem):
    return pltpu.CompilerParams(dimension_semantics=sem, vmem_limit_bytes=VMEM_LIMIT_BYTES)


def _resident(shape, index_map):
    return pl.BlockSpec(shape, index_map, pipeline_mode=pl.Buffered(1))


def _rms_scale(x):
    var = jnp.mean(x * x, axis=-1, keepdims=True)
    return x * lax.rsqrt(var + RMS_EPS)


def _pack_bf16_pairs(x):
    c = x.shape[1] // 2
    xb = x.astype(BF16).astype(F32)
    lo = lax.bitcast_convert_type(xb[:, :c], jnp.uint32) >> 16
    hi = lax.bitcast_convert_type(xb[:, c:], jnp.uint32) & jnp.uint32(0xFFFF0000)
    return hi | lo


def _unpack_bf16_pairs(w):
    lo = lax.bitcast_convert_type(w << 16, F32)
    hi = lax.bitcast_convert_type(w & jnp.uint32(0xFFFF0000), F32)
    return lo, hi


def _inproj_body(x_ref, g_ref, w_ref, b_ref, o_ref, xn_ref):
    @pl.when(pl.program_id(1) == 0)
    def _():
        xn_ref[...] = (_rms_scale(x_ref[...]) * g_ref[...]).astype(BF16)

    acc = jnp.dot(xn_ref[...], w_ref[...], preferred_element_type=F32)
    o_ref[...] = (acc + b_ref[...]).astype(o_ref.dtype)


def _inproj(x2, g, w, b, *, tm=1024, tn=1536):
    n, d = x2.shape
    width = w.shape[1]
    return pl.pallas_call(
        _inproj_body,
        grid=(n // tm, width // tn),
        in_specs=[
            pl.BlockSpec((tm, d), lambda i, j: (i, 0)),
            pl.BlockSpec((1, d), lambda i, j: (0, 0)),
            pl.BlockSpec((d, tn), lambda i, j: (0, j)),
            pl.BlockSpec((1, tn), lambda i, j: (0, j)),
        ],
        out_specs=pl.BlockSpec((tm, tn), lambda i, j: (i, j)),
        out_shape=jax.ShapeDtypeStruct((n, width), BF16),
        scratch_shapes=[pltpu.VMEM((tm, d), BF16)],
        compiler_params=_params("arbitrary", "arbitrary"),
        name="inproj",
    )(x2, g, w, b)


def _block_diag_pair(pair, odd, lo_mask):
    swapped = jnp.concatenate([pair[:, HEAD_DIM:], pair[:, :HEAD_DIM]], axis=1)
    zero = jnp.zeros_like(pair)
    if odd:
        lo = jnp.where(lo_mask, swapped, zero)
        hi = jnp.where(lo_mask, zero, pair)
    else:
        lo = jnp.where(lo_mask, pair, zero)
        hi = jnp.where(lo_mask, zero, swapped)
    return jnp.concatenate([lo, hi], axis=0)


def _attn_body(sinks_ref, q_ref, kvp_ref, kvc_ref, o_ref, *, tq):
    nblk = tq // WINDOW
    group = N_Q_HEADS // N_KV_HEADS
    pairs = group // 2
    rows = pairs * WINDOW
    band_w = 2 * WINDOW
    s_idx = pl.program_id(1)

    lo_mask = lax.broadcasted_iota(jnp.int32, (1, 2 * HEAD_DIM), 1) < HEAD_DIM
    qi = lax.broadcasted_iota(jnp.int32, (rows, band_w), 0) % WINDOW
    kj = lax.broadcasted_iota(jnp.int32, (rows, band_w), 1)
    in_band = (kj > qi) & (kj <= qi + WINDOW)
    first_key = jnp.where(s_idx > 0, 0, WINDOW)
    row_pair = lax.broadcasted_iota(jnp.int32, (rows, 1), 0) // WINDOW
    neg = jnp.finfo(F32).min

    for i in range(nblk):
        if i == 0:
            band = jnp.concatenate([kvp_ref[...], kvc_ref[0:WINDOW, :]], axis=0)
            mask = in_band & (kj >= first_key)
        else:
            band = kvc_ref[(i - 1) * WINDOW:(i + 1) * WINDOW, :]
            mask = in_band
        for kh in range(N_KV_HEADS):
            col = (kh // 2) * 2 * HEAD_DIM
            kk = _block_diag_pair(band[:, col:col + 2 * HEAD_DIM], kh % 2 == 1, lo_mask)
            vv = _block_diag_pair(band[:, KV_WIDTH + col:KV_WIDTH + col + 2 * HEAD_DIM], kh % 2 == 1, lo_mask)
            qbase = kh * group * HEAD_DIM
            q4 = jnp.concatenate(
                [q_ref[i * WINDOW:(i + 1) * WINDOW, qbase + 2 * HEAD_DIM * j:qbase + 2 * HEAD_DIM * (j + 1)]
                 for j in range(pairs)], axis=0)
            s = lax.dot_general(q4, kk, (((1,), (1,)), ((), ())), preferred_element_type=F32)
            s = s * (HEAD_DIM ** -0.5 * LOG2_E)
            sink_lo = jnp.zeros((rows, 1), F32)
            sink_hi = jnp.zeros((rows, 1), F32)
            for j in range(pairs):
                sink_lo = jnp.where(row_pair == j, sinks_ref[kh * group + 2 * j] * LOG2_E, sink_lo)
                sink_hi = jnp.where(row_pair == j, sinks_ref[kh * group + 2 * j + 1] * LOG2_E, sink_hi)
            halves = []
            inv = []
            for half, sink in ((0, sink_lo), (1, sink_hi)):
                sh = jnp.where(mask, s[:, half * band_w:(half + 1) * band_w], neg)
                m = jnp.maximum(jnp.max(sh, axis=1, keepdims=True), sink)
                p = jnp.exp2(sh - m)
                denom = jnp.sum(p, axis=1, keepdims=True) + jnp.exp2(sink - m)
                halves.append(p.astype(BF16))
                inv.append(1.0 / denom)
            pcat = jnp.concatenate(halves, axis=1)
            o = jnp.dot(pcat, vv, preferred_element_type=F32)
            o = o * jnp.where(lo_mask, inv[0], inv[1])
            for j in range(pairs):
                o_ref[i * WINDOW:(i + 1) * WINDOW, qbase + 2 * HEAD_DIM * j:qbase + 2 * HEAD_DIM * (j + 1)] = (
                    o[j * WINDOW:(j + 1) * WINDOW, :].astype(o_ref.dtype))


def _attention(z, sinks, *, batch, seq, q_col, kv_col, tq=512):
    n = z.shape[0]
    spb = seq // tq
    wpb = seq // WINDOW
    kvw = 2 * KV_WIDTH

    def prev_map(b, s, sinks_ref):
        return (jnp.maximum(b * wpb + s * (tq // WINDOW) - 1, 0), kv_col)

    return pl.pallas_call(
        functools.partial(_attn_body, tq=tq),
        grid_spec=pltpu.PrefetchScalarGridSpec(
            num_scalar_prefetch=1,
            grid=(batch, spb),
            in_specs=[
                pl.BlockSpec((tq, Q_WIDTH), lambda b, s, sinks_ref: (b * spb + s, q_col)),
                pl.BlockSpec((WINDOW, kvw), prev_map),
                pl.BlockSpec((tq, kvw), lambda b, s, sinks_ref: (b * spb + s, kv_col)),
            ],
            out_specs=pl.BlockSpec((tq, Q_WIDTH), lambda b, s, sinks_ref: (b * spb + s, 0)),
        ),
        out_shape=jax.ShapeDtypeStruct((n, Q_WIDTH), BF16),
        compiler_params=_params("arbitrary", "arbitrary"),
        name="attn",
    )(sinks, z, z, z)


def _gelu_tanh(x):
    c = (2.0 / jnp.pi) ** 0.5
    return x * (0.5 * (1.0 + jnp.tanh(c * (x + 0.044715 * (x * x * x)))))


def _lru_body(u_ref, up_ref, gate_ref, cw_ref, cb_ref, wrg_ref, brg_ref, lam_ref, y_ref,
              ext_ref, a_ref, b_ref, h_ref):
    s_idx = pl.program_id(2)
    tl, w = u_ref.shape
    pad = 8

    @pl.when(s_idx == 0)
    def _():
        ext_ref[0:pad, :] = jnp.zeros((pad, w), F32)
        h_ref[...] = jnp.zeros_like(h_ref)

    @pl.when(s_idx > 0)
    def _():
        ext_ref[0:pad, :] = up_ref[...].astype(F32)[BF16_SUBLANES - pad:BF16_SUBLANES, :]

    ext_ref[pad:pad + tl, :] = u_ref[...].astype(F32)
    first = pad - (CONV_WIDTH - 1)
    uc = cb_ref[...] + ext_ref[first:first + tl, :] * cw_ref[0:1, :]
    for tap in range(1, CONV_WIDTH):
        uc = uc + ext_ref[first + tap:first + tap + tl, :] * cw_ref[tap:tap + 1, :]

    gates = jnp.dot(uc.astype(BF16), wrg_ref[...], preferred_element_type=F32) + brg_ref[...]
    r = jax.nn.sigmoid(gates[:, :w])
    ig = jax.nn.sigmoid(gates[:, w:])
    neg_lam = -lam_ref[...]
    softplus = jnp.maximum(neg_lam, 0.0) + jnp.log1p(jnp.exp(-jnp.abs(neg_lam)))
    log_a = (-LRU_C * r) * softplus
    a = jnp.exp(log_a)
    m2 = -jnp.tanh(log_a) * (a * a + 1.0)
    mult = jnp.where(m2 > 0.0, m2 * lax.rsqrt(m2), 0.0)
    row = lax.broadcasted_iota(jnp.int32, (tl, 1), 0)
    mult = jnp.where((row == 0) & (s_idx == 0), 1.0, mult)
    b = mult * (ig * uc)

    sub = row % 8
    for d in (1, 2, 4):
        keep = sub >= d
        b_prev = jnp.where(keep, pltpu.roll(b, d, 0), 0.0)
        a_prev = jnp.where(keep, pltpu.roll(a, d, 0), 1.0)
        b = b + a * b_prev
        a = a * a_prev
    a_ref[...] = a
    b_ref[...] = b

    def group_step(gi, h):
        off = pl.multiple_of(gi * 8, 8)
        hg = b_ref[pl.ds(off, 8), :] + a_ref[pl.ds(off, 8), :] * h
        b_ref[pl.ds(off, 8), :] = hg
        return hg[7:8, :]

    h_ref[...] = lax.fori_loop(0, tl // 8, group_step, h_ref[...])
    y_ref[...] = (b_ref[...] * _gelu_tanh(gate_ref[...].astype(F32))).astype(y_ref.dtype)


def _rglru(z, conv_w, conv_b, w_rg, b_rg, lam, *, batch, seq, u_col, gate_col, tl=512):
    n = z.shape[0]
    w = LRU_BLOCK_WIDTH
    spb = seq // tl
    rpb = seq // BF16_SUBLANES

    def prev_map(b, c, s):
        return (jnp.maximum(b * rpb + s * (tl // BF16_SUBLANES) - 1, 0), u_col + c)

    return pl.pallas_call(
        _lru_body,
        grid=(batch, LRU_BLOCKS, spb),
        in_specs=[
            pl.BlockSpec((tl, w), lambda b, c, s: (b * spb + s, u_col + c)),
            pl.BlockSpec((BF16_SUBLANES, w), prev_map),
            pl.BlockSpec((tl, w), lambda b, c, s: (b * spb + s, gate_col + c)),
            pl.BlockSpec((CONV_WIDTH, w), lambda b, c, s: (0, c)),
            pl.BlockSpec((1, w), lambda b, c, s: (0, c)),
            pl.BlockSpec((None, w, 2 * w), lambda b, c, s: (c, 0, 0)),
            pl.BlockSpec((None, 1, 2 * w), lambda b, c, s: (c, 0, 0)),
            pl.BlockSpec((1, w), lambda b, c, s: (0, c)),
        ],
        out_specs=pl.BlockSpec((tl, w), lambda b, c, s: (b * spb + s, c)),
        out_shape=jax.ShapeDtypeStruct((n, LRU_WIDTH), BF16),
        scratch_shapes=[
            pltpu.VMEM((tl + 8, w), F32),
            pltpu.VMEM((tl, w), F32),
            pltpu.VMEM((tl, w), F32),
            pltpu.VMEM((1, w), F32),
        ],
        compiler_params=_params("arbitrary", "arbitrary", "arbitrary"),
        name="rglru",
    )(z, z, z, conv_w, conv_b, w_rg, b_rg, lam)


def _merge_body(ya_ref, yl_ref, ga_ref, gl_ref, pa_ref, pl_ref, o_ref):
    pa = jnp.dot(ya_ref[...], pa_ref[...], preferred_element_type=F32)
    pr = jnp.dot(yl_ref[...], pl_ref[...], preferred_element_type=F32)
    merged = jax.nn.sigmoid(ga_ref[...].astype(F32)) * pa + jax.nn.sigmoid(gl_ref[...].astype(F32)) * pr
    o_ref[...] = merged.astype(o_ref.dtype)


def _merge(y_attn, y_lru, z, w_attn_proj, w_lru_proj, *, ga_col, gl_col, tm=512):
    n, d = y_attn.shape
    return pl.pallas_call(
        _merge_body,
        grid=(n // tm,),
        in_specs=[
            pl.BlockSpec((tm, d), lambda i: (i, 0)),
            pl.BlockSpec((tm, d), lambda i: (i, 0)),
            pl.BlockSpec((tm, d), lambda i: (i, ga_col)),
            pl.BlockSpec((tm, d), lambda i: (i, gl_col)),
            _resident((d, d), lambda i: (0, 0)),
            _resident((d, d), lambda i: (0, 0)),
        ],
        out_specs=pl.BlockSpec((tm, d), lambda i: (i, 0)),
        out_shape=jax.ShapeDtypeStruct((n, d), BF16),
        compiler_params=_params("arbitrary"),
        name="merge",
    )(y_attn, y_lru, z, z, w_attn_proj, w_lru_proj)


def _outproj_body(x_ref, m_ref, wo_ref, g_ref, wr_ref, br_ref, tri_ref, h_ref, xn_ref, idx_ref, wt_ref, rank_ref,
                  cnt_ref):
    @pl.when(pl.program_id(0) == 0)
    def _():
        cnt_ref[...] = jnp.zeros_like(cnt_ref)

    h = x_ref[...] + jnp.dot(m_ref[...], wo_ref[...], preferred_element_type=F32)
    h_ref[...] = h
    xn = _rms_scale(h) * g_ref[...]
    xn_ref[...] = _pack_bf16_pairs(xn)
    logits = lax.dot_general(wr_ref[...], xn.astype(BF16), (((1,), (1,)), ((), ())),
                             preferred_element_type=F32) + br_ref[...]
    expert = lax.broadcasted_iota(jnp.int32, logits.shape, 0).astype(F32)
    vals, sels = [], []
    for k in range(TOP_K):
        m = jnp.max(logits, axis=0, keepdims=True)
        sel = jnp.min(jnp.where(logits == m, expert, float(N_EXPERTS)), axis=0, keepdims=True)
        vals.append(m)
        sels.append(sel)
        idx_ref[k:k + 1, :] = sel.astype(jnp.int32)
        logits = jnp.where(expert == sel, -jnp.inf, logits)
    exps = [jnp.exp(v - vals[0]) for v in vals]
    denom = exps[0] + exps[1] + exps[2] + exps[3]
    for k in range(TOP_K):
        wt_ref[k:k + 1, :] = exps[k] / denom

    chosen = jnp.zeros(logits.shape, F32)
    for sel in sels:
        chosen = chosen + jnp.where(expert == sel, 1.0, 0.0)
    before = jnp.dot(chosen.astype(BF16), tri_ref[...], preferred_element_type=F32) + cnt_ref[:, 0:1]
    for k in range(TOP_K):
        rank = jnp.sum(jnp.where(expert == sels[k], before, 0.0), axis=0, keepdims=True)
        rank_ref[k:k + 1, :] = rank.astype(jnp.int32)
    cnt_ref[...] = cnt_ref[...] + jnp.sum(chosen, axis=1, keepdims=True)


def _outproj_router(x2, merged, w_out, g_ffn, w_router_t, b_router, *, tm=512):
    n, d = x2.shape
    lane = 128
    earlier = (jnp.arange(tm)[:, None] < jnp.arange(tm)[None, :]).astype(BF16)
    return pl.pallas_call(
        _outproj_body,
        grid=(n // tm,),
        in_specs=[
            pl.BlockSpec((tm, d), lambda i: (i, 0)),
            pl.BlockSpec((tm, d), lambda i: (i, 0)),
            _resident((d, d), lambda i: (0, 0)),
            pl.BlockSpec((1, d), lambda i: (0, 0)),
            pl.BlockSpec((N_EXPERTS, d), lambda i: (0, 0)),
            pl.BlockSpec((N_EXPERTS, 1), lambda i: (0, 0)),
            pl.BlockSpec((tm, tm), lambda i: (0, 0)),
        ],
        out_specs=[
            pl.BlockSpec((tm, d), lambda i: (i, 0)),
            pl.BlockSpec((tm, d // 2), lambda i: (i, 0)),
            pl.BlockSpec((TOP_K, tm), lambda i: (0, i)),
            pl.BlockSpec((TOP_K, tm), lambda i: (0, i)),
            pl.BlockSpec((TOP_K, tm), lambda i: (0, i)),
            pl.BlockSpec((N_EXPERTS, lane), lambda i: (0, 0)),
        ],
        out_shape=[
            jax.ShapeDtypeStruct((n, d), F32),
            jax.ShapeDtypeStruct((n, d // 2), jnp.uint32),
            jax.ShapeDtypeStruct((TOP_K, n), jnp.int32),
            jax.ShapeDtypeStruct((TOP_K, n), F32),
            jax.ShapeDtypeStruct((TOP_K, n), jnp.int32),
            jax.ShapeDtypeStruct((N_EXPERTS, lane), F32),
        ],
        compiler_params=_params("arbitrary"),
        name="outproj_router",
    )(x2, merged, w_out, g_ffn, w_router_t, b_router, earlier)


def _row_copy(src, dst, sem, src_row, dst_row):
    return pltpu.make_async_copy(src.at[pl.ds(src_row, 1), :], dst.at[pl.ds(dst_row, 1), :], sem)


def _dispatch_body(dst_ref, x_ref, init_hbm, out_hbm, stage, sems):
    del init_hbm
    i = pl.program_id(0)
    n = pl.num_programs(0)
    c = dst_ref.shape[1]
    slot = i % 2
    stage[slot] = x_ref[...]
    for r in range(c):
        _row_copy(stage.at[slot], out_hbm, sems.at[slot], r // TOP_K, dst_ref[0, r]).start()

    @pl.when(i > 0)
    def _():
        for r in range(c):
            _row_copy(stage.at[1 - slot], out_hbm, sems.at[1 - slot], 0, 0).wait()

    @pl.when(i == n - 1)
    def _():
        for r in range(c):
            _row_copy(stage.at[slot], out_hbm, sems.at[slot], 0, 0).wait()


def _dispatch(rows, pos, n_sorted, *, tm=128):
    n, w = rows.shape
    dst3 = pos.reshape(n // tm, 1, tm * TOP_K)
    init = jnp.zeros((n_sorted, w), rows.dtype)
    return pl.pallas_call(
        _dispatch_body,
        grid=(n // tm,),
        in_specs=[
            pl.BlockSpec((None, 1, tm * TOP_K), lambda i: (i, 0, 0), memory_space=pltpu.SMEM),
            pl.BlockSpec((tm, w), lambda i: (i, 0)),
            pl.BlockSpec(memory_space=pl.ANY),
        ],
        out_specs=pl.BlockSpec(memory_space=pl.ANY),
        out_shape=jax.ShapeDtypeStruct((n_sorted, w), rows.dtype),
        scratch_shapes=[pltpu.VMEM((2, tm, w), rows.dtype), pltpu.SemaphoreType.DMA((2,))],
        input_output_aliases={2: 0},
        compiler_params=_params("arbitrary"),
        name="dispatch",
    )(dst3, rows, init)


def _combine_body(idx_ref, nxt_ref, wt_ref, src_hbm, o_ref, buf, sems):
    i = pl.program_id(0)
    n = pl.num_programs(0)
    tm = o_ref.shape[0]
    count = TOP_K * tm
    slot = i % 2

    @pl.when(i == 0)
    def _():
        for r in range(count):
            _row_copy(src_hbm, buf.at[0], sems.at[0], idx_ref[0, r], r).start()

    @pl.when(i + 1 < n)
    def _():
        for r in range(count):
            _row_copy(src_hbm, buf.at[1 - slot], sems.at[1 - slot], nxt_ref[0, r], r).start()

    for r in range(count):
        _row_copy(src_hbm, buf.at[slot], sems.at[slot], 0, r).wait()

    c = o_ref.shape[1] // 2
    acc_lo = jnp.zeros((tm, c), F32)
    acc_hi = jnp.zeros((tm, c), F32)
    for k in range(TOP_K):
        lo, hi = _unpack_bf16_pairs(buf[slot, k * tm:(k + 1) * tm, :])
        acc_lo = acc_lo + wt_ref[:, k:k + 1] * lo
        acc_hi = acc_hi + wt_ref[:, k:k + 1] * hi
    o_ref[:, :c] = acc_lo
    o_ref[:, c:] = acc_hi


def _combine(y_sorted, pos, weights, *, tm=128):
    n = pos.shape[0]
    w = y_sorted.shape[1]
    steps = n // tm
    idx3 = pos.reshape(steps, tm, TOP_K).transpose(0, 2, 1).reshape(steps, 1, TOP_K * tm)
    return pl.pallas_call(
        _combine_body,
        grid=(steps,),
        in_specs=[
            pl.BlockSpec((None, 1, TOP_K * tm), lambda i: (i, 0, 0), memory_space=pltpu.SMEM),
            pl.BlockSpec((None, 1, TOP_K * tm), lambda i: (jnp.minimum(i + 1, steps - 1), 0, 0),
                         memory_space=pltpu.SMEM),
            pl.BlockSpec((tm, TOP_K), lambda i: (i, 0)),
            pl.BlockSpec(memory_space=pl.ANY),
        ],
        out_specs=pl.BlockSpec((tm, 2 * w), lambda i: (i, 0)),
        out_shape=jax.ShapeDtypeStruct((n, 2 * w), F32),
        scratch_shapes=[pltpu.VMEM((2, TOP_K * tm, w), y_sorted.dtype), pltpu.SemaphoreType.DMA((2,))],
        compiler_params=_params("arbitrary"),
        name="combine",
    )(idx3, idx3, weights, y_sorted)


def _moe_mlp_body(ie_ref, nvb_ref, nu_ref, x_ref, w1_hbm, w2_hbm, bg_ref, bl_ref, b2_ref, o_ref,
                  acc_ref, sg, sl, s2, wg_ref, wl_ref, w2_ref, sems):
    item = pl.program_id(0)
    chunk = pl.program_id(1)
    n_items = pl.num_programs(0)
    n_chunks = pl.num_programs(1)
    nvb = nvb_ref[item]
    f = w2_hbm.shape[1]
    fc = sg.shape[1]

    def weight_copies(it, ch):
        e = ie_ref[it]
        c0 = pl.multiple_of(ch * fc, fc)
        return (pltpu.make_async_copy(w1_hbm.at[e, :, pl.ds(c0, fc)], sg, sems.at[0]),
                pltpu.make_async_copy(w1_hbm.at[e, :, pl.ds(f + c0, fc)], sl, sems.at[1]),
                pltpu.make_async_copy(w2_hbm.at[e, pl.ds(c0, fc), :], s2, sems.at[2]))

    @pl.when(chunk == 0)
    def _():
        acc_ref[...] = jnp.broadcast_to(b2_ref[...], acc_ref.shape)

    @pl.when((item == 0) & (chunk == 0))
    def _():
        for cp in weight_copies(item, chunk):
            cp.start()

    @pl.when(nvb > 0)
    def _():
        for cp in weight_copies(item, chunk):
            cp.wait()
        wg_ref[...] = sg[...].astype(BF16)
        wl_ref[...] = sl[...].astype(BF16)
        w2_ref[...] = s2[...].astype(BF16)

        same_item = chunk + 1 < n_chunks
        nxt_item = jnp.minimum(jnp.where(same_item, item, item + 1), n_items - 1)
        nxt_chunk = jnp.where(same_item, chunk + 1, 0)

        @pl.when(same_item | ((item + 1 < n_items) & (nvb_ref[nxt_item] > 0)))
        def _():
            for cp in weight_copies(nxt_item, nxt_chunk):
                cp.start()

        def run_rows(first_block, n_blocks):
            rows = slice(first_block * EXPERT_BLOCK, (first_block + n_blocks) * EXPERT_BLOCK)
            lo, hi = _unpack_bf16_pairs(x_ref[rows, :])
            xj = jnp.concatenate([lo.astype(BF16), hi.astype(BF16)], axis=1)
            glu = jnp.dot(xj, wg_ref[...], preferred_element_type=F32) + bg_ref[...]
            lin = jnp.dot(xj, wl_ref[...], preferred_element_type=F32) + bl_ref[...]
            glu = jnp.minimum(glu, SWIGLU_LIMIT)
            lin = jnp.clip(lin, -SWIGLU_LIMIT, SWIGLU_LIMIT)
            act = glu * jax.nn.sigmoid(SWIGLU_ALPHA * glu) * (lin + 1.0)
            acc_ref[rows, :] += jnp.dot(act.astype(BF16), w2_ref[...], preferred_element_type=F32)

        n_sub = x_ref.shape[0] // EXPERT_BLOCK
        for j in range(0, n_sub, 2):
            if j + 2 <= n_sub:
                pl.when(nvb >= j + 2)(functools.partial(run_rows, j, 2))
                pl.when(nvb == j + 1)(functools.partial(run_rows, j, 1))
            else:
                pl.when(nvb >= j + 1)(functools.partial(run_rows, j, 1))

    @pl.when(chunk == n_chunks - 1)
    def _():
        o_ref[...] = _pack_bf16_pairs(acc_ref[...])


def _moe_mlp(x_sorted, w1, b1, w2, b2, item_expert, item_nvb, n_used):
    n_rows, half = x_sorted.shape
    d = 2 * half
    n_items = n_rows // EXPERT_ITEM
    f = w2.shape[1]
    nc = f // FF_CHUNK

    def live_chunk(i, c, nu):
        return jnp.where(i < nu[0], c, nc - 1)

    def rows_map(i, c, ie, nvb, nu):
        return (jnp.minimum(i, nu[0] - 1), 0)

    return pl.pallas_call(
        _moe_mlp_body,
        grid_spec=pltpu.PrefetchScalarGridSpec(
            num_scalar_prefetch=3,
            grid=(n_items, nc),
            in_specs=[
                pl.BlockSpec((EXPERT_ITEM, half), rows_map),
                pl.BlockSpec(memory_space=pl.ANY),
                pl.BlockSpec(memory_space=pl.ANY),
                pl.BlockSpec((None, 1, FF_CHUNK), lambda i, c, ie, nvb, nu: (ie[i], 0, live_chunk(i, c, nu))),
                pl.BlockSpec((None, 1, FF_CHUNK), lambda i, c, ie, nvb, nu: (ie[i], 0, nc + live_chunk(i, c, nu))),
                pl.BlockSpec((None, 1, d), lambda i, c, ie, nvb, nu: (ie[i], 0, 0)),
            ],
            out_specs=pl.BlockSpec((EXPERT_ITEM, half), lambda i, c, ie, nvb, nu: (i, 0)),
            scratch_shapes=[
                pltpu.VMEM((EXPERT_ITEM, d), F32),
                pltpu.VMEM((d, FF_CHUNK), F32),
                pltpu.VMEM((d, FF_CHUNK), F32),
                pltpu.VMEM((FF_CHUNK, d), F32),
                pltpu.VMEM((d, FF_CHUNK), BF16),
                pltpu.VMEM((d, FF_CHUNK), BF16),
                pltpu.VMEM((FF_CHUNK, d), BF16),
                pltpu.SemaphoreType.DMA((3,)),
            ],
        ),
        out_shape=jax.ShapeDtypeStruct((n_rows, half), jnp.uint32),
        compiler_params=pltpu.CompilerParams(dimension_semantics=("arbitrary", "arbitrary"),
                                             vmem_limit_bytes=MOE_VMEM_LIMIT_BYTES),
        name="moe_mlp",
    )(item_expert, item_nvb, n_used, x_sorted, w1, w2, b1, b1, b2)


def _final_body(h_ref, moe_ref, p_ref, gp_ref, wg_ref, wp_ref, gf_ref, o_ref, *, final_norm):
    h = h_ref[...] + moe_ref[...]
    xn = (_rms_scale(h) * gp_ref[...]).astype(BF16)
    gate = jax.nn.sigmoid(jnp.dot(xn, wg_ref[...], preferred_element_type=F32))
    ple = jnp.dot(p_ref[...].astype(BF16), wp_ref[...], preferred_element_type=F32)
    h = h + gate * ple
    o_ref[...] = _rms_scale(h) * gf_ref[...] if final_norm else h


def _final(h1, moe, p2, g_ple, w_ple_gate, w_ple, g_final, *, final_norm, tm=512):
    n, d = h1.shape
    return pl.pallas_call(
        functools.partial(_final_body, final_norm=final_norm),
        grid=(n // tm,),
        in_specs=[
            pl.BlockSpec((tm, d), lambda i: (i, 0)),
            pl.BlockSpec((tm, d), lambda i: (i, 0)),
            pl.BlockSpec((tm, PLE_DIM), lambda i: (i, 0)),
            pl.BlockSpec((1, d), lambda i: (0, 0)),
            _resident((d, d), lambda i: (0, 0)),
            _resident((PLE_DIM, d), lambda i: (0, 0)),
            pl.BlockSpec((1, d), lambda i: (0, 0)),
        ],
        out_specs=pl.BlockSpec((tm, d), lambda i: (i, 0)),
        out_shape=jax.ShapeDtypeStruct((n, d), F32),
        compiler_params=_params("arbitrary"),
        name="final",
    )(h1, moe, p2, g_ple, w_ple_gate, w_ple, g_final)


def _routing_tables(expert_ids, rank, counts, n_tokens):
    nk = n_tokens * TOP_K
    sub = EXPERT_ITEM // EXPERT_BLOCK
    padded = (counts + EXPERT_ITEM - 1) // EXPERT_ITEM * EXPERT_ITEM
    pend = jnp.cumsum(padded)
    pstart = pend - padded
    hit = expert_ids[None, :, :] == jnp.arange(N_EXPERTS, dtype=jnp.int32)[:, None, None]
    dest = (jnp.sum(jnp.where(hit, pstart[:, None, None], 0), axis=0) + rank).T
    n_items = nk // EXPERT_ITEM + N_EXPERTS
    n_used = (pend[-1] // EXPERT_ITEM).astype(jnp.int32)
    item_start = jnp.arange(n_items, dtype=jnp.int32) * EXPERT_ITEM
    item_expert = jnp.minimum(jnp.searchsorted(pend, item_start, side='right'), N_EXPERTS - 1).astype(jnp.int32)
    used = jnp.arange(n_items) < n_used
    item_expert = jnp.where(used, item_expert, item_expert[n_used - 1])
    rows_left = counts[item_expert] - (item_start - pstart[item_expert])
    item_nvb = jnp.where(used, jnp.clip((rows_left + EXPERT_BLOCK - 1) // EXPERT_BLOCK, 0, sub), 0).astype(jnp.int32)
    pos = dest.reshape(n_tokens, TOP_K).astype(jnp.int32)
    return pos, item_expert, item_nvb, n_used.reshape(1), n_items * EXPERT_ITEM


def _layer(h, p_l, norm_mix_g, w_in, b_in, conv_w, conv_b, w_rg_a, b_rg_a, w_rg_x, b_rg_x, lru_lambda,
           attn_sinks, w_attn_proj, w_lru_proj, w_out, norm_ffn_g, w_router, b_router, w_mlp1, b_mlp1,
           w_mlp2, b_mlp2, norm_ple_g, w_ple, w_ple_gate, norm_final_g, *, batch, seq, final_norm):
    n = h.shape[0]
    row = lambda v: v.reshape(1, -1)

    kv0 = Q_WIDTH
    rest0 = Q_WIDTH + 2 * KV_WIDTH
    reorder = lambda a: jnp.concatenate([a[..., :kv0], a[..., rest0:], a[..., kv0:rest0]], axis=-1)
    z = _inproj(h, row(norm_mix_g), reorder(w_in).astype(BF16), row(reorder(b_in)))

    kv_col = (Q_WIDTH + 2 * LRU_WIDTH + 2 * D_MODEL) // (2 * KV_WIDTH)
    y_attn = _attention(z, attn_sinks, batch=batch, seq=seq, q_col=0, kv_col=kv_col)

    w_rg = jnp.concatenate([w_rg_a, w_rg_x], axis=-1).astype(BF16)
    b_rg = jnp.concatenate([b_rg_a, b_rg_x], axis=-1).reshape(LRU_BLOCKS, 1, 2 * LRU_BLOCK_WIDTH)
    y_lru = _rglru(z, conv_w, row(conv_b), w_rg, b_rg, row(lru_lambda), batch=batch, seq=seq,
                   u_col=Q_WIDTH // LRU_BLOCK_WIDTH, gate_col=(Q_WIDTH + LRU_WIDTH) // LRU_BLOCK_WIDTH)

    merged = _merge(y_attn, y_lru, z, w_attn_proj.astype(BF16), w_lru_proj.astype(BF16),
                    ga_col=(Q_WIDTH + 2 * LRU_WIDTH) // D_MODEL, gl_col=(Q_WIDTH + 2 * LRU_WIDTH + D_MODEL) // D_MODEL)
    h1, xn2, expert_ids, expert_w, rank, counts = _outproj_router(
        h, merged, w_out.astype(BF16), row(norm_ffn_g), w_router.T.astype(BF16), b_router.reshape(N_EXPERTS, 1))

    pos, item_expert, item_nvb, n_used, n_sorted = _routing_tables(
        expert_ids, rank, counts[:, 0].astype(jnp.int32), n)
    x_sorted = _dispatch(xn2, pos, n_sorted)
    y_sorted = _moe_mlp(x_sorted, w_mlp1, b_mlp1[:, None, :], w_mlp2, b_mlp2[:, None, :], item_expert, item_nvb, n_used)
    moe = _combine(y_sorted, pos, expert_w.T)

    return _final(h1, moe, p_l, row(norm_ple_g), w_ple_gate.astype(BF16), w_ple.astype(BF16), row(norm_final_g),
                  final_norm=final_norm)


def kernel(x, p, norm_mix_g, w_in, b_in, conv_w, conv_b, w_rg_a, b_rg_a, w_rg_x, b_rg_x, lru_lambda, attn_sinks, w_attn_proj, w_lru_proj, w_out, norm_ffn_g, w_router, b_router, w_mlp1, b_mlp1, w_mlp2, b_mlp2, norm_ple_g, w_ple, w_ple_gate, norm_final_g):
    batch, seq, d = x.shape
    depth = p.shape[0]
    h = x.reshape(batch * seq, d)
    for l in range(depth):
        h = _layer(h, p[l].reshape(batch * seq, -1), norm_mix_g[l], w_in[l], b_in[l], conv_w[l], conv_b[l],
                   w_rg_a[l], b_rg_a[l], w_rg_x[l], b_rg_x[l], lru_lambda[l], attn_sinks[l], w_attn_proj[l],
                   w_lru_proj[l], w_out[l], norm_ffn_g[l], w_router[l], b_router[l], w_mlp1[l], b_mlp1[l],
                   w_mlp2[l], b_mlp2[l], norm_ple_g[l], w_ple[l], w_ple_gate[l], norm_final_g,
                   batch=batch, seq=seq, final_norm=(l == depth - 1))
    return h.reshape(batch, seq, d)
```

```python
import functools

import jax
import jax.numpy as jnp
from jax import lax
from jax.experimental import pallas as pl
from jax.experimental.pallas import tpu as pltpu

D_MODEL = 2048
N_Q_HEADS = 32
N_KV_HEADS = 4
HEAD_DIM = 64
Q_WIDTH = N_Q_HEADS * HEAD_DIM
KV_WIDTH = N_KV_HEADS * HEAD_DIM
WINDOW = 128
LRU_WIDTH = D_MODEL
LRU_BLOCKS = 8
LRU_BLOCK_WIDTH = LRU_WIDTH // LRU_BLOCKS
CONV_WIDTH = 4
LRU_C = 8.0
IN_WIDTH = Q_WIDTH + 2 * KV_WIDTH + 2 * LRU_WIDTH + 2 * D_MODEL
N_EXPERTS = 32
TOP_K = 4
D_EXPERT = D_MODEL
SWIGLU_LIMIT = 7.0
SWIGLU_ALPHA = 1.702
EXPERT_BLOCK = 256
EXPERT_ITEM = 1280
FF_CHUNK = 512
PLE_DIM = 256
RMS_EPS = 1e-6
LOG2_E = 1.4426950408889634

VMEM_LIMIT_BYTES = 56 * 1024 * 1024
MOE_VMEM_LIMIT_BYTES = 60 * 1024 * 1024
BF16_SUBLANES = 16

F32 = jnp.float32
BF16 = jnp.bfloat16


def _params(*sem):
    return pltpu.CompilerParams(dimension_semantics=sem, vmem_limit_bytes=VMEM_LIMIT_BYTES)


def _resident(shape, index_map):
    return pl.BlockSpec(shape, index_map, pipeline_mode=pl.Buffered(1))


def _rms_scale(x):
    var = jnp.mean(x * x, axis=-1, keepdims=True)
    return x * lax.rsqrt(var + RMS_EPS)


def _pack_bf16_pairs(x):
    c = x.shape[1] // 2
    xb = x.astype(BF16).astype(F32)
    lo = lax.bitcast_convert_type(xb[:, :c], jnp.uint32) >> 16
    hi = lax.bitcast_convert_type(xb[:, c:], jnp.uint32) & jnp.uint32(0xFFFF0000)
    return hi | lo


def _unpack_bf16_pairs(w):
    lo = lax.bitcast_convert_type(w << 16, F32)
    hi = lax.bitcast_convert_type(w & jnp.uint32(0xFFFF0000), F32)
    return lo, hi


def _inproj_body(x_ref, g_ref, w_ref, b_ref, o_ref, xn_ref):
    @pl.when(pl.program_id(1) == 0)
    def _():
        xn_ref[...] = (_rms_scale(x_ref[...]) * g_ref[...]).astype(BF16)

    acc = jnp.dot(xn_ref[...], w_ref[...], preferred_element_type=F32)
    o_ref[...] = (acc + b_ref[...]).astype(o_ref.dtype)


def _inproj(x2, g, w, b, *, tm=1024, tn=1536):
    n, d = x2.shape
    width = w.shape[1]
    return pl.pallas_call(
        _inproj_body,
        grid=(n // tm, width // tn),
        in_specs=[
            pl.BlockSpec((tm, d), lambda i, j: (i, 0)),
            pl.BlockSpec((1, d), lambda i, j: (0, 0)),
            pl.BlockSpec((d, tn), lambda i, j: (0, j)),
            pl.BlockSpec((1, tn), lambda i, j: (0, j)),
        ],
        out_specs=pl.BlockSpec((tm, tn), lambda i, j: (i, j)),
        out_shape=jax.ShapeDtypeStruct((n, width), BF16),
        scratch_shapes=[pltpu.VMEM((tm, d), BF16)],
        compiler_params=_params("arbitrary", "arbitrary"),
        name="inproj",
    )(x2, g, w, b)


def _block_diag_pair(pair, odd, lo_mask):
    swapped = jnp.concatenate([pair[:, HEAD_DIM:], pair[:, :HEAD_DIM]], axis=1)
    zero = jnp.zeros_like(pair)
    if odd:
        lo = jnp.where(lo_mask, swapped, zero)
        hi = jnp.where(lo_mask, zero, pair)
    else:
        lo = jnp.where(lo_mask, pair, zero)
        hi = jnp.where(lo_mask, zero, swapped)
    return jnp.concatenate([lo, hi], axis=0)


def _attn_body(sinks_ref, q_ref, kvp_ref, kvc_ref, o_ref, *, tq):
    nblk = tq // WINDOW
    group = N_Q_HEADS // N_KV_HEADS
    pairs = group // 2
    rows = pairs * WINDOW
    band_w = 2 * WINDOW
    s_idx = pl.program_id(1)

    lo_mask = lax.broadcasted_iota(jnp.int32, (1, 2 * HEAD_DIM), 1) < HEAD_DIM
    qi = lax.broadcasted_iota(jnp.int32, (rows, band_w), 0) % WINDOW
    kj = lax.broadcasted_iota(jnp.int32, (rows, band_w), 1)
    in_band = (kj > qi) & (kj <= qi + WINDOW)
    first_key = jnp.where(s_idx > 0, 0, WINDOW)
    row_pair = lax.broadcasted_iota(jnp.int32, (rows, 1), 0) // WINDOW
    neg = jnp.finfo(F32).min

    for i in range(nblk):
        if i == 0:
            band = jnp.concatenate([kvp_ref[...], kvc_ref[0:WINDOW, :]], axis=0)
            mask = in_band & (kj >= first_key)
        else:
            band = kvc_ref[(i - 1) * WINDOW:(i + 1) * WINDOW, :]
            mask = in_band
        for kh in range(N_KV_HEADS):
            col = (kh // 2) * 2 * HEAD_DIM
            kk = _block_diag_pair(band[:, col:col + 2 * HEAD_DIM], kh % 2 == 1, lo_mask)
            vv = _block_diag_pair(band[:, KV_WIDTH + col:KV_WIDTH + col + 2 * HEAD_DIM], kh % 2 == 1, lo_mask)
            qbase = kh * group * HEAD_DIM
            q4 = jnp.concatenate(
                [q_ref[i * WINDOW:(i + 1) * WINDOW, qbase + 2 * HEAD_DIM * j:qbase + 2 * HEAD_DIM * (j + 1)]
                 for j in range(pairs)], axis=0)
            s = lax.dot_general(q4, kk, (((1,), (1,)), ((), ())), preferred_element_type=F32)
            s = s * (HEAD_DIM ** -0.5 * LOG2_E)
            sink_lo = jnp.zeros((rows, 1), F32)
            sink_hi = jnp.zeros((rows, 1), F32)
            for j in range(pairs):
                sink_lo = jnp.where(row_pair == j, sinks_ref[kh * group + 2 * j] * LOG2_E, sink_lo)
                sink_hi = jnp.where(row_pair == j, sinks_ref[kh * group + 2 * j + 1] * LOG2_E, sink_hi)
            halves = []
            inv = []
            for half, sink in ((0, sink_lo), (1, sink_hi)):
                sh = jnp.where(mask, s[:, half * band_w:(half + 1) * band_w], neg)
                m = jnp.maximum(jnp.max(sh, axis=1, keepdims=True), sink)
                p = jnp.exp2(sh - m)
                denom = jnp.sum(p, axis=1, keepdims=True) + jnp.exp2(sink - m)
                halves.append(p.astype(BF16))
                inv.append(1.0 / denom)
            pcat = jnp.concatenate(halves, axis=1)
            o = jnp.dot(pcat, vv, preferred_element_type=F32)
            o = o * jnp.where(lo_mask, inv[0], inv[1])
            for j in range(pairs):
                o_ref[i * WINDOW:(i + 1) * WINDOW, qbase + 2 * HEAD_DIM * j:qbase + 2 * HEAD_DIM * (j + 1)] = (
                    o[j * WINDOW:(j + 1) * WINDOW, :].astype(o_ref.dtype))


def _attention(z, sinks, *, batch, seq, q_col, kv_col, tq=512):
    n = z.shape[0]
    spb = seq // tq
    wpb = seq // WINDOW
    kvw = 2 * KV_WIDTH

    def prev_map(b, s, sinks_ref):
        return (jnp.maximum(b * wpb + s * (tq // WINDOW) - 1, 0), kv_col)

    return pl.pallas_call(
        functools.partial(_attn_body, tq=tq),
        grid_spec=pltpu.PrefetchScalarGridSpec(
            num_scalar_prefetch=1,
            grid=(batch, spb),
            in_specs=[
                pl.BlockSpec((tq, Q_WIDTH), lambda b, s, sinks_ref: (b * spb + s, q_col)),
                pl.BlockSpec((WINDOW, kvw), prev_map),
                pl.BlockSpec((tq, kvw), lambda b, s, sinks_ref: (b * spb + s, kv_col)),
            ],
            out_specs=pl.BlockSpec((tq, Q_WIDTH), lambda b, s, sinks_ref: (b * spb + s, 0)),
        ),
        out_shape=jax.ShapeDtypeStruct((n, Q_WIDTH), BF16),
        compiler_params=_params("arbitrary", "arbitrary"),
        name="attn",
    )(sinks, z, z, z)


def _gelu_tanh(x):
    c = (2.0 / jnp.pi) ** 0.5
    return x * (0.5 * (1.0 + jnp.tanh(c * (x + 0.044715 * (x * x * x)))))


def _lru_body(u_ref, up_ref, gate_ref, cw_ref, cb_ref, wrg_ref, brg_ref, lam_ref, y_ref,
              ext_ref, a_ref, b_ref, h_ref):
    s_idx = pl.program_id(2)
    tl, w = u_ref.shape
    pad = 8

    @pl.when(s_idx == 0)
    def _():
        ext_ref[0:pad, :] = jnp.zeros((pad, w), F32)
        h_ref[...] = jnp.zeros_like(h_ref)

    @pl.when(s_idx > 0)
    def _():
        ext_ref[0:pad, :] = up_ref[...].astype(F32)[BF16_SUBLANES - pad:BF16_SUBLANES, :]

    ext_ref[pad:pad + tl, :] = u_ref[...].astype(F32)
    first = pad - (CONV_WIDTH - 1)
    uc = cb_ref[...] + ext_ref[first:first + tl, :] * cw_ref[0:1, :]
    for tap in range(1, CONV_WIDTH):
        uc = uc + ext_ref[first + tap:first + tap + tl, :] * cw_ref[tap:tap + 1, :]

    gates = jnp.dot(uc.astype(BF16), wrg_ref[...], preferred_element_type=F32) + brg_ref[...]
    r = jax.nn.sigmoid(gates[:, :w])
    ig = jax.nn.sigmoid(gates[:, w:])
    neg_lam = -lam_ref[...]
    softplus = jnp.maximum(neg_lam, 0.0) + jnp.log1p(jnp.exp(-jnp.abs(neg_lam)))
    log_a = (-LRU_C * r) * softplus
    a = jnp.exp(log_a)
    m2 = -jnp.tanh(log_a) * (a * a + 1.0)
    mult = jnp.where(m2 > 0.0, m2 * lax.rsqrt(m2), 0.0)
    row = lax.broadcasted_iota(jnp.int32, (tl, 1), 0)
    mult = jnp.where((row == 0) & (s_idx == 0), 1.0, mult)
    b = mult * (ig * uc)

    sub = row % 8
    for d in (1, 2, 4):
        keep = sub >= d
        b_prev = jnp.where(keep, pltpu.roll(b, d, 0), 0.0)
        a_prev = jnp.where(keep, pltpu.roll(a, d, 0), 1.0)
        b = b + a * b_prev
        a = a * a_prev
    a_ref[...] = a
    b_ref[...] = b

    def group_step(gi, h):
        off = pl.multiple_of(gi * 8, 8)
        hg = b_ref[pl.ds(off, 8), :] + a_ref[pl.ds(off, 8), :] * h
        b_ref[pl.ds(off, 8), :] = hg
        return hg[7:8, :]

    h_ref[...] = lax.fori_loop(0, tl // 8, group_step, h_ref[...])
    y_ref[...] = (b_ref[...] * _gelu_tanh(gate_ref[...].astype(F32))).astype(y_ref.dtype)


def _rglru(z, conv_w, conv_b, w_rg, b_rg, lam, *, batch, seq, u_col, gate_col, tl=512):
    n = z.shape[0]
    w = LRU_BLOCK_WIDTH
    spb = seq // tl
    rpb = seq // BF16_SUBLANES

    def prev_map(b, c, s):
        return (jnp.maximum(b * rpb + s * (tl // BF16_SUBLANES) - 1, 0), u_col + c)

    return pl.pallas_call(
        _lru_body,
        grid=(batch, LRU_BLOCKS, spb),
        in_specs=[
            pl.BlockSpec((tl, w), lambda b, c, s: (b * spb + s, u_col + c)),
            pl.BlockSpec((BF16_SUBLANES, w), prev_map),
            pl.BlockSpec((tl, w), lambda b, c, s: (b * spb + s, gate_col + c)),
            pl.BlockSpec((CONV_WIDTH, w), lambda b, c, s: (0, c)),
            pl.BlockSpec((1, w), lambda b, c, s: (0, c)),
            pl.BlockSpec((None, w, 2 * w), lambda b, c, s: (c, 0, 0)),
            pl.BlockSpec((None, 1, 2 * w), lambda b, c, s: (c, 0, 0)),
            pl.BlockSpec((1, w), lambda b, c, s: (0, c)),
        ],
        out_specs=pl.BlockSpec((tl, w), lambda b, c, s: (b * spb + s, c)),
        out_shape=jax.ShapeDtypeStruct((n, LRU_WIDTH), BF16),
        scratch_shapes=[
            pltpu.VMEM((tl + 8, w), F32),
            pltpu.VMEM((tl, w), F32),
            pltpu.VMEM((tl, w), F32),
            pltpu.VMEM((1, w), F32),
        ],
        compiler_params=_params("arbitrary", "arbitrary", "arbitrary"),
        name="rglru",
    )(z, z, z, conv_w, conv_b, w_rg, b_rg, lam)


def _merge_body(ya_ref, yl_ref, ga_ref, gl_ref, pa_ref, pl_ref, o_ref):
    pa = jnp.dot(ya_ref[...], pa_ref[...], preferred_element_type=F32)
    pr = jnp.dot(yl_ref[...], pl_ref[...], preferred_element_type=F32)
    merged = jax.nn.sigmoid(ga_ref[...].astype(F32)) * pa + jax.nn.sigmoid(gl_ref[...].astype(F32)) * pr
    o_ref[...] = merged.astype(o_ref.dtype)


def _merge(y_attn, y_lru, z, w_attn_proj, w_lru_proj, *, ga_col, gl_col, tm=512):
    n, d = y_attn.shape
    return pl.pallas_call(
        _merge_body,
        grid=(n // tm,),
        in_specs=[
            pl.BlockSpec((tm, d), lambda i: (i, 0)),
            pl.BlockSpec((tm, d), lambda i: (i, 0)),
            pl.BlockSpec((tm, d), lambda i: (i, ga_col)),
            pl.BlockSpec((tm, d), lambda i: (i, gl_col)),
            _resident((d, d), lambda i: (0, 0)),
            _resident((d, d), lambda i: (0, 0)),
        ],
        out_specs=pl.BlockSpec((tm, d), lambda i: (i, 0)),
        out_shape=jax.ShapeDtypeStruct((n, d), BF16),
        compiler_params=_params("arbitrary"),
        name="merge",
    )(y_attn, y_lru, z, z, w_attn_proj, w_lru_proj)


def _outproj_body(x_ref, m_ref, wo_ref, g_ref, wr_ref, br_ref, tri_ref, h_ref, xn_ref, idx_ref, wt_ref, rank_ref,
                  cnt_ref):
    @pl.when(pl.program_id(0) == 0)
    def _():
        cnt_ref[...] = jnp.zeros_like(cnt_ref)

    h = x_ref[...] + jnp.dot(m_ref[...], wo_ref[...], preferred_element_type=F32)
    h_ref[...] = h
    xn = _rms_scale(h) * g_ref[...]
    xn_ref[...] = _pack_bf16_pairs(xn)
    logits = lax.dot_general(wr_ref[...], xn.astype(BF16), (((1,), (1,)), ((), ())),
                             preferred_element_type=F32) + br_ref[...]
    expert = lax.broadcasted_iota(jnp.int32, logits.shape, 0).astype(F32)
    vals, sels = [], []
    for k in range(TOP_K):
        m = jnp.max(logits, axis=0, keepdims=True)
        sel = jnp.min(jnp.where(logits == m, expert, float(N_EXPERTS)), axis=0, keepdims=True)
        vals.append(m)
        sels.append(sel)
        idx_ref[k:k + 1, :] = sel.astype(jnp.int32)
        logits = jnp.where(expert == sel, -jnp.inf, logits)
    exps = [jnp.exp(v - vals[0]) for v in vals]
    denom = exps[0] + exps[1] + exps[2] + exps[3]
    for k in range(TOP_K):
        wt_ref[k:k + 1, :] = exps[k] / denom

    chosen = jnp.zeros(logits.shape, F32)
    for sel in sels:
        chosen = chosen + jnp.where(expert == sel, 1.0, 0.0)
    before = jnp.dot(chosen.astype(BF16), tri_ref[...], preferred_element_type=F32) + cnt_ref[:, 0:1]
    for k in range(TOP_K):
        rank = jnp.sum(jnp.where(expert == sels[k], before, 0.0), axis=0, keepdims=True)
        rank_ref[k:k + 1, :] = rank.astype(jnp.int32)
    cnt_ref[...] = cnt_ref[...] + jnp.sum(chosen, axis=1, keepdims=True)


def _outproj_router(x2, merged, w_out, g_ffn, w_router_t, b_router, *, tm=512):
    n, d = x2.shape
    lane = 128
    earlier = (jnp.arange(tm)[:, None] < jnp.arange(tm)[None, :]).astype(BF16)
    return pl.pallas_call(
        _outproj_body,
        grid=(n // tm,),
        in_specs=[
            pl.BlockSpec((tm, d), lambda i: (i, 0)),
            pl.BlockSpec((tm, d), lambda i: (i, 0)),
            _resident((d, d), lambda i: (0, 0)),
            pl.BlockSpec((1, d), lambda i: (0, 0)),
            pl.BlockSpec((N_EXPERTS, d), lambda i: (0, 0)),
            pl.BlockSpec((N_EXPERTS, 1), lambda i: (0, 0)),
            pl.BlockSpec((tm, tm), lambda i: (0, 0)),
        ],
        out_specs=[
            pl.BlockSpec((tm, d), lambda i: (i, 0)),
            pl.BlockSpec((tm, d // 2), lambda i: (i, 0)),
            pl.BlockSpec((TOP_K, tm), lambda i: (0, i)),
            pl.BlockSpec((TOP_K, tm), lambda i: (0, i)),
            pl.BlockSpec((TOP_K, tm), lambda i: (0, i)),
            pl.BlockSpec((N_EXPERTS, lane), lambda i: (0, 0)),
        ],
        out_shape=[
            jax.ShapeDtypeStruct((n, d), F32),
            jax.ShapeDtypeStruct((n, d // 2), jnp.uint32),
            jax.ShapeDtypeStruct((TOP_K, n), jnp.int32),
            jax.ShapeDtypeStruct((TOP_K, n), F32),
            jax.ShapeDtypeStruct((TOP_K, n), jnp.int32),
            jax.ShapeDtypeStruct((N_EXPERTS, lane), F32),
        ],
        compiler_params=_params("arbitrary"),
        name="outproj_router",
    )(x2, merged, w_out, g_ffn, w_router_t, b_router, earlier)


def _row_copy(src, dst, sem, src_row, dst_row):
    return pltpu.make_async_copy(src.at[pl.ds(src_row, 1), :], dst.at[pl.ds(dst_row, 1), :], sem)


def _dispatch_body(dst_ref, x_ref, init_hbm, out_hbm, stage, sems):
    del init_hbm
    i = pl.program_id(0)
    n = pl.num_programs(0)
    c = dst_ref.shape[1]
    slot = i % 2
    stage[slot] = x_ref[...]
    for r in range(c):
        _row_copy(stage.at[slot], out_hbm, sems.at[slot], r // TOP_K, dst_ref[0, r]).start()

    @pl.when(i > 0)
    def _():
        for r in range(c):
            _row_copy(stage.at[1 - slot], out_hbm, sems.at[1 - slot], 0, 0).wait()

    @pl.when(i == n - 1)
    def _():
        for r in range(c):
            _row_copy(stage.at[slot], out_hbm, sems.at[slot], 0, 0).wait()


def _dispatch(rows, pos, n_sorted, *, tm=256):
    n, w = rows.shape
    dst3 = pos.reshape(n // tm, 1, tm * TOP_K)
    init = jnp.zeros((n_sorted, w), rows.dtype)
    return pl.pallas_call(
        _dispatch_body,
        grid=(n // tm,),
        in_specs=[
            pl.BlockSpec((None, 1, tm * TOP_K), lambda i: (i, 0, 0), memory_space=pltpu.SMEM),
            pl.BlockSpec((tm, w), lambda i: (i, 0)),
            pl.BlockSpec(memory_space=pl.ANY),
        ],
        out_specs=pl.BlockSpec(memory_space=pl.ANY),
        out_shape=jax.ShapeDtypeStruct((n_sorted, w), rows.dtype),
        scratch_shapes=[pltpu.VMEM((2, tm, w), rows.dtype), pltpu.SemaphoreType.DMA((2,))],
        input_output_aliases={2: 0},
        compiler_params=_params("arbitrary"),
        name="dispatch",
    )(dst3, rows, init)


def _combine_body(idx_ref, nxt_ref, wt_ref, src_hbm, o_ref, buf, sems):
    i = pl.program_id(0)
    n = pl.num_programs(0)
    tm = o_ref.shape[0]
    count = TOP_K * tm
    slot = i % 2

    @pl.when(i == 0)
    def _():
        for r in range(count):
            _row_copy(src_hbm, buf.at[0], sems.at[0], idx_ref[0, r], r).start()

    @pl.when(i + 1 < n)
    def _():
        for r in range(count):
            _row_copy(src_hbm, buf.at[1 - slot], sems.at[1 - slot], nxt_ref[0, r], r).start()

    for r in range(count):
        _row_copy(src_hbm, buf.at[slot], sems.at[slot], 0, r).wait()

    c = o_ref.shape[1] // 2
    acc_lo = jnp.zeros((tm, c), F32)
    acc_hi = jnp.zeros((tm, c), F32)
    for k in range(TOP_K):
        lo, hi = _unpack_bf16_pairs(buf[slot, k * tm:(k + 1) * tm, :])
        acc_lo = acc_lo + wt_ref[:, k:k + 1] * lo
        acc_hi = acc_hi + wt_ref[:, k:k + 1] * hi
    o_ref[:, :c] = acc_lo
    o_ref[:, c:] = acc_hi


def _combine(y_sorted, pos, weights, *, tm=256):
    n = pos.shape[0]
    w = y_sorted.shape[1]
    steps = n // tm
    idx3 = pos.reshape(steps, tm, TOP_K).transpose(0, 2, 1).reshape(steps, 1, TOP_K * tm)
    return pl.pallas_call(
        _combine_body,
        grid=(steps,),
        in_specs=[
            pl.BlockSpec((None, 1, TOP_K * tm), lambda i: (i, 0, 0), memory_space=pltpu.SMEM),
            pl.BlockSpec((None, 1, TOP_K * tm), lambda i: (jnp.minimum(i + 1, steps - 1), 0, 0),
                         memory_space=pltpu.SMEM),
            pl.BlockSpec((tm, TOP_K), lambda i: (i, 0)),
            pl.BlockSpec(memory_space=pl.ANY),
        ],
        out_specs=pl.BlockSpec((tm, 2 * w), lambda i: (i, 0)),
        out_shape=jax.ShapeDtypeStruct((n, 2 * w), F32),
        scratch_shapes=[pltpu.VMEM((2, TOP_K * tm, w), y_sorted.dtype), pltpu.SemaphoreType.DMA((2,))],
        compiler_params=_params("arbitrary"),
        name="combine",
    )(idx3, idx3, weights, y_sorted)


def _moe_mlp_body(ie_ref, nvb_ref, nu_ref, x_ref, w1_hbm, w2_hbm, bg_ref, bl_ref, b2_ref, o_ref,
                  acc_ref, sg, sl, s2, wg_ref, wl_ref, w2_ref, sems):
    item = pl.program_id(0)
    chunk = pl.program_id(1)
    n_items = pl.num_programs(0)
    n_chunks = pl.num_programs(1)
    nvb = nvb_ref[item]
    f = w2_hbm.shape[1]
    fc = sg.shape[1]

    def weight_copies(it, ch):
        e = ie_ref[it]
        c0 = pl.multiple_of(ch * fc, fc)
        return (pltpu.make_async_copy(w1_hbm.at[e, :, pl.ds(c0, fc)], sg, sems.at[0]),
                pltpu.make_async_copy(w1_hbm.at[e, :, pl.ds(f + c0, fc)], sl, sems.at[1]),
                pltpu.make_async_copy(w2_hbm.at[e, pl.ds(c0, fc), :], s2, sems.at[2]))

    @pl.when(chunk == 0)
    def _():
        acc_ref[...] = jnp.broadcast_to(b2_ref[...], acc_ref.shape)

    @pl.when((item == 0) & (chunk == 0))
    def _():
        for cp in weight_copies(item, chunk):
            cp.start()

    @pl.when(nvb > 0)
    def _():
        for cp in weight_copies(item, chunk):
            cp.wait()
        wg_ref[...] = sg[...].astype(BF16)
        wl_ref[...] = sl[...].astype(BF16)
        w2_ref[...] = s2[...].astype(BF16)

        same_item = chunk + 1 < n_chunks
        nxt_item = jnp.minimum(jnp.where(same_item, item, item + 1), n_items - 1)
        nxt_chunk = jnp.where(same_item, chunk + 1, 0)

        @pl.when(same_item | ((item + 1 < n_items) & (nvb_ref[nxt_item] > 0)))
        def _():
            for cp in weight_copies(nxt_item, nxt_chunk):
                cp.start()

        def run_rows(first_block, n_blocks):
            rows = slice(first_block * EXPERT_BLOCK, (first_block + n_blocks) * EXPERT_BLOCK)
            lo, hi = _unpack_bf16_pairs(x_ref[rows, :])
            xj = jnp.concatenate([lo.astype(BF16), hi.astype(BF16)], axis=1)
            glu = jnp.dot(xj, wg_ref[...], preferred_element_type=F32) + bg_ref[...]
            lin = jnp.dot(xj, wl_ref[...], preferred_element_type=F32) + bl_ref[...]
            glu = jnp.minimum(glu, SWIGLU_LIMIT)
            lin = jnp.clip(lin, -SWIGLU_LIMIT, SWIGLU_LIMIT)
            act = glu * jax.nn.sigmoid(SWIGLU_ALPHA * glu) * (lin + 1.0)
            acc_ref[rows, :] += jnp.dot(act.astype(BF16), w2_ref[...], preferred_element_type=F32)

        n_sub = x_ref.shape[0] // EXPERT_BLOCK
        for j in range(0, n_sub, 2):
            if j + 2 <= n_sub:
                pl.when(nvb >= j + 2)(functools.partial(run_rows, j, 2))
                pl.when(nvb == j + 1)(functools.partial(run_rows, j, 1))
            else:
                pl.when(nvb >= j + 1)(functools.partial(run_rows, j, 1))

    @pl.when(chunk == n_chunks - 1)
    def _():
        o_ref[...] = _pack_bf16_pairs(acc_ref[...])


def _moe_mlp(x_sorted, w1, b1, w2, b2, item_expert, item_nvb, n_used):
    n_rows, half = x_sorted.shape
    d = 2 * half
    n_items = n_rows // EXPERT_ITEM
    f = w2.shape[1]
    nc = f // FF_CHUNK

    def live_chunk(i, c, nu):
        return jnp.where(i < nu[0], c, nc - 1)

    def rows_map(i, c, ie, nvb, nu):
        return (jnp.minimum(i, nu[0] - 1), 0)

    return pl.pallas_call(
        _moe_mlp_body,
        grid_spec=pltpu.PrefetchScalarGridSpec(
            num_scalar_prefetch=3,
            grid=(n_items, nc),
            in_specs=[
                pl.BlockSpec((EXPERT_ITEM, half), rows_map),
                pl.BlockSpec(memory_space=pl.ANY),
                pl.BlockSpec(memory_space=pl.ANY),
                pl.BlockSpec((None, 1, FF_CHUNK), lambda i, c, ie, nvb, nu: (ie[i], 0, live_chunk(i, c, nu))),
                pl.BlockSpec((None, 1, FF_CHUNK), lambda i, c, ie, nvb, nu: (ie[i], 0, nc + live_chunk(i, c, nu))),
                pl.BlockSpec((None, 1, d), lambda i, c, ie, nvb, nu: (ie[i], 0, 0)),
            ],
            out_specs=pl.BlockSpec((EXPERT_ITEM, half), lambda i, c, ie, nvb, nu: (i, 0)),
            scratch_shapes=[
                pltpu.VMEM((EXPERT_ITEM, d), F32),
                pltpu.VMEM((d, FF_CHUNK), F32),
                pltpu.VMEM((d, FF_CHUNK), F32),
                pltpu.VMEM((FF_CHUNK, d), F32),
                pltpu.VMEM((d, FF_CHUNK), BF16),
                pltpu.VMEM((d, FF_CHUNK), BF16),
                pltpu.VMEM((FF_CHUNK, d), BF16),
                pltpu.SemaphoreType.DMA((3,)),
            ],
        ),
        out_shape=jax.ShapeDtypeStruct((n_rows, half), jnp.uint32),
        compiler_params=pltpu.CompilerParams(dimension_semantics=("arbitrary", "arbitrary"),
                                             vmem_limit_bytes=MOE_VMEM_LIMIT_BYTES),
        name="moe_mlp",
    )(item_expert, item_nvb, n_used, x_sorted, w1, w2, b1, b1, b2)


def _final_body(h_ref, moe_ref, p_ref, gp_ref, wg_ref, wp_ref, gf_ref, o_ref, *, final_norm):
    h = h_ref[...] + moe_ref[...]
    xn = (_rms_scale(h) * gp_ref[...]).astype(BF16)
    gate = jax.nn.sigmoid(jnp.dot(xn, wg_ref[...], preferred_element_type=F32))
    ple = jnp.dot(p_ref[...].astype(BF16), wp_ref[...], preferred_element_type=F32)
    h = h + gate * ple
    o_ref[...] = _rms_scale(h) * gf_ref[...] if final_norm else h


def _final(h1, moe, p_all, layer, g_ple, w_ple_gate, w_ple, g_final, *, final_norm, tm=512):
    n, d = h1.shape
    first = layer * (n // tm)
    return pl.pallas_call(
        functools.partial(_final_body, final_norm=final_norm),
        grid=(n // tm,),
        in_specs=[
            pl.BlockSpec((tm, d), lambda i: (i, 0)),
            pl.BlockSpec((tm, d), lambda i: (i, 0)),
            pl.BlockSpec((tm, PLE_DIM), lambda i: (first + i, 0)),
            pl.BlockSpec((1, d), lambda i: (0, 0)),
            _resident((d, d), lambda i: (0, 0)),
            _resident((PLE_DIM, d), lambda i: (0, 0)),
            pl.BlockSpec((1, d), lambda i: (0, 0)),
        ],
        out_specs=pl.BlockSpec((tm, d), lambda i: (i, 0)),
        out_shape=jax.ShapeDtypeStruct((n, d), F32),
        compiler_params=_params("arbitrary"),
        name="final",
    )(h1, moe, p_all, g_ple, w_ple_gate, w_ple, g_final)


def _routing_tables(expert_ids, rank, counts, n_tokens):
    nk = n_tokens * TOP_K
    sub = EXPERT_ITEM // EXPERT_BLOCK
    padded = (counts + EXPERT_ITEM - 1) // EXPERT_ITEM * EXPERT_ITEM
    pend = jnp.cumsum(padded)
    pstart = pend - padded
    hit = expert_ids[None, :, :] == jnp.arange(N_EXPERTS, dtype=jnp.int32)[:, None, None]
    dest = (jnp.sum(jnp.where(hit, pstart[:, None, None], 0), axis=0) + rank).T
    n_items = nk // EXPERT_ITEM + N_EXPERTS
    n_used = (pend[-1] // EXPERT_ITEM).astype(jnp.int32)
    item_start = jnp.arange(n_items, dtype=jnp.int32) * EXPERT_ITEM
    item_expert = jnp.minimum(jnp.searchsorted(pend, item_start, side='right'), N_EXPERTS - 1).astype(jnp.int32)
    used = jnp.arange(n_items) < n_used
    item_expert = jnp.where(used, item_expert, item_expert[n_used - 1])
    rows_left = counts[item_expert] - (item_start - pstart[item_expert])
    item_nvb = jnp.where(used, jnp.clip((rows_left + EXPERT_BLOCK - 1) // EXPERT_BLOCK, 0, sub), 0).astype(jnp.int32)
    pos = dest.reshape(n_tokens, TOP_K).astype(jnp.int32)
    return pos, item_expert, item_nvb, n_used.reshape(1), n_items * EXPERT_ITEM


def _layer(h, p_all, layer, norm_mix_g, w_in, b_in, conv_w, conv_b, w_rg_a, b_rg_a, w_rg_x, b_rg_x, lru_lambda,
           attn_sinks, w_attn_proj, w_lru_proj, w_out, norm_ffn_g, w_router, b_router, w_mlp1, b_mlp1,
           w_mlp2, b_mlp2, norm_ple_g, w_ple, w_ple_gate, norm_final_g, *, batch, seq, final_norm):
    n = h.shape[0]
    row = lambda v: v.reshape(1, -1)

    kv0 = Q_WIDTH
    rest0 = Q_WIDTH + 2 * KV_WIDTH
    reorder = lambda a: jnp.concatenate([a[..., :kv0], a[..., rest0:], a[..., kv0:rest0]], axis=-1)
    z = _inproj(h, row(norm_mix_g), reorder(w_in.astype(BF16)), row(reorder(b_in)))

    kv_col = (Q_WIDTH + 2 * LRU_WIDTH + 2 * D_MODEL) // (2 * KV_WIDTH)
    y_attn = _attention(z, attn_sinks, batch=batch, seq=seq, q_col=0, kv_col=kv_col)

    w_rg = jnp.concatenate([w_rg_a, w_rg_x], axis=-1).astype(BF16)
    b_rg = jnp.concatenate([b_rg_a, b_rg_x], axis=-1).reshape(LRU_BLOCKS, 1, 2 * LRU_BLOCK_WIDTH)
    y_lru = _rglru(z, conv_w, row(conv_b), w_rg, b_rg, row(lru_lambda), batch=batch, seq=seq,
                   u_col=Q_WIDTH // LRU_BLOCK_WIDTH, gate_col=(Q_WIDTH + LRU_WIDTH) // LRU_BLOCK_WIDTH)

    merged = _merge(y_attn, y_lru, z, w_attn_proj.astype(BF16), w_lru_proj.astype(BF16),
                    ga_col=(Q_WIDTH + 2 * LRU_WIDTH) // D_MODEL, gl_col=(Q_WIDTH + 2 * LRU_WIDTH + D_MODEL) // D_MODEL)
    h1, xn2, expert_ids, expert_w, rank, counts = _outproj_router(
        h, merged, w_out.astype(BF16), row(norm_ffn_g), w_router.T.astype(BF16), b_router.reshape(N_EXPERTS, 1))

    pos, item_expert, item_nvb, n_used, n_sorted = _routing_tables(
        expert_ids, rank, counts[:, 0].astype(jnp.int32), n)
    x_sorted = _dispatch(xn2, pos, n_sorted)
    y_sorted = _moe_mlp(x_sorted, w_mlp1, b_mlp1[:, None, :], w_mlp2, b_mlp2[:, None, :], item_expert, item_nvb, n_used)
    moe = _combine(y_sorted, pos, expert_w.T)

    return _final(h1, moe, p_all, layer, row(norm_ple_g), w_ple_gate.astype(BF16), w_ple.astype(BF16),
                  row(norm_final_g), final_norm=final_norm)


def kernel(x, p, norm_mix_g, w_in, b_in, conv_w, conv_b, w_rg_a, b_rg_a, w_rg_x, b_rg_x, lru_lambda, attn_sinks, w_attn_proj, w_lru_proj, w_out, norm_ffn_g, w_router, b_router, w_mlp1, b_mlp1, w_mlp2, b_mlp2, norm_ple_g, w_ple, w_ple_gate, norm_final_g):
    batch, seq, d = x.shape
    depth = p.shape[0]
    h = x.reshape(batch * seq, d)
    for l in range(depth):
        h = _layer(h, p.reshape(depth * batch * seq, -1), l, norm_mix_g[l], w_in[l], b_in[l], conv_w[l], conv_b[l],
                   w_rg_a[l], b_rg_a[l], w_rg_x[l], b_rg_x[l], lru_lambda[l], attn_sinks[l], w_attn_proj[l],
                   w_lru_proj[l], w_out[l], norm_ffn_g[l], w_router[l], b_router[l], w_mlp1[l], b_mlp1[l],
                   w_mlp2[l], b_mlp2[l], norm_ple_g[l], w_ple[l], w_ple_gate[l], norm_final_g,
                   batch=batch, seq=seq, final_norm=(l == depth - 1))
    return h.reshape(batch, seq, d)
```

```python
import functools

import jax
import jax.numpy as jnp
from jax import lax
from jax.experimental import pallas as pl
from jax.experimental.pallas import tpu as pltpu

D_MODEL = 2048
N_Q_HEADS = 32
N_KV_HEADS = 4
HEAD_DIM = 64
Q_WIDTH = N_Q_HEADS * HEAD_DIM
KV_WIDTH = N_KV_HEADS * HEAD_DIM
WINDOW = 128
LRU_WIDTH = D_MODEL
LRU_BLOCKS = 8
LRU_BLOCK_WIDTH = LRU_WIDTH // LRU_BLOCKS
CONV_WIDTH = 4
LRU_C = 8.0
IN_WIDTH = Q_WIDTH + 2 * KV_WIDTH + 2 * LRU_WIDTH + 2 * D_MODEL
N_EXPERTS = 32
TOP_K = 4
D_EXPERT = D_MODEL
SWIGLU_LIMIT = 7.0
SWIGLU_ALPHA = 1.702
EXPERT_BLOCK = 256
EXPERT_ITEM = 1280
FF_CHUNK = 512
PLE_DIM = 256
RMS_EPS = 1e-6
GATE_CHUNK = 512
LOG2_E = 1.4426950408889634

VMEM_LIMIT_BYTES = 56 * 1024 * 1024
MOE_VMEM_LIMIT_BYTES = 60 * 1024 * 1024
BF16_SUBLANES = 16

F32 = jnp.float32
BF16 = jnp.bfloat16


def _params(*sem):
    return pltpu.CompilerParams(dimension_semantics=sem, vmem_limit_bytes=VMEM_LIMIT_BYTES)


def _resident(shape, index_map):
    return pl.BlockSpec(shape, index_map, pipeline_mode=pl.Buffered(1))


def _rms_scale(x):
    var = jnp.mean(x * x, axis=-1, keepdims=True)
    return x * lax.rsqrt(var + RMS_EPS)


def _pack_bf16_pairs(x):
    c = x.shape[1] // 2
    xb = x.astype(BF16).astype(F32)
    lo = lax.bitcast_convert_type(xb[:, :c], jnp.uint32) >> 16
    hi = lax.bitcast_convert_type(xb[:, c:], jnp.uint32) & jnp.uint32(0xFFFF0000)
    return hi | lo


def _unpack_bf16_pairs(w):
    lo = lax.bitcast_convert_type(w << 16, F32)
    hi = lax.bitcast_convert_type(w & jnp.uint32(0xFFFF0000), F32)
    return lo, hi


def _inproj_body(x_ref, g_ref, w_ref, b_ref, o_ref, xn_ref):
    @pl.when(pl.program_id(1) == 0)
    def _():
        xn_ref[...] = (_rms_scale(x_ref[...]) * g_ref[...]).astype(BF16)

    acc = jnp.dot(xn_ref[...], w_ref[...], preferred_element_type=F32)
    o_ref[...] = (acc + b_ref[...]).astype(o_ref.dtype)


def _inproj(x2, g, w, b, *, tm=1024, tn=1536):
    n, d = x2.shape
    width = w.shape[1]
    return pl.pallas_call(
        _inproj_body,
        grid=(n // tm, width // tn),
        in_specs=[
            pl.BlockSpec((tm, d), lambda i, j: (i, 0)),
            pl.BlockSpec((1, d), lambda i, j: (0, 0)),
            pl.BlockSpec((d, tn), lambda i, j: (0, j)),
            pl.BlockSpec((1, tn), lambda i, j: (0, j)),
        ],
        out_specs=pl.BlockSpec((tm, tn), lambda i, j: (i, j)),
        out_shape=jax.ShapeDtypeStruct((n, width), BF16),
        scratch_shapes=[pltpu.VMEM((tm, d), BF16)],
        compiler_params=_params("arbitrary", "arbitrary"),
        name="inproj",
    )(x2, g, w, b)


def _block_diag_pair(pair, odd, lo_mask):
    swapped = jnp.concatenate([pair[:, HEAD_DIM:], pair[:, :HEAD_DIM]], axis=1)
    zero = jnp.zeros_like(pair)
    if odd:
        lo = jnp.where(lo_mask, swapped, zero)
        hi = jnp.where(lo_mask, zero, pair)
    else:
        lo = jnp.where(lo_mask, pair, zero)
        hi = jnp.where(lo_mask, zero, swapped)
    return jnp.concatenate([lo, hi], axis=0)


def _attn_body(sinks_ref, q_ref, kvp_ref, kvc_ref, o_ref, *, tq):
    nblk = tq // WINDOW
    group = N_Q_HEADS // N_KV_HEADS
    pairs = group // 2
    rows = pairs * WINDOW
    band_w = 2 * WINDOW
    s_idx = pl.program_id(1)

    lo_mask = lax.broadcasted_iota(jnp.int32, (1, 2 * HEAD_DIM), 1) < HEAD_DIM
    qi = lax.broadcasted_iota(jnp.int32, (rows, band_w), 0) % WINDOW
    kj = lax.broadcasted_iota(jnp.int32, (rows, band_w), 1)
    in_band = (kj > qi) & (kj <= qi + WINDOW)
    first_key = jnp.where(s_idx > 0, 0, WINDOW)
    row_pair = lax.broadcasted_iota(jnp.int32, (rows, 1), 0) // WINDOW
    neg = jnp.finfo(F32).min

    for i in range(nblk):
        if i == 0:
            band = jnp.concatenate([kvp_ref[...], kvc_ref[0:WINDOW, :]], axis=0)
            mask = in_band & (kj >= first_key)
        else:
            band = kvc_ref[(i - 1) * WINDOW:(i + 1) * WINDOW, :]
            mask = in_band
        for kh in range(N_KV_HEADS):
            col = (kh // 2) * 2 * HEAD_DIM
            kk = _block_diag_pair(band[:, col:col + 2 * HEAD_DIM], kh % 2 == 1, lo_mask)
            vv = _block_diag_pair(band[:, KV_WIDTH + col:KV_WIDTH + col + 2 * HEAD_DIM], kh % 2 == 1, lo_mask)
            qbase = kh * group * HEAD_DIM
            q4 = jnp.concatenate(
                [q_ref[i * WINDOW:(i + 1) * WINDOW, qbase + 2 * HEAD_DIM * j:qbase + 2 * HEAD_DIM * (j + 1)]
                 for j in range(pairs)], axis=0)
            s = lax.dot_general(q4, kk, (((1,), (1,)), ((), ())), preferred_element_type=F32)
            s = s * (HEAD_DIM ** -0.5 * LOG2_E)
            sink_lo = jnp.zeros((rows, 1), F32)
            sink_hi = jnp.zeros((rows, 1), F32)
            for j in range(pairs):
                sink_lo = jnp.where(row_pair == j, sinks_ref[kh * group + 2 * j] * LOG2_E, sink_lo)
                sink_hi = jnp.where(row_pair == j, sinks_ref[kh * group + 2 * j + 1] * LOG2_E, sink_hi)
            halves = []
            inv = []
            for half, sink in ((0, sink_lo), (1, sink_hi)):
                sh = jnp.where(mask, s[:, half * band_w:(half + 1) * band_w], neg)
                m = jnp.maximum(jnp.max(sh, axis=1, keepdims=True), sink)
                p = jnp.exp2(sh - m)
                denom = jnp.sum(p, axis=1, keepdims=True) + jnp.exp2(sink - m)
                halves.append(p.astype(BF16))
                inv.append(1.0 / denom)
            pcat = jnp.concatenate(halves, axis=1)
            o = jnp.dot(pcat, vv, preferred_element_type=F32)
            o = o * jnp.where(lo_mask, inv[0], inv[1])
            for j in range(pairs):
                o_ref[i * WINDOW:(i + 1) * WINDOW, qbase + 2 * HEAD_DIM * j:qbase + 2 * HEAD_DIM * (j + 1)] = (
                    o[j * WINDOW:(j + 1) * WINDOW, :].astype(o_ref.dtype))


def _attention(z, sinks, *, batch, seq, q_col, kv_col, tq=512):
    n = z.shape[0]
    spb = seq // tq
    wpb = seq // WINDOW
    kvw = 2 * KV_WIDTH

    def prev_map(b, s, sinks_ref):
        return (jnp.maximum(b * wpb + s * (tq // WINDOW) - 1, 0), kv_col)

    return pl.pallas_call(
        functools.partial(_attn_body, tq=tq),
        grid_spec=pltpu.PrefetchScalarGridSpec(
            num_scalar_prefetch=1,
            grid=(batch, spb),
            in_specs=[
                pl.BlockSpec((tq, Q_WIDTH), lambda b, s, sinks_ref: (b * spb + s, q_col)),
                pl.BlockSpec((WINDOW, kvw), prev_map),
                pl.BlockSpec((tq, kvw), lambda b, s, sinks_ref: (b * spb + s, kv_col)),
            ],
            out_specs=pl.BlockSpec((tq, Q_WIDTH), lambda b, s, sinks_ref: (b * spb + s, 0)),
        ),
        out_shape=jax.ShapeDtypeStruct((n, Q_WIDTH), BF16),
        compiler_params=_params("arbitrary", "arbitrary"),
        name="attn",
    )(sinks, z, z, z)


def _gelu_tanh(x):
    c = (2.0 / jnp.pi) ** 0.5
    return x * (0.5 * (1.0 + jnp.tanh(c * (x + 0.044715 * (x * x * x)))))


def _lru_body(u_ref, up_ref, gate_ref, cw_ref, cb_ref, wrg_ref, brg_ref, lam_ref, y_ref,
              ext_ref, a_ref, b_ref, h_ref):
    s_idx = pl.program_id(2)
    tl, w = u_ref.shape
    pad = 8

    @pl.when(s_idx == 0)
    def _():
        ext_ref[0:pad, :] = jnp.zeros((pad, w), F32)
        h_ref[...] = jnp.zeros_like(h_ref)

    @pl.when(s_idx > 0)
    def _():
        ext_ref[0:pad, :] = up_ref[...].astype(F32)[BF16_SUBLANES - pad:BF16_SUBLANES, :]

    ext_ref[pad:pad + tl, :] = u_ref[...].astype(F32)
    first = pad - (CONV_WIDTH - 1)
    uc = cb_ref[...] + ext_ref[first:first + tl, :] * cw_ref[0:1, :]
    for tap in range(1, CONV_WIDTH):
        uc = uc + ext_ref[first + tap:first + tap + tl, :] * cw_ref[tap:tap + 1, :]

    gates = jnp.dot(uc.astype(BF16), wrg_ref[...], preferred_element_type=F32) + brg_ref[...]
    r = jax.nn.sigmoid(gates[:, :w])
    ig = jax.nn.sigmoid(gates[:, w:])
    neg_lam = -lam_ref[...]
    softplus = jnp.maximum(neg_lam, 0.0) + jnp.log1p(jnp.exp(-jnp.abs(neg_lam)))
    log_a = (-LRU_C * r) * softplus
    a = jnp.exp(log_a)
    m2 = -jnp.tanh(log_a) * (a * a + 1.0)
    mult = jnp.where(m2 > 0.0, m2 * lax.rsqrt(m2), 0.0)
    row = lax.broadcasted_iota(jnp.int32, (tl, 1), 0)
    mult = jnp.where((row == 0) & (s_idx == 0), 1.0, mult)
    b = mult * (ig * uc)

    sub = row % 8
    for d in (1, 2, 4):
        keep = sub >= d
        b_prev = jnp.where(keep, pltpu.roll(b, d, 0), 0.0)
        a_prev = jnp.where(keep, pltpu.roll(a, d, 0), 1.0)
        b = b + a * b_prev
        a = a * a_prev
    a_ref[...] = a
    b_ref[...] = b

    def group_step(gi, h):
        off = pl.multiple_of(gi * 8, 8)
        hg = b_ref[pl.ds(off, 8), :] + a_ref[pl.ds(off, 8), :] * h
        b_ref[pl.ds(off, 8), :] = hg
        return hg[7:8, :]

    h_ref[...] = lax.fori_loop(0, tl // 8, group_step, h_ref[...])
    y_ref[...] = (b_ref[...] * _gelu_tanh(gate_ref[...].astype(F32))).astype(y_ref.dtype)


def _rglru(z, conv_w, conv_b, w_rg, b_rg, lam, *, batch, seq, u_col, gate_col, tl=512):
    n = z.shape[0]
    w = LRU_BLOCK_WIDTH
    spb = seq // tl
    rpb = seq // BF16_SUBLANES

    def prev_map(b, c, s):
        return (jnp.maximum(b * rpb + s * (tl // BF16_SUBLANES) - 1, 0), u_col + c)

    return pl.pallas_call(
        _lru_body,
        grid=(batch, LRU_BLOCKS, spb),
        in_specs=[
            pl.BlockSpec((tl, w), lambda b, c, s: (b * spb + s, u_col + c)),
            pl.BlockSpec((BF16_SUBLANES, w), prev_map),
            pl.BlockSpec((tl, w), lambda b, c, s: (b * spb + s, gate_col + c)),
            pl.BlockSpec((CONV_WIDTH, w), lambda b, c, s: (0, c)),
            pl.BlockSpec((1, w), lambda b, c, s: (0, c)),
            pl.BlockSpec((None, w, 2 * w), lambda b, c, s: (c, 0, 0)),
            pl.BlockSpec((None, 1, 2 * w), lambda b, c, s: (c, 0, 0)),
            pl.BlockSpec((1, w), lambda b, c, s: (0, c)),
        ],
        out_specs=pl.BlockSpec((tl, w), lambda b, c, s: (b * spb + s, c)),
        out_shape=jax.ShapeDtypeStruct((n, LRU_WIDTH), BF16),
        scratch_shapes=[
            pltpu.VMEM((tl + 8, w), F32),
            pltpu.VMEM((tl, w), F32),
            pltpu.VMEM((tl, w), F32),
            pltpu.VMEM((1, w), F32),
        ],
        compiler_params=_params("arbitrary", "arbitrary", "arbitrary"),
        name="rglru",
    )(z, z, z, conv_w, conv_b, w_rg, b_rg, lam)


def _merge_body(ya_ref, yl_ref, *refs):
    n_chunks = (len(refs) - 3) // 2
    ga_refs, gl_refs = refs[:n_chunks], refs[n_chunks:2 * n_chunks]
    pa_ref, pl_ref, o_ref = refs[2 * n_chunks:]
    pa = jnp.dot(ya_ref[...], pa_ref[...], preferred_element_type=F32)
    pr = jnp.dot(yl_ref[...], pl_ref[...], preferred_element_type=F32)
    ga = jnp.concatenate([r[...] for r in ga_refs], axis=1).astype(F32)
    gl = jnp.concatenate([r[...] for r in gl_refs], axis=1).astype(F32)
    merged = jax.nn.sigmoid(ga) * pa + jax.nn.sigmoid(gl) * pr
    o_ref[...] = merged.astype(o_ref.dtype)


def _merge(y_attn, y_lru, z, w_attn_proj, w_lru_proj, *, ga_col, gl_col, tm=512):
    n, d = y_attn.shape
    n_chunks = d // GATE_CHUNK

    def gate_specs(col):
        return [pl.BlockSpec((tm, GATE_CHUNK), functools.partial(lambda i, blk: (i, blk), blk=col // GATE_CHUNK + c))
                for c in range(n_chunks)]

    return pl.pallas_call(
        _merge_body,
        grid=(n // tm,),
        in_specs=[
            pl.BlockSpec((tm, d), lambda i: (i, 0)),
            pl.BlockSpec((tm, d), lambda i: (i, 0)),
            *gate_specs(ga_col),
            *gate_specs(gl_col),
            _resident((d, d), lambda i: (0, 0)),
            _resident((d, d), lambda i: (0, 0)),
        ],
        out_specs=pl.BlockSpec((tm, d), lambda i: (i, 0)),
        out_shape=jax.ShapeDtypeStruct((n, d), BF16),
        compiler_params=_params("arbitrary"),
        name="merge",
    )(y_attn, y_lru, *([z] * (2 * n_chunks)), w_attn_proj, w_lru_proj)


def _outproj_body(x_ref, m_ref, wo_ref, g_ref, wr_ref, br_ref, tri_ref, h_ref, xn_ref, idx_ref, wt_ref, rank_ref,
                  cnt_ref):
    @pl.when(pl.program_id(0) == 0)
    def _():
        cnt_ref[...] = jnp.zeros_like(cnt_ref)

    h = x_ref[...] + jnp.dot(m_ref[...], wo_ref[...], preferred_element_type=F32)
    h_ref[...] = h
    xn = _rms_scale(h) * g_ref[...]
    xn_ref[...] = _pack_bf16_pairs(xn)
    logits = lax.dot_general(wr_ref[...], xn.astype(BF16), (((1,), (1,)), ((), ())),
                             preferred_element_type=F32) + br_ref[...]
    expert = lax.broadcasted_iota(jnp.int32, logits.shape, 0).astype(F32)
    vals, sels = [], []
    for k in range(TOP_K):
        m = jnp.max(logits, axis=0, keepdims=True)
        sel = jnp.min(jnp.where(logits == m, expert, float(N_EXPERTS)), axis=0, keepdims=True)
        vals.append(m)
        sels.append(sel)
        idx_ref[k:k + 1, :] = sel.astype(jnp.int32)
        logits = jnp.where(expert == sel, -jnp.inf, logits)
    exps = [jnp.exp(v - vals[0]) for v in vals]
    denom = exps[0] + exps[1] + exps[2] + exps[3]
    for k in range(TOP_K):
        wt_ref[k:k + 1, :] = exps[k] / denom

    chosen = jnp.zeros(logits.shape, F32)
    for sel in sels:
        chosen = chosen + jnp.where(expert == sel, 1.0, 0.0)
    before = jnp.dot(chosen.astype(BF16), tri_ref[...], preferred_element_type=F32) + cnt_ref[:, 0:1]
    for k in range(TOP_K):
        rank = jnp.sum(jnp.where(expert == sels[k], before, 0.0), axis=0, keepdims=True)
        rank_ref[k:k + 1, :] = rank.astype(jnp.int32)
    cnt_ref[...] = cnt_ref[...] + jnp.sum(chosen, axis=1, keepdims=True)


def _outproj_router(x2, merged, w_out, g_ffn, w_router_t, b_router, *, tm=512):
    n, d = x2.shape
    lane = 128
    earlier = (jnp.arange(tm)[:, None] < jnp.arange(tm)[None, :]).astype(BF16)
    return pl.pallas_call(
        _outproj_body,
        grid=(n // tm,),
        in_specs=[
            pl.BlockSpec((tm, d), lambda i: (i, 0)),
            pl.BlockSpec((tm, d), lambda i: (i, 0)),
            _resident((d, d), lambda i: (0, 0)),
            pl.BlockSpec((1, d), lambda i: (0, 0)),
            pl.BlockSpec((N_EXPERTS, d), lambda i: (0, 0)),
            pl.BlockSpec((N_EXPERTS, 1), lambda i: (0, 0)),
            pl.BlockSpec((tm, tm), lambda i: (0, 0)),
        ],
        out_specs=[
            pl.BlockSpec((tm, d), lambda i: (i, 0)),
            pl.BlockSpec((tm, d // 2), lambda i: (i, 0)),
            pl.BlockSpec((TOP_K, tm), lambda i: (0, i)),
            pl.BlockSpec((TOP_K, tm), lambda i: (0, i)),
            pl.BlockSpec((TOP_K, tm), lambda i: (0, i)),
            pl.BlockSpec((N_EXPERTS, lane), lambda i: (0, 0)),
        ],
        out_shape=[
            jax.ShapeDtypeStruct((n, d), F32),
            jax.ShapeDtypeStruct((n, d // 2), jnp.uint32),
            jax.ShapeDtypeStruct((TOP_K, n), jnp.int32),
            jax.ShapeDtypeStruct((TOP_K, n), F32),
            jax.ShapeDtypeStruct((TOP_K, n), jnp.int32),
            jax.ShapeDtypeStruct((N_EXPERTS, lane), F32),
        ],
        compiler_params=_params("arbitrary"),
        name="outproj_router",
    )(x2, merged, w_out, g_ffn, w_router_t, b_router, earlier)


def _row_copy(src, dst, sem, src_row, dst_row):
    return pltpu.make_async_copy(src.at[pl.ds(src_row, 1), :], dst.at[pl.ds(dst_row, 1), :], sem)


def _dispatch_body(dst_ref, x_ref, init_hbm, out_hbm, stage, sems):
    del init_hbm
    i = pl.program_id(0)
    n = pl.num_programs(0)
    c = dst_ref.shape[1]
    slot = i % 2
    stage[slot] = x_ref[...]
    for r in range(c):
        _row_copy(stage.at[slot], out_hbm, sems.at[slot], r // TOP_K, dst_ref[0, r]).start()

    @pl.when(i > 0)
    def _():
        for r in range(c):
            _row_copy(stage.at[1 - slot], out_hbm, sems.at[1 - slot], 0, 0).wait()

    @pl.when(i == n - 1)
    def _():
        for r in range(c):
            _row_copy(stage.at[slot], out_hbm, sems.at[slot], 0, 0).wait()


def _dispatch(rows, pos, n_sorted, *, tm=256):
    n, w = rows.shape
    dst3 = pos.reshape(n // tm, 1, tm * TOP_K)
    init = jnp.zeros((n_sorted, w), rows.dtype)
    return pl.pallas_call(
        _dispatch_body,
        grid=(n // tm,),
        in_specs=[
            pl.BlockSpec((None, 1, tm * TOP_K), lambda i: (i, 0, 0), memory_space=pltpu.SMEM),
            pl.BlockSpec((tm, w), lambda i: (i, 0)),
            pl.BlockSpec(memory_space=pl.ANY),
        ],
        out_specs=pl.BlockSpec(memory_space=pl.ANY),
        out_shape=jax.ShapeDtypeStruct((n_sorted, w), rows.dtype),
        scratch_shapes=[pltpu.VMEM((2, tm, w), rows.dtype), pltpu.SemaphoreType.DMA((2,))],
        input_output_aliases={2: 0},
        compiler_params=_params("arbitrary"),
        name="dispatch",
    )(dst3, rows, init)


def _combine_body(idx_ref, nxt_ref, wt_ref, src_hbm, o_ref, buf, sems):
    i = pl.program_id(0)
    n = pl.num_programs(0)
    tm = o_ref.shape[0]
    count = TOP_K * tm
    slot = i % 2

    @pl.when(i == 0)
    def _():
        for r in range(count):
            _row_copy(src_hbm, buf.at[0], sems.at[0], idx_ref[0, r], r).start()

    @pl.when(i + 1 < n)
    def _():
        for r in range(count):
            _row_copy(src_hbm, buf.at[1 - slot], sems.at[1 - slot], nxt_ref[0, r], r).start()

    for r in range(count):
        _row_copy(src_hbm, buf.at[slot], sems.at[slot], 0, r).wait()

    c = o_ref.shape[1] // 2
    acc_lo = jnp.zeros((tm, c), F32)
    acc_hi = jnp.zeros((tm, c), F32)
    for k in range(TOP_K):
        lo, hi = _unpack_bf16_pairs(buf[slot, k * tm:(k + 1) * tm, :])
        acc_lo = acc_lo + wt_ref[:, k:k + 1] * lo
        acc_hi = acc_hi + wt_ref[:, k:k + 1] * hi
    o_ref[:, :c] = acc_lo
    o_ref[:, c:] = acc_hi


def _combine(y_sorted, pos, weights, *, tm=256):
    n = pos.shape[0]
    w = y_sorted.shape[1]
    steps = n // tm
    idx3 = pos.reshape(steps, tm, TOP_K).transpose(0, 2, 1).reshape(steps, 1, TOP_K * tm)
    return pl.pallas_call(
        _combine_body,
        grid=(steps,),
        in_specs=[
            pl.BlockSpec((None, 1, TOP_K * tm), lambda i: (i, 0, 0), memory_space=pltpu.SMEM),
            pl.BlockSpec((None, 1, TOP_K * tm), lambda i: (jnp.minimum(i + 1, steps - 1), 0, 0),
                         memory_space=pltpu.SMEM),
            pl.BlockSpec((tm, TOP_K), lambda i: (i, 0)),
            pl.BlockSpec(memory_space=pl.ANY),
        ],
        out_specs=pl.BlockSpec((tm, 2 * w), lambda i: (i, 0)),
        out_shape=jax.ShapeDtypeStruct((n, 2 * w), F32),
        scratch_shapes=[pltpu.VMEM((2, TOP_K * tm, w), y_sorted.dtype), pltpu.SemaphoreType.DMA((2,))],
        compiler_params=_params("arbitrary"),
        name="combine",
    )(idx3, idx3, weights, y_sorted)


def _moe_mlp_body(ie_ref, nvb_ref, nu_ref, x_ref, w1_hbm, w2_hbm, bg_ref, bl_ref, b2_ref, o_ref,
                  acc_ref, sg, sl, s2, wg_ref, wl_ref, w2_ref, sems):
    item = pl.program_id(0)
    chunk = pl.program_id(1)
    n_items = pl.num_programs(0)
    n_chunks = pl.num_programs(1)
    nvb = nvb_ref[item]
    f = w2_hbm.shape[1]
    fc = sg.shape[1]

    def weight_copies(it, ch):
        e = ie_ref[it]
        c0 = pl.multiple_of(ch * fc, fc)
        return (pltpu.make_async_copy(w1_hbm.at[e, :, pl.ds(c0, fc)], sg, sems.at[0]),
                pltpu.make_async_copy(w1_hbm.at[e, :, pl.ds(f + c0, fc)], sl, sems.at[1]),
                pltpu.make_async_copy(w2_hbm.at[e, pl.ds(c0, fc), :], s2, sems.at[2]))

    @pl.when(chunk == 0)
    def _():
        acc_ref[...] = jnp.broadcast_to(b2_ref[...], acc_ref.shape)

    @pl.when((item == 0) & (chunk == 0))
    def _():
        for cp in weight_copies(item, chunk):
            cp.start()

    @pl.when(nvb > 0)
    def _():
        for cp in weight_copies(item, chunk):
            cp.wait()
        wg_ref[...] = sg[...].astype(BF16)
        wl_ref[...] = sl[...].astype(BF16)
        w2_ref[...] = s2[...].astype(BF16)

        same_item = chunk + 1 < n_chunks
        nxt_item = jnp.minimum(jnp.where(same_item, item, item + 1), n_items - 1)
        nxt_chunk = jnp.where(same_item, chunk + 1, 0)

        @pl.when(same_item | ((item + 1 < n_items) & (nvb_ref[nxt_item] > 0)))
        def _():
            for cp in weight_copies(nxt_item, nxt_chunk):
                cp.start()

        def run_rows(first_block, n_blocks):
            rows = slice(first_block * EXPERT_BLOCK, (first_block + n_blocks) * EXPERT_BLOCK)
            lo, hi = _unpack_bf16_pairs(x_ref[rows, :])
            xj = jnp.concatenate([lo.astype(BF16), hi.astype(BF16)], axis=1)
            glu = jnp.dot(xj, wg_ref[...], preferred_element_type=F32) + bg_ref[...]
            lin = jnp.dot(xj, wl_ref[...], preferred_element_type=F32) + bl_ref[...]
            glu = jnp.minimum(glu, SWIGLU_LIMIT)
            lin = jnp.clip(lin, -SWIGLU_LIMIT, SWIGLU_LIMIT)
            act = glu * jax.nn.sigmoid(SWIGLU_ALPHA * glu) * (lin + 1.0)
            acc_ref[rows, :] += jnp.dot(act.astype(BF16), w2_ref[...], preferred_element_type=F32)

        n_sub = x_ref.shape[0] // EXPERT_BLOCK
        for j in range(0, n_sub, 2):
            if j + 2 <= n_sub:
                pl.when(nvb >= j + 2)(functools.partial(run_rows, j, 2))
                pl.when(nvb == j + 1)(functools.partial(run_rows, j, 1))
            else:
                pl.when(nvb >= j + 1)(functools.partial(run_rows, j, 1))

    @pl.when(chunk == n_chunks - 1)
    def _():
        o_ref[...] = _pack_bf16_pairs(acc_ref[...])


def _moe_mlp(x_sorted, w1, b1, w2, b2, item_expert, item_nvb, n_used):
    n_rows, half = x_sorted.shape
    d = 2 * half
    n_items = n_rows // EXPERT_ITEM
    f = w2.shape[1]
    nc = f // FF_CHUNK

    def live_chunk(i, c, nu):
        return jnp.where(i < nu[0], c, nc - 1)

    def rows_map(i, c, ie, nvb, nu):
        return (jnp.minimum(i, nu[0] - 1), 0)

    return pl.pallas_call(
        _moe_mlp_body,
        grid_spec=pltpu.PrefetchScalarGridSpec(
            num_scalar_prefetch=3,
            grid=(n_items, nc),
            in_specs=[
                pl.BlockSpec((EXPERT_ITEM, half), rows_map),
                pl.BlockSpec(memory_space=pl.ANY),
                pl.BlockSpec(memory_space=pl.ANY),
                pl.BlockSpec((None, 1, FF_CHUNK), lambda i, c, ie, nvb, nu: (ie[i], 0, live_chunk(i, c, nu))),
                pl.BlockSpec((None, 1, FF_CHUNK), lambda i, c, ie, nvb, nu: (ie[i], 0, nc + live_chunk(i, c, nu))),
                pl.BlockSpec((None, 1, d), lambda i, c, ie, nvb, nu: (ie[i], 0, 0)),
            ],
            out_specs=pl.BlockSpec((EXPERT_ITEM, half), lambda i, c, ie, nvb, nu: (i, 0)),
            scratch_shapes=[
                pltpu.VMEM((EXPERT_ITEM, d), F32),
                pltpu.VMEM((d, FF_CHUNK), F32),
                pltpu.VMEM((d, FF_CHUNK), F32),
                pltpu.VMEM((FF_CHUNK, d), F32),
                pltpu.VMEM((d, FF_CHUNK), BF16),
                pltpu.VMEM((d, FF_CHUNK), BF16),
                pltpu.VMEM((FF_CHUNK, d), BF16),
                pltpu.SemaphoreType.DMA((3,)),
            ],
        ),
        out_shape=jax.ShapeDtypeStruct((n_rows, half), jnp.uint32),
        compiler_params=pltpu.CompilerParams(dimension_semantics=("arbitrary", "arbitrary"),
                                             vmem_limit_bytes=MOE_VMEM_LIMIT_BYTES),
        name="moe_mlp",
    )(item_expert, item_nvb, n_used, x_sorted, w1, w2, b1, b1, b2)


def _final_body(h_ref, moe_ref, p_ref, gp_ref, wg_ref, wp_ref, gf_ref, o_ref, *, final_norm):
    h = h_ref[...] + moe_ref[...]
    xn = (_rms_scale(h) * gp_ref[...]).astype(BF16)
    gate = jax.nn.sigmoid(jnp.dot(xn, wg_ref[...], preferred_element_type=F32))
    ple = jnp.dot(p_ref[...].astype(BF16), wp_ref[...], preferred_element_type=F32)
    h = h + gate * ple
    o_ref[...] = _rms_scale(h) * gf_ref[...] if final_norm else h


def _final(h1, moe, p_all, layer, g_ple, w_ple_gate, w_ple, g_final, *, final_norm, tm=512):
    n, d = h1.shape
    first = layer * (n // tm)
    return pl.pallas_call(
        functools.partial(_final_body, final_norm=final_norm),
        grid=(n // tm,),
        in_specs=[
            pl.BlockSpec((tm, d), lambda i: (i, 0)),
            pl.BlockSpec((tm, d), lambda i: (i, 0)),
            pl.BlockSpec((tm, PLE_DIM), lambda i: (first + i, 0)),
            pl.BlockSpec((1, d), lambda i: (0, 0)),
            _resident((d, d), lambda i: (0, 0)),
            _resident((PLE_DIM, d), lambda i: (0, 0)),
            pl.BlockSpec((1, d), lambda i: (0, 0)),
        ],
        out_specs=pl.BlockSpec((tm, d), lambda i: (i, 0)),
        out_shape=jax.ShapeDtypeStruct((n, d), F32),
        compiler_params=_params("arbitrary"),
        name="final",
    )(h1, moe, p_all, g_ple, w_ple_gate, w_ple, g_final)


def _routing_tables(expert_ids, rank, counts, n_tokens):
    nk = n_tokens * TOP_K
    sub = EXPERT_ITEM // EXPERT_BLOCK
    padded = (counts + EXPERT_ITEM - 1) // EXPERT_ITEM * EXPERT_ITEM
    pend = jnp.cumsum(padded)
    pstart = pend - padded
    hit = expert_ids[None, :, :] == jnp.arange(N_EXPERTS, dtype=jnp.int32)[:, None, None]
    dest = (jnp.sum(jnp.where(hit, pstart[:, None, None], 0), axis=0) + rank).T
    n_items = nk // EXPERT_ITEM + N_EXPERTS
    n_used = (pend[-1] // EXPERT_ITEM).astype(jnp.int32)
    item_start = jnp.arange(n_items, dtype=jnp.int32) * EXPERT_ITEM
    item_expert = jnp.minimum(jnp.searchsorted(pend, item_start, side='right'), N_EXPERTS - 1).astype(jnp.int32)
    used = jnp.arange(n_items) < n_used
    item_expert = jnp.where(used, item_expert, item_expert[n_used - 1])
    rows_left = counts[item_expert] - (item_start - pstart[item_expert])
    item_nvb = jnp.where(used, jnp.clip((rows_left + EXPERT_BLOCK - 1) // EXPERT_BLOCK, 0, sub), 0).astype(jnp.int32)
    pos = dest.reshape(n_tokens, TOP_K).astype(jnp.int32)
    return pos, item_expert, item_nvb, n_used.reshape(1), n_items * EXPERT_ITEM


def _layer(h, p_all, layer, norm_mix_g, w_in, b_in, conv_w, conv_b, w_rg_a, b_rg_a, w_rg_x, b_rg_x, lru_lambda,
           attn_sinks, w_attn_proj, w_lru_proj, w_out, norm_ffn_g, w_router, b_router, w_mlp1, b_mlp1,
           w_mlp2, b_mlp2, norm_ple_g, w_ple, w_ple_gate, norm_final_g, *, batch, seq, final_norm):
    n = h.shape[0]
    row = lambda v: v.reshape(1, -1)

    z = _inproj(h, row(norm_mix_g), w_in.astype(BF16), row(b_in))

    u0 = Q_WIDTH + 2 * KV_WIDTH
    y_attn = _attention(z, attn_sinks, batch=batch, seq=seq, q_col=0, kv_col=Q_WIDTH // (2 * KV_WIDTH))

    w_rg = jnp.concatenate([w_rg_a, w_rg_x], axis=-1).astype(BF16)
    b_rg = jnp.concatenate([b_rg_a, b_rg_x], axis=-1).reshape(LRU_BLOCKS, 1, 2 * LRU_BLOCK_WIDTH)
    y_lru = _rglru(z, conv_w, row(conv_b), w_rg, b_rg, row(lru_lambda), batch=batch, seq=seq,
                   u_col=u0 // LRU_BLOCK_WIDTH, gate_col=(u0 + LRU_WIDTH) // LRU_BLOCK_WIDTH)

    merged = _merge(y_attn, y_lru, z, w_attn_proj.astype(BF16), w_lru_proj.astype(BF16),
                    ga_col=u0 + 2 * LRU_WIDTH, gl_col=u0 + 2 * LRU_WIDTH + D_MODEL)
    h1, xn2, expert_ids, expert_w, rank, counts = _outproj_router(
        h, merged, w_out.astype(BF16), row(norm_ffn_g), w_router.T.astype(BF16), b_router.reshape(N_EXPERTS, 1))

    pos, item_expert, item_nvb, n_used, n_sorted = _routing_tables(
        expert_ids, rank, counts[:, 0].astype(jnp.int32), n)
    x_sorted = _dispatch(xn2, pos, n_sorted)
    y_sorted = _moe_mlp(x_sorted, w_mlp1, b_mlp1[:, None, :], w_mlp2, b_mlp2[:, None, :], item_expert, item_nvb, n_used)
    moe = _combine(y_sorted, pos, expert_w.T)

    return _final(h1, moe, p_all, layer, row(norm_ple_g), w_ple_gate.astype(BF16), w_ple.astype(BF16),
                  row(norm_final_g), final_norm=final_norm)


def kernel(x, p, norm_mix_g, w_in, b_in, conv_w, conv_b, w_rg_a, b_rg_a, w_rg_x, b_rg_x, lru_lambda, attn_sinks, w_attn_proj, w_lru_proj, w_out, norm_ffn_g, w_router, b_router, w_mlp1, b_mlp1, w_mlp2, b_mlp2, norm_ple_g, w_ple, w_ple_gate, norm_final_g):
    batch, seq, d = x.shape
    depth = p.shape[0]
    h = x.reshape(batch * seq, d)
    for l in range(depth):
        h = _layer(h, p.reshape(depth * batch * seq, -1), l, norm_mix_g[l], w_in[l], b_in[l], conv_w[l], conv_b[l],
                   w_rg_a[l], b_rg_a[l], w_rg_x[l], b_rg_x[l], lru_lambda[l], attn_sinks[l], w_attn_proj[l],
                   w_lru_proj[l], w_out[l], norm_ffn_g[l], w_router[l], b_router[l], w_mlp1[l], b_mlp1[l],
                   w_mlp2[l], b_mlp2[l], norm_ple_g[l], w_ple[l], w_ple_gate[l], norm_final_g,
                   batch=batch, seq=seq, final_norm=(l == depth - 1))
    return h.reshape(batch, seq, d)
```

```python
import functools

import jax
import jax.numpy as jnp
from jax import lax
from jax.experimental import pallas as pl
from jax.experimental.pallas import tpu as pltpu

D_MODEL = 2048
N_Q_HEADS = 32
N_KV_HEADS = 4
HEAD_DIM = 64
Q_WIDTH = N_Q_HEADS * HEAD_DIM
KV_WIDTH = N_KV_HEADS * HEAD_DIM
WINDOW = 128
LRU_WIDTH = D_MODEL
LRU_BLOCKS = 8
LRU_BLOCK_WIDTH = LRU_WIDTH // LRU_BLOCKS
CONV_WIDTH = 4
LRU_C = 8.0
IN_WIDTH = Q_WIDTH + 2 * KV_WIDTH + 2 * LRU_WIDTH + 2 * D_MODEL
N_EXPERTS = 32
TOP_K = 4
D_EXPERT = D_MODEL
SWIGLU_LIMIT = 7.0
SWIGLU_ALPHA = 1.702
EXPERT_BLOCK = 256
EXPERT_ITEM = 1280
FF_CHUNK = 512
PLE_DIM = 256
RMS_EPS = 1e-6
GATE_CHUNK = 512
LOG2_E = 1.4426950408889634

VMEM_LIMIT_BYTES = 56 * 1024 * 1024
MOE_VMEM_LIMIT_BYTES = 60 * 1024 * 1024
BF16_SUBLANES = 16

F32 = jnp.float32
BF16 = jnp.bfloat16


def _params(*sem):
    return pltpu.CompilerParams(dimension_semantics=sem, vmem_limit_bytes=VMEM_LIMIT_BYTES)


def _resident(shape, index_map):
    return pl.BlockSpec(shape, index_map, pipeline_mode=pl.Buffered(1))


def _rms_scale(x):
    var = jnp.mean(x * x, axis=-1, keepdims=True)
    return x * lax.rsqrt(var + RMS_EPS)


def _pack_bf16_pairs(x):
    c = x.shape[1] // 2
    xb = x.astype(BF16).astype(F32)
    lo = lax.bitcast_convert_type(xb[:, :c], jnp.uint32) >> 16
    hi = lax.bitcast_convert_type(xb[:, c:], jnp.uint32) & jnp.uint32(0xFFFF0000)
    return hi | lo


def _unpack_bf16_pairs(w):
    lo = lax.bitcast_convert_type(w << 16, F32)
    hi = lax.bitcast_convert_type(w & jnp.uint32(0xFFFF0000), F32)
    return lo, hi


def _inproj_body(x_ref, g_ref, w_ref, b_ref, o_ref, xn_ref):
    @pl.when(pl.program_id(1) == 0)
    def _():
        xn_ref[...] = (_rms_scale(x_ref[...]) * g_ref[...]).astype(BF16)

    acc = jnp.dot(xn_ref[...], w_ref[...], preferred_element_type=F32)
    o_ref[...] = (acc + b_ref[...]).astype(o_ref.dtype)


def _inproj(x2, g, w, b, *, tm=1024, tn=1536):
    n, d = x2.shape
    width = w.shape[1]
    return pl.pallas_call(
        _inproj_body,
        grid=(n // tm, width // tn),
        in_specs=[
            pl.BlockSpec((tm, d), lambda i, j: (i, 0)),
            pl.BlockSpec((1, d), lambda i, j: (0, 0)),
            pl.BlockSpec((d, tn), lambda i, j: (0, j)),
            pl.BlockSpec((1, tn), lambda i, j: (0, j)),
        ],
        out_specs=pl.BlockSpec((tm, tn), lambda i, j: (i, j)),
        out_shape=jax.ShapeDtypeStruct((n, width), BF16),
        scratch_shapes=[pltpu.VMEM((tm, d), BF16)],
        compiler_params=_params("arbitrary", "arbitrary"),
        name="inproj",
    )(x2, g, w, b)


def _block_diag_pair(pair, odd, lo_mask):
    swapped = jnp.concatenate([pair[:, HEAD_DIM:], pair[:, :HEAD_DIM]], axis=1)
    zero = jnp.zeros_like(pair)
    if odd:
        lo = jnp.where(lo_mask, swapped, zero)
        hi = jnp.where(lo_mask, zero, pair)
    else:
        lo = jnp.where(lo_mask, pair, zero)
        hi = jnp.where(lo_mask, zero, swapped)
    return jnp.concatenate([lo, hi], axis=0)


def _attn_body(sinks_ref, q_ref, kvp_ref, kvc_ref, o_ref, *, tq):
    nblk = tq // WINDOW
    group = N_Q_HEADS // N_KV_HEADS
    pairs = group // 2
    rows = pairs * WINDOW
    band_w = 2 * WINDOW
    s_idx = pl.program_id(1)

    lo_mask = lax.broadcasted_iota(jnp.int32, (1, 2 * HEAD_DIM), 1) < HEAD_DIM
    qi = lax.broadcasted_iota(jnp.int32, (rows, band_w), 0) % WINDOW
    kj = lax.broadcasted_iota(jnp.int32, (rows, band_w), 1)
    in_band = (kj > qi) & (kj <= qi + WINDOW)
    first_key = jnp.where(s_idx > 0, 0, WINDOW)
    row_pair = lax.broadcasted_iota(jnp.int32, (rows, 1), 0) // WINDOW
    neg = jnp.finfo(F32).min

    for i in range(nblk):
        if i == 0:
            band = jnp.concatenate([kvp_ref[...], kvc_ref[0:WINDOW, :]], axis=0)
            mask = in_band & (kj >= first_key)
        else:
            band = kvc_ref[(i - 1) * WINDOW:(i + 1) * WINDOW, :]
            mask = in_band
        for kh in range(N_KV_HEADS):
            col = (kh // 2) * 2 * HEAD_DIM
            kk = _block_diag_pair(band[:, col:col + 2 * HEAD_DIM], kh % 2 == 1, lo_mask)
            vv = _block_diag_pair(band[:, KV_WIDTH + col:KV_WIDTH + col + 2 * HEAD_DIM], kh % 2 == 1, lo_mask)
            qbase = kh * group * HEAD_DIM
            q4 = jnp.concatenate(
                [q_ref[i * WINDOW:(i + 1) * WINDOW, qbase + 2 * HEAD_DIM * j:qbase + 2 * HEAD_DIM * (j + 1)]
                 for j in range(pairs)], axis=0)
            s = lax.dot_general(q4, kk, (((1,), (1,)), ((), ())), preferred_element_type=F32)
            s = s * (HEAD_DIM ** -0.5 * LOG2_E)
            sink_lo = jnp.zeros((rows, 1), F32)
            sink_hi = jnp.zeros((rows, 1), F32)
            for j in range(pairs):
                sink_lo = jnp.where(row_pair == j, sinks_ref[kh * group + 2 * j] * LOG2_E, sink_lo)
                sink_hi = jnp.where(row_pair == j, sinks_ref[kh * group + 2 * j + 1] * LOG2_E, sink_hi)
            halves = []
            inv = []
            for half, sink in ((0, sink_lo), (1, sink_hi)):
                sh = jnp.where(mask, s[:, half * band_w:(half + 1) * band_w], neg)
                m = jnp.maximum(jnp.max(sh, axis=1, keepdims=True), sink)
                p = jnp.exp2(sh - m)
                denom = jnp.sum(p, axis=1, keepdims=True) + jnp.exp2(sink - m)
                halves.append(p.astype(BF16))
                inv.append(1.0 / denom)
            pcat = jnp.concatenate(halves, axis=1)
            o = jnp.dot(pcat, vv, preferred_element_type=F32)
            o = o * jnp.where(lo_mask, inv[0], inv[1])
            for j in range(pairs):
                o_ref[i * WINDOW:(i + 1) * WINDOW, qbase + 2 * HEAD_DIM * j:qbase + 2 * HEAD_DIM * (j + 1)] = (
                    o[j * WINDOW:(j + 1) * WINDOW, :].astype(o_ref.dtype))


def _attention(z, sinks, *, batch, seq, q_col, kv_col, tq=512):
    n = z.shape[0]
    spb = seq // tq
    wpb = seq // WINDOW
    kvw = 2 * KV_WIDTH

    def prev_map(b, s, sinks_ref):
        return (jnp.maximum(b * wpb + s * (tq // WINDOW) - 1, 0), kv_col)

    return pl.pallas_call(
        functools.partial(_attn_body, tq=tq),
        grid_spec=pltpu.PrefetchScalarGridSpec(
            num_scalar_prefetch=1,
            grid=(batch, spb),
            in_specs=[
                pl.BlockSpec((tq, Q_WIDTH), lambda b, s, sinks_ref: (b * spb + s, q_col)),
                pl.BlockSpec((WINDOW, kvw), prev_map),
                pl.BlockSpec((tq, kvw), lambda b, s, sinks_ref: (b * spb + s, kv_col)),
            ],
            out_specs=pl.BlockSpec((tq, Q_WIDTH), lambda b, s, sinks_ref: (b * spb + s, 0)),
        ),
        out_shape=jax.ShapeDtypeStruct((n, Q_WIDTH), BF16),
        compiler_params=_params("arbitrary", "arbitrary"),
        name="attn",
    )(sinks, z, z, z)


def _gelu_tanh(x):
    c = (2.0 / jnp.pi) ** 0.5
    return x * (0.5 * (1.0 + jnp.tanh(c * (x + 0.044715 * (x * x * x)))))


def _lru_body(u_ref, up_ref, gate_ref, cw_ref, cb_ref, wrg_ref, brg_ref, lam_ref, y_ref,
              ext_ref, a_ref, b_ref, h_ref):
    s_idx = pl.program_id(2)
    tl, w = u_ref.shape
    pad = 8

    @pl.when(s_idx == 0)
    def _():
        ext_ref[0:pad, :] = jnp.zeros((pad, w), F32)
        h_ref[...] = jnp.zeros_like(h_ref)

    @pl.when(s_idx > 0)
    def _():
        ext_ref[0:pad, :] = up_ref[...].astype(F32)[BF16_SUBLANES - pad:BF16_SUBLANES, :]

    ext_ref[pad:pad + tl, :] = u_ref[...].astype(F32)
    first = pad - (CONV_WIDTH - 1)
    uc = cb_ref[...] + ext_ref[first:first + tl, :] * cw_ref[0:1, :]
    for tap in range(1, CONV_WIDTH):
        uc = uc + ext_ref[first + tap:first + tap + tl, :] * cw_ref[tap:tap + 1, :]

    gates = jnp.dot(uc.astype(BF16), wrg_ref[...], preferred_element_type=F32) + brg_ref[...]
    r = jax.nn.sigmoid(gates[:, :w])
    ig = jax.nn.sigmoid(gates[:, w:])
    neg_lam = -lam_ref[...]
    softplus = jnp.maximum(neg_lam, 0.0) + jnp.log1p(jnp.exp(-jnp.abs(neg_lam)))
    log_a = (-LRU_C * r) * softplus
    a = jnp.exp(log_a)
    m2 = -jnp.tanh(log_a) * (a * a + 1.0)
    mult = jnp.where(m2 > 0.0, m2 * lax.rsqrt(m2), 0.0)
    row = lax.broadcasted_iota(jnp.int32, (tl, 1), 0)
    mult = jnp.where((row == 0) & (s_idx == 0), 1.0, mult)
    b = mult * (ig * uc)

    sub = row % 8
    for d in (1, 2, 4):
        keep = sub >= d
        b_prev = jnp.where(keep, pltpu.roll(b, d, 0), 0.0)
        a_prev = jnp.where(keep, pltpu.roll(a, d, 0), 1.0)
        b = b + a * b_prev
        a = a * a_prev
    a_ref[...] = a
    b_ref[...] = b

    def group_step(gi, h):
        off = pl.multiple_of(gi * 8, 8)
        hg = b_ref[pl.ds(off, 8), :] + a_ref[pl.ds(off, 8), :] * h
        b_ref[pl.ds(off, 8), :] = hg
        return hg[7:8, :]

    h_ref[...] = lax.fori_loop(0, tl // 8, group_step, h_ref[...])
    y_ref[...] = (b_ref[...] * _gelu_tanh(gate_ref[...].astype(F32))).astype(y_ref.dtype)


def _rglru(z, conv_w, conv_b, w_rg, b_rg, lam, *, batch, seq, u_col, gate_col, tl=1024):
    n = z.shape[0]
    w = LRU_BLOCK_WIDTH
    spb = seq // tl
    rpb = seq // BF16_SUBLANES

    def prev_map(b, c, s):
        return (jnp.maximum(b * rpb + s * (tl // BF16_SUBLANES) - 1, 0), u_col + c)

    return pl.pallas_call(
        _lru_body,
        grid=(batch, LRU_BLOCKS, spb),
        in_specs=[
            pl.BlockSpec((tl, w), lambda b, c, s: (b * spb + s, u_col + c)),
            pl.BlockSpec((BF16_SUBLANES, w), prev_map),
            pl.BlockSpec((tl, w), lambda b, c, s: (b * spb + s, gate_col + c)),
            pl.BlockSpec((CONV_WIDTH, w), lambda b, c, s: (0, c)),
            pl.BlockSpec((1, w), lambda b, c, s: (0, c)),
            pl.BlockSpec((None, w, 2 * w), lambda b, c, s: (c, 0, 0)),
            pl.BlockSpec((None, 1, 2 * w), lambda b, c, s: (c, 0, 0)),
            pl.BlockSpec((1, w), lambda b, c, s: (0, c)),
        ],
        out_specs=pl.BlockSpec((tl, w), lambda b, c, s: (b * spb + s, c)),
        out_shape=jax.ShapeDtypeStruct((n, LRU_WIDTH), BF16),
        scratch_shapes=[
            pltpu.VMEM((tl + 8, w), F32),
            pltpu.VMEM((tl, w), F32),
            pltpu.VMEM((tl, w), F32),
            pltpu.VMEM((1, w), F32),
        ],
        compiler_params=_params("arbitrary", "arbitrary", "arbitrary"),
        name="rglru",
    )(z, z, z, conv_w, conv_b, w_rg, b_rg, lam)


def _merge_body(ya_ref, yl_ref, *refs):
    n_chunks = (len(refs) - 3) // 2
    ga_refs, gl_refs = refs[:n_chunks], refs[n_chunks:2 * n_chunks]
    pa_ref, pl_ref, o_ref = refs[2 * n_chunks:]
    pa = jnp.dot(ya_ref[...], pa_ref[...], preferred_element_type=F32)
    pr = jnp.dot(yl_ref[...], pl_ref[...], preferred_element_type=F32)
    ga = jnp.concatenate([r[...] for r in ga_refs], axis=1).astype(F32)
    gl = jnp.concatenate([r[...] for r in gl_refs], axis=1).astype(F32)
    merged = jax.nn.sigmoid(ga) * pa + jax.nn.sigmoid(gl) * pr
    o_ref[...] = merged.astype(o_ref.dtype)


def _merge(y_attn, y_lru, z, w_attn_proj, w_lru_proj, *, ga_col, gl_col, tm=512):
    n, d = y_attn.shape
    n_chunks = d // GATE_CHUNK

    def gate_specs(col):
        return [pl.BlockSpec((tm, GATE_CHUNK), functools.partial(lambda i, blk: (i, blk), blk=col // GATE_CHUNK + c))
                for c in range(n_chunks)]

    return pl.pallas_call(
        _merge_body,
        grid=(n // tm,),
        in_specs=[
            pl.BlockSpec((tm, d), lambda i: (i, 0)),
            pl.BlockSpec((tm, d), lambda i: (i, 0)),
            *gate_specs(ga_col),
            *gate_specs(gl_col),
            _resident((d, d), lambda i: (0, 0)),
            _resident((d, d), lambda i: (0, 0)),
        ],
        out_specs=pl.BlockSpec((tm, d), lambda i: (i, 0)),
        out_shape=jax.ShapeDtypeStruct((n, d), BF16),
        compiler_params=_params("arbitrary"),
        name="merge",
    )(y_attn, y_lru, *([z] * (2 * n_chunks)), w_attn_proj, w_lru_proj)


def _outproj_body(x_ref, m_ref, wo_ref, g_ref, wr_ref, br_ref, tri_ref, h_ref, xn_ref, idx_ref, wt_ref, rank_ref,
                  cnt_ref):
    @pl.when(pl.program_id(0) == 0)
    def _():
        cnt_ref[...] = jnp.zeros_like(cnt_ref)

    h = x_ref[...] + jnp.dot(m_ref[...], wo_ref[...], preferred_element_type=F32)
    h_ref[...] = h
    xn = _rms_scale(h) * g_ref[...]
    xn_ref[...] = _pack_bf16_pairs(xn)
    logits = lax.dot_general(wr_ref[...], xn.astype(BF16), (((1,), (1,)), ((), ())),
                             preferred_element_type=F32) + br_ref[...]
    expert = lax.broadcasted_iota(jnp.int32, logits.shape, 0).astype(F32)
    vals, sels = [], []
    for k in range(TOP_K):
        m = jnp.max(logits, axis=0, keepdims=True)
        sel = jnp.min(jnp.where(logits == m, expert, float(N_EXPERTS)), axis=0, keepdims=True)
        vals.append(m)
        sels.append(sel)
        idx_ref[k:k + 1, :] = sel.astype(jnp.int32)
        logits = jnp.where(expert == sel, -jnp.inf, logits)
    exps = [jnp.exp(v - vals[0]) for v in vals]
    denom = exps[0] + exps[1] + exps[2] + exps[3]
    for k in range(TOP_K):
        wt_ref[k:k + 1, :] = exps[k] / denom

    chosen = jnp.zeros(logits.shape, F32)
    for sel in sels:
        chosen = chosen + jnp.where(expert == sel, 1.0, 0.0)
    before = jnp.dot(chosen.astype(BF16), tri_ref[...], preferred_element_type=F32) + cnt_ref[:, 0:1]
    for k in range(TOP_K):
        rank = jnp.sum(jnp.where(expert == sels[k], before, 0.0), axis=0, keepdims=True)
        rank_ref[k:k + 1, :] = rank.astype(jnp.int32)
    cnt_ref[...] = cnt_ref[...] + jnp.sum(chosen, axis=1, keepdims=True)


def _outproj_router(x2, merged, w_out, g_ffn, w_router_t, b_router, *, tm=512):
    n, d = x2.shape
    lane = 128
    earlier = (jnp.arange(tm)[:, None] < jnp.arange(tm)[None, :]).astype(BF16)
    return pl.pallas_call(
        _outproj_body,
        grid=(n // tm,),
        in_specs=[
            pl.BlockSpec((tm, d), lambda i: (i, 0)),
            pl.BlockSpec((tm, d), lambda i: (i, 0)),
            _resident((d, d), lambda i: (0, 0)),
            pl.BlockSpec((1, d), lambda i: (0, 0)),
            pl.BlockSpec((N_EXPERTS, d), lambda i: (0, 0)),
            pl.BlockSpec((N_EXPERTS, 1), lambda i: (0, 0)),
            pl.BlockSpec((tm, tm), lambda i: (0, 0)),
        ],
        out_specs=[
            pl.BlockSpec((tm, d), lambda i: (i, 0)),
            pl.BlockSpec((tm, d // 2), lambda i: (i, 0)),
            pl.BlockSpec((TOP_K, tm), lambda i: (0, i)),
            pl.BlockSpec((TOP_K, tm), lambda i: (0, i)),
            pl.BlockSpec((TOP_K, tm), lambda i: (0, i)),
            pl.BlockSpec((N_EXPERTS, lane), lambda i: (0, 0)),
        ],
        out_shape=[
            jax.ShapeDtypeStruct((n, d), F32),
            jax.ShapeDtypeStruct((n, d // 2), jnp.uint32),
            jax.ShapeDtypeStruct((TOP_K, n), jnp.int32),
            jax.ShapeDtypeStruct((TOP_K, n), F32),
            jax.ShapeDtypeStruct((TOP_K, n), jnp.int32),
            jax.ShapeDtypeStruct((N_EXPERTS, lane), F32),
        ],
        compiler_params=_params("arbitrary"),
        name="outproj_router",
    )(x2, merged, w_out, g_ffn, w_router_t, b_router, earlier)


def _row_copy(src, dst, sem, src_row, dst_row):
    return pltpu.make_async_copy(src.at[pl.ds(src_row, 1), :], dst.at[pl.ds(dst_row, 1), :], sem)


def _dispatch_body(dst_ref, x_ref, init_hbm, out_hbm, stage, sems):
    del init_hbm
    i = pl.program_id(0)
    n = pl.num_programs(0)
    c = dst_ref.shape[1]
    slot = i % 2
    stage[slot] = x_ref[...]
    for r in range(c):
        _row_copy(stage.at[slot], out_hbm, sems.at[slot], r // TOP_K, dst_ref[0, r]).start()

    @pl.when(i > 0)
    def _():
        for r in range(c):
            _row_copy(stage.at[1 - slot], out_hbm, sems.at[1 - slot], 0, 0).wait()

    @pl.when(i == n - 1)
    def _():
        for r in range(c):
            _row_copy(stage.at[slot], out_hbm, sems.at[slot], 0, 0).wait()


def _dispatch(rows, pos, n_sorted, *, tm=256):
    n, w = rows.shape
    dst3 = pos.reshape(n // tm, 1, tm * TOP_K)
    init = jnp.zeros((n_sorted, w), rows.dtype)
    return pl.pallas_call(
        _dispatch_body,
        grid=(n // tm,),
        in_specs=[
            pl.BlockSpec((None, 1, tm * TOP_K), lambda i: (i, 0, 0), memory_space=pltpu.SMEM),
            pl.BlockSpec((tm, w), lambda i: (i, 0)),
            pl.BlockSpec(memory_space=pl.ANY),
        ],
        out_specs=pl.BlockSpec(memory_space=pl.ANY),
        out_shape=jax.ShapeDtypeStruct((n_sorted, w), rows.dtype),
        scratch_shapes=[pltpu.VMEM((2, tm, w), rows.dtype), pltpu.SemaphoreType.DMA((2,))],
        input_output_aliases={2: 0},
        compiler_params=_params("arbitrary"),
        name="dispatch",
    )(dst3, rows, init)


def _combine_body(idx_ref, nxt_ref, wt_ref, src_hbm, o_ref, buf, sems):
    i = pl.program_id(0)
    n = pl.num_programs(0)
    tm = o_ref.shape[0]
    count = TOP_K * tm
    slot = i % 2

    @pl.when(i == 0)
    def _():
        for r in range(count):
            _row_copy(src_hbm, buf.at[0], sems.at[0], idx_ref[0, r], r).start()

    @pl.when(i + 1 < n)
    def _():
        for r in range(count):
            _row_copy(src_hbm, buf.at[1 - slot], sems.at[1 - slot], nxt_ref[0, r], r).start()

    for r in range(count):
        _row_copy(src_hbm, buf.at[slot], sems.at[slot], 0, r).wait()

    c = o_ref.shape[1] // 2
    acc_lo = jnp.zeros((tm, c), F32)
    acc_hi = jnp.zeros((tm, c), F32)
    for k in range(TOP_K):
        lo, hi = _unpack_bf16_pairs(buf[slot, k * tm:(k + 1) * tm, :])
        acc_lo = acc_lo + wt_ref[:, k:k + 1] * lo
        acc_hi = acc_hi + wt_ref[:, k:k + 1] * hi
    o_ref[:, :c] = acc_lo
    o_ref[:, c:] = acc_hi


def _combine(y_sorted, pos, weights, *, tm=256):
    n = pos.shape[0]
    w = y_sorted.shape[1]
    steps = n // tm
    idx3 = pos.reshape(steps, tm, TOP_K).transpose(0, 2, 1).reshape(steps, 1, TOP_K * tm)
    return pl.pallas_call(
        _combine_body,
        grid=(steps,),
        in_specs=[
            pl.BlockSpec((None, 1, TOP_K * tm), lambda i: (i, 0, 0), memory_space=pltpu.SMEM),
            pl.BlockSpec((None, 1, TOP_K * tm), lambda i: (jnp.minimum(i + 1, steps - 1), 0, 0),
                         memory_space=pltpu.SMEM),
            pl.BlockSpec((tm, TOP_K), lambda i: (i, 0)),
            pl.BlockSpec(memory_space=pl.ANY),
        ],
        out_specs=pl.BlockSpec((tm, 2 * w), lambda i: (i, 0)),
        out_shape=jax.ShapeDtypeStruct((n, 2 * w), F32),
        scratch_shapes=[pltpu.VMEM((2, TOP_K * tm, w), y_sorted.dtype), pltpu.SemaphoreType.DMA((2,))],
        compiler_params=_params("arbitrary"),
        name="combine",
    )(idx3, idx3, weights, y_sorted)


def _moe_mlp_body(ie_ref, nvb_ref, nu_ref, x_ref, w1_hbm, w2_hbm, bg_ref, bl_ref, b2_ref, o_ref,
                  acc_ref, sg, sl, s2, wg_ref, wl_ref, w2_ref, sems):
    item = pl.program_id(0)
    chunk = pl.program_id(1)
    n_items = pl.num_programs(0)
    n_chunks = pl.num_programs(1)
    nvb = nvb_ref[item]
    f = w2_hbm.shape[1]
    fc = sg.shape[1]

    def weight_copies(it, ch):
        e = ie_ref[it]
        c0 = pl.multiple_of(ch * fc, fc)
        return (pltpu.make_async_copy(w1_hbm.at[e, :, pl.ds(c0, fc)], sg, sems.at[0]),
                pltpu.make_async_copy(w1_hbm.at[e, :, pl.ds(f + c0, fc)], sl, sems.at[1]),
                pltpu.make_async_copy(w2_hbm.at[e, pl.ds(c0, fc), :], s2, sems.at[2]))

    @pl.when(chunk == 0)
    def _():
        acc_ref[...] = jnp.broadcast_to(b2_ref[...], acc_ref.shape)

    @pl.when((item == 0) & (chunk == 0))
    def _():
        for cp in weight_copies(item, chunk):
            cp.start()

    @pl.when(nvb > 0)
    def _():
        for cp in weight_copies(item, chunk):
            cp.wait()
        wg_ref[...] = sg[...].astype(BF16)
        wl_ref[...] = sl[...].astype(BF16)
        w2_ref[...] = s2[...].astype(BF16)

        same_item = chunk + 1 < n_chunks
        nxt_item = jnp.minimum(jnp.where(same_item, item, item + 1), n_items - 1)
        nxt_chunk = jnp.where(same_item, chunk + 1, 0)

        @pl.when(same_item | ((item + 1 < n_items) & (nvb_ref[nxt_item] > 0)))
        def _():
            for cp in weight_copies(nxt_item, nxt_chunk):
                cp.start()

        def run_rows(first_block, n_blocks):
            rows = slice(first_block * EXPERT_BLOCK, (first_block + n_blocks) * EXPERT_BLOCK)
            lo, hi = _unpack_bf16_pairs(x_ref[rows, :])
            xj = jnp.concatenate([lo.astype(BF16), hi.astype(BF16)], axis=1)
            glu = jnp.dot(xj, wg_ref[...], preferred_element_type=F32) + bg_ref[...]
            lin = jnp.dot(xj, wl_ref[...], preferred_element_type=F32) + bl_ref[...]
            glu = jnp.minimum(glu, SWIGLU_LIMIT)
            lin = jnp.clip(lin, -SWIGLU_LIMIT, SWIGLU_LIMIT)
            act = glu * jax.nn.sigmoid(SWIGLU_ALPHA * glu) * (lin + 1.0)
            acc_ref[rows, :] += jnp.dot(act.astype(BF16), w2_ref[...], preferred_element_type=F32)

        n_sub = x_ref.shape[0] // EXPERT_BLOCK
        for j in range(0, n_sub, 2):
            if j + 2 <= n_sub:
                pl.when(nvb >= j + 2)(functools.partial(run_rows, j, 2))
                pl.when(nvb == j + 1)(functools.partial(run_rows, j, 1))
            else:
                pl.when(nvb >= j + 1)(functools.partial(run_rows, j, 1))

    @pl.when(chunk == n_chunks - 1)
    def _():
        o_ref[...] = _pack_bf16_pairs(acc_ref[...])


def _moe_mlp(x_sorted, w1, b1, w2, b2, item_expert, item_nvb, n_used):
    n_rows, half = x_sorted.shape
    d = 2 * half
    n_items = n_rows // EXPERT_ITEM
    f = w2.shape[1]
    nc = f // FF_CHUNK

    def live_chunk(i, c, nu):
        return jnp.where(i < nu[0], c, nc - 1)

    def rows_map(i, c, ie, nvb, nu):
        return (jnp.minimum(i, nu[0] - 1), 0)

    return pl.pallas_call(
        _moe_mlp_body,
        grid_spec=pltpu.PrefetchScalarGridSpec(
            num_scalar_prefetch=3,
            grid=(n_items, nc),
            in_specs=[
                pl.BlockSpec((EXPERT_ITEM, half), rows_map),
                pl.BlockSpec(memory_space=pl.ANY),
                pl.BlockSpec(memory_space=pl.ANY),
                pl.BlockSpec((None, 1, FF_CHUNK), lambda i, c, ie, nvb, nu: (ie[i], 0, live_chunk(i, c, nu))),
                pl.BlockSpec((None, 1, FF_CHUNK), lambda i, c, ie, nvb, nu: (ie[i], 0, nc + live_chunk(i, c, nu))),
                pl.BlockSpec((None, 1, d), lambda i, c, ie, nvb, nu: (ie[i], 0, 0)),
            ],
            out_specs=pl.BlockSpec((EXPERT_ITEM, half), lambda i, c, ie, nvb, nu: (i, 0)),
            scratch_shapes=[
                pltpu.VMEM((EXPERT_ITEM, d), F32),
                pltpu.VMEM((d, FF_CHUNK), F32),
                pltpu.VMEM((d, FF_CHUNK), F32),
                pltpu.VMEM((FF_CHUNK, d), F32),
                pltpu.VMEM((d, FF_CHUNK), BF16),
                pltpu.VMEM((d, FF_CHUNK), BF16),
                pltpu.VMEM((FF_CHUNK, d), BF16),
                pltpu.SemaphoreType.DMA((3,)),
            ],
        ),
        out_shape=jax.ShapeDtypeStruct((n_rows, half), jnp.uint32),
        compiler_params=pltpu.CompilerParams(dimension_semantics=("arbitrary", "arbitrary"),
                                             vmem_limit_bytes=MOE_VMEM_LIMIT_BYTES),
        name="moe_mlp",
    )(item_expert, item_nvb, n_used, x_sorted, w1, w2, b1, b1, b2)


def _final_body(h_ref, moe_ref, p_ref, gp_ref, wg_ref, wp_ref, gf_ref, o_ref, *, final_norm):
    h = h_ref[...] + moe_ref[...]
    xn = (_rms_scale(h) * gp_ref[...]).astype(BF16)
    gate = jax.nn.sigmoid(jnp.dot(xn, wg_ref[...], preferred_element_type=F32))
    ple = jnp.dot(p_ref[...].astype(BF16), wp_ref[...], preferred_element_type=F32)
    h = h + gate * ple
    o_ref[...] = _rms_scale(h) * gf_ref[...] if final_norm else h


def _final(h1, moe, p_all, layer, g_ple, w_ple_gate, w_ple, g_final, *, final_norm, tm=512):
    n, d = h1.shape
    first = layer * (n // tm)
    return pl.pallas_call(
        functools.partial(_final_body, final_norm=final_norm),
        grid=(n // tm,),
        in_specs=[
            pl.BlockSpec((tm, d), lambda i: (i, 0)),
            pl.BlockSpec((tm, d), lambda i: (i, 0)),
            pl.BlockSpec((tm, PLE_DIM), lambda i: (first + i, 0)),
            pl.BlockSpec((1, d), lambda i: (0, 0)),
            _resident((d, d), lambda i: (0, 0)),
            _resident((PLE_DIM, d), lambda i: (0, 0)),
            pl.BlockSpec((1, d), lambda i: (0, 0)),
        ],
        out_specs=pl.BlockSpec((tm, d), lambda i: (i, 0)),
        out_shape=jax.ShapeDtypeStruct((n, d), F32),
        compiler_params=_params("arbitrary"),
        name="final",
    )(h1, moe, p_all, g_ple, w_ple_gate, w_ple, g_final)


def _routing_tables(expert_ids, rank, counts, n_tokens):
    nk = n_tokens * TOP_K
    sub = EXPERT_ITEM // EXPERT_BLOCK
    padded = (counts + EXPERT_ITEM - 1) // EXPERT_ITEM * EXPERT_ITEM
    pend = jnp.cumsum(padded)
    pstart = pend - padded
    hit = expert_ids[None, :, :] == jnp.arange(N_EXPERTS, dtype=jnp.int32)[:, None, None]
    dest = (jnp.sum(jnp.where(hit, pstart[:, None, None], 0), axis=0) + rank).T
    n_items = nk // EXPERT_ITEM + N_EXPERTS
    n_used = (pend[-1] // EXPERT_ITEM).astype(jnp.int32)
    item_start = jnp.arange(n_items, dtype=jnp.int32) * EXPERT_ITEM
    item_expert = jnp.minimum(jnp.searchsorted(pend, item_start, side='right'), N_EXPERTS - 1).astype(jnp.int32)
    used = jnp.arange(n_items) < n_used
    item_expert = jnp.where(used, item_expert, item_expert[n_used - 1])
    rows_left = counts[item_expert] - (item_start - pstart[item_expert])
    item_nvb = jnp.where(used, jnp.clip((rows_left + EXPERT_BLOCK - 1) // EXPERT_BLOCK, 0, sub), 0).astype(jnp.int32)
    pos = dest.reshape(n_tokens, TOP_K).astype(jnp.int32)
    return pos, item_expert, item_nvb, n_used.reshape(1), n_items * EXPERT_ITEM


def _layer(h, p_all, layer, norm_mix_g, w_in, b_in, conv_w, conv_b, w_rg_a, b_rg_a, w_rg_x, b_rg_x, lru_lambda,
           attn_sinks, w_attn_proj, w_lru_proj, w_out, norm_ffn_g, w_router, b_router, w_mlp1, b_mlp1,
           w_mlp2, b_mlp2, norm_ple_g, w_ple, w_ple_gate, norm_final_g, *, batch, seq, final_norm):
    n = h.shape[0]
    row = lambda v: v.reshape(1, -1)

    z = _inproj(h, row(norm_mix_g), w_in.astype(BF16), row(b_in))

    u0 = Q_WIDTH + 2 * KV_WIDTH
    y_attn = _attention(z, attn_sinks, batch=batch, seq=seq, q_col=0, kv_col=Q_WIDTH // (2 * KV_WIDTH))

    w_rg = jnp.concatenate([w_rg_a, w_rg_x], axis=-1).astype(BF16)
    b_rg = jnp.concatenate([b_rg_a, b_rg_x], axis=-1).reshape(LRU_BLOCKS, 1, 2 * LRU_BLOCK_WIDTH)
    y_lru = _rglru(z, conv_w, row(conv_b), w_rg, b_rg, row(lru_lambda), batch=batch, seq=seq,
                   u_col=u0 // LRU_BLOCK_WIDTH, gate_col=(u0 + LRU_WIDTH) // LRU_BLOCK_WIDTH)

    merged = _merge(y_attn, y_lru, z, w_attn_proj.astype(BF16), w_lru_proj.astype(BF16),
                    ga_col=u0 + 2 * LRU_WIDTH, gl_col=u0 + 2 * LRU_WIDTH + D_MODEL)
    h1, xn2, expert_ids, expert_w, rank, counts = _outproj_router(
        h, merged, w_out.astype(BF16), row(norm_ffn_g), w_router.T.astype(BF16), b_router.reshape(N_EXPERTS, 1))

    pos, item_expert, item_nvb, n_used, n_sorted = _routing_tables(
        expert_ids, rank, counts[:, 0].astype(jnp.int32), n)
    x_sorted = _dispatch(xn2, pos, n_sorted)
    y_sorted = _moe_mlp(x_sorted, w_mlp1, b_mlp1[:, None, :], w_mlp2, b_mlp2[:, None, :], item_expert, item_nvb, n_used)
    moe = _combine(y_sorted, pos, expert_w.T)

    return _final(h1, moe, p_all, layer, row(norm_ple_g), w_ple_gate.astype(BF16), w_ple.astype(BF16),
                  row(norm_final_g), final_norm=final_norm)


def kernel(x, p, norm_mix_g, w_in, b_in, conv_w, conv_b, w_rg_a, b_rg_a, w_rg_x, b_rg_x, lru_lambda, attn_sinks, w_attn_proj, w_lru_proj, w_out, norm_ffn_g, w_router, b_router, w_mlp1, b_mlp1, w_mlp2, b_mlp2, norm_ple_g, w_ple, w_ple_gate, norm_final_g):
    batch, seq, d = x.shape
    depth = p.shape[0]
    h = x.reshape(batch * seq, d)
    for l in range(depth):
        h = _layer(h, p.reshape(depth * batch * seq, -1), l, norm_mix_g[l], w_in[l], b_in[l], conv_w[l], conv_b[l],
                   w_rg_a[l], b_rg_a[l], w_rg_x[l], b_rg_x[l], lru_lambda[l], attn_sinks[l], w_attn_proj[l],
                   w_lru_proj[l], w_out[l], norm_ffn_g[l], w_router[l], b_router[l], w_mlp1[l], b_mlp1[l],
                   w_mlp2[l], b_mlp2[l], norm_ple_g[l], w_ple[l], w_ple_gate[l], norm_final_g,
                   batch=batch, seq=seq, final_norm=(l == depth - 1))
    return h.reshape(batch, seq, d)
```
